```python
import math
import jax, jax.numpy as jnp
from jax import lax
import numpy as np

D_MODEL = 4096
BATCH = 1
SEQ = 8192
DEPTH = 1

CHUNK = 64
MIX_WIDTH = D_MODEL
SSM_WIDTH = MIX_WIDTH // 2
SSM_GROUP = 16
SSM_GROUPS = SSM_WIDTH // SSM_GROUP
SSM_STATE = 64
ATT_QK_DIM = 128
ATT_V_DIM = 2 * ATT_QK_DIM
ATT_WIDTH = MIX_WIDTH - SSM_WIDTH
ATT_HEADS = ATT_WIDTH // ATT_V_DIM
QK_WIDTH = ATT_HEADS * 2 * ATT_QK_DIM
IN_WIDTH = SSM_WIDTH + 2 * QK_WIDTH + ATT_WIDTH
D_FF = ((8 * D_MODEL // 3 + 255) // 256) * 256
CONV_WIDTH = 3
REL_BUCKETS = 32
REL_MAX_DIST = 128
Q_BLOCK = 128
ALPHA = (2 * DEPTH) ** 0.25
BETA = (8 * DEPTH) ** -0.25
LN_EPS = 1e-5
NEG_INF = -1e30

kernel_name = 'hybrid_s5_diffattn_convffn_deepnorm'


def layer_norm(x, g, b):
    xf = x.astype(jnp.float32)
    mu = jnp.mean(xf, axis=-1, keepdims=True)
    xc = xf - mu
    var = jnp.mean(xc * xc, axis=-1, keepdims=True)
    return (xc * lax.rsqrt(var + LN_EPS) * g.astype(jnp.float32) + b.astype(jnp.float32)).astype(x.dtype)


def ssm_mixer(u, log_step, lam_re, lam_im, b_re, b_im, c_re, c_im, d_skip, w_glu, b_glu):
    f32 = jnp.float32
    bsz, seq, _ = u.shape
    ug = u.reshape(bsz, seq, SSM_GROUPS, SSM_GROUP).astype(f32)
    step = jnp.exp(log_step.astype(f32))[:, None]
    lr = lam_re.astype(f32)
    li = lam_im.astype(f32)
    mag = jnp.exp(lr * step)
    ab_re = mag * jnp.cos(li * step)
    ab_im = mag * jnp.sin(li * step)
    den = lr * lr + li * li
    nr = ab_re - 1.0
    ni = ab_im
    z_re = (nr * lr + ni * li) / den
    z_im = (ni * lr - nr * li) / den
    br = b_re.astype(f32)
    bi = b_im.astype(f32)
    bb_re = z_re[..., None] * br - z_im[..., None] * bi
    bb_im = z_re[..., None] * bi + z_im[..., None] * br
    bu_re = jnp.einsum('bsgh,gnh->bsgn', ug, bb_re)
    bu_im = jnp.einsum('bsgh,gnh->bsgn', ug, bb_im)
    a_re = jnp.broadcast_to(ab_re, bu_re.shape)
    a_im = jnp.broadcast_to(ab_im, bu_im.shape)

    def combine(e1, e2):
        a1r, a1i, b1r, b1i = e1
        a2r, a2i, b2r, b2i = e2
        return (a2r * a1r - a2i * a1i,
                a2r * a1i + a2i * a1r,
                a2r * b1r - a2i * b1i + b2r,
                a2r * b1i + a2i * b1r + b2i)

    _, _, h_re, h_im = lax.associative_scan(combine, (a_re, a_im, bu_re, bu_im), axis=1)
    y = (jnp.einsum('ghn,bsgn->bsgh', c_re.astype(f32), h_re)
         - jnp.einsum('ghn,bsgn->bsgh', c_im.astype(f32), h_im))
    y = y + d_skip.astype(f32).reshape(SSM_GROUPS, SSM_GROUP) * ug
    y = jax.nn.gelu(y.reshape(bsz, seq, SSM_WIDTH).astype(u.dtype))
    return y * jax.nn.sigmoid(y @ w_glu + b_glu)


def t5_bucket(rel):
    half = REL_BUCKETS // 2
    max_exact = half // 2
    ret = jnp.where(rel > 0, half, 0)
    n = jnp.abs(rel)
    nf = jnp.maximum(n, 1).astype(jnp.float32)
    large = max_exact + (jnp.log(nf / max_exact) / math.log(REL_MAX_DIST / max_exact)
                         * (half - max_exact)).astype(jnp.int32)
    large = jnp.minimum(large, half - 1)
    return ret + jnp.where(n < max_exact, n, large)


def diff_attention(q, k, v, rel_bias, lam_q1, lam_k1, lam_q2, lam_k2, subln_g, lambda_init):
    f32 = jnp.float32
    bsz, seq = q.shape[0], q.shape[1]
    nb = seq // Q_BLOCK
    scale = ATT_QK_DIM ** -0.5
    lam = (jnp.exp(jnp.sum(lam_q1.astype(f32) * lam_k1.astype(f32)))
           - jnp.exp(jnp.sum(lam_q2.astype(f32) * lam_k2.astype(f32))) + lambda_init)
    kt = k.transpose(0, 2, 3, 1, 4)
    vt = v.transpose(0, 2, 1, 3)
    qb = (q * scale).reshape(bsz, nb, Q_BLOCK, ATT_HEADS, 2, ATT_QK_DIM).transpose(1, 0, 3, 4, 2, 5)
    kpos = jnp.arange(seq)

    def block(args):
        qblk, bidx = args
        qpos = bidx * Q_BLOCK + jnp.arange(Q_BLOCK)
        rel = kpos[None, :] - qpos[:, None]
        bias = rel_bias[t5_bucket(rel)].transpose(2, 0, 1).astype(f32)
        visible = (kpos[None, :] // CHUNK) <= (qpos[:, None] // CHUNK)
        s = jnp.einsum('bhcqd,bhckd->bhcqk', qblk, kt).astype(f32) + bias[None, :, None]
        s = jnp.where(visible, s, NEG_INF)
        p = jax.nn.softmax(s, axis=-1)
        attn = p[:, :, 0] - lam * p[:, :, 1]
        return jnp.einsum('bhqk,bhkd->bhqd', attn.astype(v.dtype), vt)

    o = lax.map(block, (qb, jnp.arange(nb)))
    o = o.transpose(1, 0, 3, 2, 4).reshape(bsz, seq, ATT_HEADS, ATT_V_DIM).astype(f32)
    o = o * lax.rsqrt(jnp.mean(o * o, axis=-1, keepdims=True) + LN_EPS) * subln_g.astype(f32)
    o = o * (1.0 - lambda_init)
    return o.reshape(bsz, seq, ATT_WIDTH).astype(v.dtype)


def conv_ffn(h, w_up, conv_w, conv_b, w_down):
    seq = h.shape[1]
    up = h @ w_up
    a, g = jnp.split(up, 2, axis=-1)
    gp = jnp.pad(g, ((0, 0), (CONV_WIDTH - 1, 0), (0, 0)))
    gc = conv_b
    for j in range(CONV_WIDTH):
        gc = gc + gp[:, j:j + seq] * conv_w[j]
    return (jax.nn.silu(gc) * a) @ w_down


def setup_inputs(seed: int = 0) -> dict:
    key = jax.random.key(seed)
    ks = jax.random.split(key, 32)
    f32 = jnp.float32
    L = DEPTH
    nrm = lambda k, shp, s: jax.random.normal(k, shp, f32) * s
    x = nrm(ks[0], (BATCH, SEQ, D_MODEL), 1.0)
    col_scale = jnp.concatenate([jnp.ones((SSM_WIDTH + 2 * QK_WIDTH,), f32),
                                 jnp.full((ATT_WIDTH,), BETA, f32)])
    w_in = nrm(ks[1], (L, D_MODEL, IN_WIDTH), D_MODEL ** -0.5) * col_scale
    ssm_log_step = jax.random.uniform(ks[2], (L, SSM_GROUPS), f32, math.log(1e-3), math.log(1e-1))
    ssm_lambda_re = -0.5 + nrm(ks[3], (L, SSM_GROUPS, SSM_STATE), 0.01)
    ssm_lambda_im = jnp.broadcast_to(math.pi * jnp.arange(SSM_STATE, dtype=f32), (L, SSM_GROUPS, SSM_STATE)) \
        + nrm(ks[4], (L, SSM_GROUPS, SSM_STATE), 0.01)
    ssm_b_re = nrm(ks[5], (L, SSM_GROUPS, SSM_STATE, SSM_GROUP), (2 * SSM_GROUP) ** -0.5)
    ssm_b_im = nrm(ks[6], (L, SSM_GROUPS, SSM_STATE, SSM_GROUP), (2 * SSM_GROUP) ** -0.5)
    ssm_c_re = nrm(ks[7], (L, SSM_GROUPS, SSM_GROUP, SSM_STATE), (2 * SSM_STATE) ** -0.5)
    ssm_c_im = nrm(ks[8], (L, SSM_GROUPS, SSM_GROUP, SSM_STATE), (2 * SSM_STATE) ** -0.5)
    ssm_d = nrm(ks[9], (L, SSM_WIDTH), 1.0)
    ssm_w_glu = nrm(ks[10], (L, SSM_WIDTH, SSM_WIDTH), SSM_WIDTH ** -0.5)
    ssm_b_glu = nrm(ks[11], (L, SSM_WIDTH), 0.02)
    att_lambda_q1 = nrm(ks[12], (L, ATT_QK_DIM), 0.1)
    att_lambda_k1 = nrm(ks[13], (L, ATT_QK_DIM), 0.1)
    att_lambda_q2 = nrm(ks[14], (L, ATT_QK_DIM), 0.1)
    att_lambda_k2 = nrm(ks[15], (L, ATT_QK_DIM), 0.1)
    att_subln_g = 1.0 + nrm(ks[16], (L, ATT_V_DIM), 0.02)
    rel_bias = nrm(ks[17], (REL_BUCKETS, ATT_HEADS), 0.5)
    w_out = nrm(ks[18], (L, MIX_WIDTH, D_MODEL), MIX_WIDTH ** -0.5 * BETA)
    ln1_g = 1.0 + nrm(ks[19], (L, D_MODEL), 0.02)
    ln1_b = nrm(ks[20], (L, D_MODEL), 0.02)
    ffn_w_up = nrm(ks[21], (L, D_MODEL, 2 * D_FF), D_MODEL ** -0.5)
    ffn_conv_w = nrm(ks[22], (L, CONV_WIDTH, D_FF), CONV_WIDTH ** -0.5)
    ffn_conv_b = nrm(ks[23], (L, D_FF), 0.02)
    ffn_w_down = nrm(ks[24], (L, D_FF, D_MODEL), D_FF ** -0.5 * BETA)
    ln2_g = 1.0 + nrm(ks[25], (L, D_MODEL), 0.02)
    ln2_b = nrm(ks[26], (L, D_MODEL), 0.02)
    return {'x': x, 'w_in': w_in, 'ssm_log_step': ssm_log_step, 'ssm_lambda_re': ssm_lambda_re,
            'ssm_lambda_im': ssm_lambda_im, 'ssm_b_re': ssm_b_re, 'ssm_b_im': ssm_b_im,
            'ssm_c_re': ssm_c_re, 'ssm_c_im': ssm_c_im, 'ssm_d': ssm_d, 'ssm_w_glu': ssm_w_glu,
            'ssm_b_glu': ssm_b_glu, 'att_lambda_q1': att_lambda_q1, 'att_lambda_k1': att_lambda_k1,
            'att_lambda_q2': att_lambda_q2, 'att_lambda_k2': att_lambda_k2, 'att_subln_g': att_subln_g,
            'rel_bias': rel_bias, 'w_out': w_out, 'ln1_g': ln1_g, 'ln1_b': ln1_b,
            'ffn_w_up': ffn_w_up, 'ffn_conv_w': ffn_conv_w, 'ffn_conv_b': ffn_conv_b,
            'ffn_w_down': ffn_w_down, 'ln2_g': ln2_g, 'ln2_b': ln2_b}


def reference(x, w_in, ssm_log_step, ssm_lambda_re, ssm_lambda_im, ssm_b_re, ssm_b_im,
              ssm_c_re, ssm_c_im, ssm_d, ssm_w_glu, ssm_b_glu, att_lambda_q1, att_lambda_k1,
              att_lambda_q2, att_lambda_k2, att_subln_g, rel_bias, w_out, ln1_g, ln1_b,
              ffn_w_up, ffn_conv_w, ffn_conv_b, ffn_w_down, ln2_g, ln2_b):
    h = x
    bsz, seq = x.shape[0], x.shape[1]
    for l in range(DEPTH):
        lambda_init = 0.8 - 0.6 * math.exp(-0.3 * l)
        proj = h @ w_in[l]
        u, q, k, v = jnp.split(proj, [SSM_WIDTH, SSM_WIDTH + QK_WIDTH, SSM_WIDTH + 2 * QK_WIDTH], axis=-1)
        q = q.reshape(bsz, seq, ATT_HEADS, 2, ATT_QK_DIM)
        k = k.reshape(bsz, seq, ATT_HEADS, 2, ATT_QK_DIM)
        v = v.reshape(bsz, seq, ATT_HEADS, ATT_V_DIM)
        y_ssm = ssm_mixer(u, ssm_log_step[l], ssm_lambda_re[l], ssm_lambda_im[l], ssm_b_re[l],
                          ssm_b_im[l], ssm_c_re[l], ssm_c_im[l], ssm_d[l], ssm_w_glu[l], ssm_b_glu[l])
        y_att = diff_attention(q, k, v, rel_bias, att_lambda_q1[l], att_lambda_k1[l],
                               att_lambda_q2[l], att_lambda_k2[l], att_subln_g[l], lambda_init)
        mix = jnp.concatenate([y_ssm, y_att], axis=-1) @ w_out[l]
        h = layer_norm(ALPHA * h + mix, ln1_g[l], ln1_b[l])
        ff = conv_ffn(h, ffn_w_up[l], ffn_conv_w[l], ffn_conv_b[l], ffn_w_down[l])
        h = layer_norm(ALPHA * h + ff, ln2_g[l], ln2_b[l])
    return h
```

```python
import functools
import math

import numpy as np
import jax
import jax.numpy as jnp
from jax import lax
from jax.experimental import pallas as pl
from jax.experimental.pallas import tpu as pltpu

F32 = jnp.float32
BF16 = jnp.bfloat16

D_MODEL = 4096
CHUNK = 64
SSM_WIDTH = 2048
SSM_GROUP = 16
SSM_GROUPS = SSM_WIDTH // SSM_GROUP
SSM_STATE = 64
ATT_QK_DIM = 128
ATT_V_DIM = 256
ATT_WIDTH = 2048
ATT_HEADS = 8
QK_WIDTH = 2048
D_FF = 11008
REL_BUCKETS = 32
REL_MAX_DIST = 128
DEPTH = 1
ALPHA = (2 * DEPTH) ** 0.25
LN_EPS = 1e-5
NEG_INF = -1e30
LAMBDA_INIT = 0.8 - 0.6 * math.exp(-0.3 * 0)

LANES = 128
SSM_L = 16
GROUPS_PER_TILE = LANES // SSM_GROUP
N_LANE_TILES = SSM_WIDTH // LANES
STATE_LANES = GROUPS_PER_TILE * SSM_STATE
D_FF_PAD = 11264
VMEM_LIMIT = 56 * 1024 * 1024


def _cparams(n_axes, vmem=VMEM_LIMIT):
    return pltpu.CompilerParams(dimension_semantics=("arbitrary",) * n_axes,
                                vmem_limit_bytes=vmem)


def _matmul_kernel(x_ref, w_ref, o_ref, acc_ref):
    k = pl.program_id(2)

    @pl.when(k == 0)
    def _():
        acc_ref[...] = jnp.zeros_like(acc_ref)

    acc_ref[...] += jnp.dot(x_ref[...], w_ref[...], preferred_element_type=F32)

    @pl.when(k == pl.num_programs(2) - 1)
    def _():
        o_ref[...] = acc_ref[...].astype(o_ref.dtype)


def _matmul(x, w, out_dtype, tm, tn, tk):
    m, kdim = x.shape
    n = w.shape[1]
    tm, tn, tk = min(tm, m), min(tn, n), min(tk, kdim)
    assert m % tm == 0 and n % tn == 0 and kdim % tk == 0
    return pl.pallas_call(
        _matmul_kernel,
        grid=(m // tm, n // tn, kdim // tk),
        in_specs=[pl.BlockSpec((tm, tk), lambda i, j, k: (i, k)),
                  pl.BlockSpec((tk, tn), lambda i, j, k: (k, j))],
        out_specs=pl.BlockSpec((tm, tn), lambda i, j, k: (i, j)),
        out_shape=jax.ShapeDtypeStruct((m, n), out_dtype),
        scratch_shapes=[pltpu.VMEM((tm, tn), F32)],
        compiler_params=_cparams(3),
        name="matmul",
    )(x, w)


def _gelu_tanh(x):
    c = math.sqrt(2.0 / math.pi)
    return 0.5 * x * (1.0 + jnp.tanh(c * (x + 0.044715 * (x * x * x))))


def _ssm_kernel(u_ref, d_ref, p_ref, q_ref, a_ref, dskip_ref, y_ref,
                urev_ref, e_ref, hin_ref):
    n_chunks = u_ref.shape[0] // SSM_L

    for i in range(SSM_L):
        rows = u_ref[pl.ds(i, n_chunks, stride=SSM_L), :]
        urev_ref[:, (SSM_L - 1 - i) * LANES:(SSM_L - i) * LANES] = rows.astype(BF16)

    e_ref[...] = jnp.dot(urev_ref[...], p_ref[...], preferred_element_type=F32)

    a_re = a_ref[0:1, :]
    a_im = a_ref[1:2, :]

    def chunk_step(c, carry):
        h_re, h_im = carry
        hin_ref[pl.ds(c, 1), 0:STATE_LANES] = h_re
        hin_ref[pl.ds(c, 1), STATE_LANES:2 * STATE_LANES] = h_im
        e_re = e_ref[pl.ds(c, 1), 0:STATE_LANES]
        e_im = e_ref[pl.ds(c, 1), STATE_LANES:2 * STATE_LANES]
        return (a_re * h_re - a_im * h_im + e_re,
                a_re * h_im + a_im * h_re + e_im)

    zero = jnp.zeros((1, STATE_LANES), F32)
    lax.fori_loop(0, n_chunks, chunk_step, (zero, zero))

    carry_in = jnp.dot(hin_ref[...].astype(BF16), q_ref[...], preferred_element_type=F32)

    dskip = dskip_ref[...]
    for i in range(SSM_L):
        intra = jnp.dot(urev_ref[:, (SSM_L - 1 - i) * LANES:], d_ref[0:(i + 1) * LANES, :],
                        preferred_element_type=F32)
        u_i = u_ref[pl.ds(i, n_chunks, stride=SSM_L), :]
        y = intra + carry_in[:, i * LANES:(i + 1) * LANES] + dskip * u_i
        y_ref[pl.ds(i, n_chunks, stride=SSM_L), :] = _gelu_tanh(y)


def _ssm_mixer(u, dstack, pstack, qstack, a_chunk, dskip):
    seq = u.shape[0]
    n_chunks = seq // SSM_L
    return pl.pallas_call(
        _ssm_kernel,
        grid=(N_LANE_TILES,),
        in_specs=[pl.BlockSpec((seq, LANES), lambda j: (0, j)),
                  pl.BlockSpec((None, SSM_L * LANES, LANES), lambda j: (j, 0, 0)),
                  pl.BlockSpec((None, SSM_L * LANES, 2 * STATE_LANES), lambda j: (j, 0, 0)),
                  pl.BlockSpec((None, 2 * STATE_LANES, SSM_L * LANES), lambda j: (j, 0, 0)),
                  pl.BlockSpec((None, 2, STATE_LANES), lambda j: (j, 0, 0)),
                  pl.BlockSpec((None, 1, LANES), lambda j: (j, 0, 0))],
        out_specs=pl.BlockSpec((seq, LANES), lambda j: (0, j)),
        out_shape=jax.ShapeDtypeStruct((seq, SSM_WIDTH), F32),
        scratch_shapes=[pltpu.VMEM((n_chunks, SSM_L * LANES), BF16),
                        pltpu.VMEM((n_chunks, 2 * STATE_LANES), F32),
                        pltpu.VMEM((n_chunks, 2 * STATE_LANES), F32)],
        compiler_params=_cparams(1),
        name="ssm_mixer",
    )(u, dstack, pstack, qstack, a_chunk, dskip)


def _ssm_operators(log_step, lam_re, lam_im, b_re, b_im, c_re, c_im):
    hi = lax.Precision.HIGHEST
    step = jnp.exp(log_step.astype(F32))[:, None]
    lr = lam_re.astype(F32)
    li = lam_im.astype(F32)
    mag = jnp.exp(lr * step)
    ab_re = mag * jnp.cos(li * step)
    ab_im = mag * jnp.sin(li * step)
    den = lr * lr + li * li
    nr = ab_re - 1.0
    ni = ab_im
    z_re = (nr * lr + ni * li) / den
    z_im = (ni * lr - nr * li) / den
    br = b_re.astype(F32)
    bi = b_im.astype(F32)
    bb_re = z_re[..., None] * br - z_im[..., None] * bi
    bb_im = z_re[..., None] * bi + z_im[..., None] * br

    pw_re = [jnp.ones_like(ab_re)]
    pw_im = [jnp.zeros_like(ab_im)]
    for _ in range(SSM_L):
        r, m = pw_re[-1], pw_im[-1]
        pw_re.append(ab_re * r - ab_im * m)
        pw_im.append(ab_re * m + ab_im * r)
    pw_re = jnp.stack(pw_re)
    pw_im = jnp.stack(pw_im)

    abb_re = pw_re[:SSM_L, :, :, None] * bb_re - pw_im[:SSM_L, :, :, None] * bb_im
    abb_im = pw_re[:SSM_L, :, :, None] * bb_im + pw_im[:SSM_L, :, :, None] * bb_re
    cr = c_re.astype(F32)
    ci = c_im.astype(F32)
    kern = (jnp.einsum('gpn,tgnh->tghp', cr, abb_re, precision=hi)
            - jnp.einsum('gpn,tgnh->tghp', ci, abb_im, precision=hi))

    eye = jnp.eye(GROUPS_PER_TILE, dtype=F32)
    nt, gl = N_LANE_TILES, GROUPS_PER_TILE

    k5 = kern.reshape(SSM_L, nt, gl, SSM_GROUP, SSM_GROUP).transpose(1, 0, 2, 3, 4)
    dstack = (k5[:, :, :, :, None, :] * eye[None, None, :, None, :, None]
              ).reshape(nt, SSM_L * LANES, LANES)

    def p_part(abb):
        a5 = abb.reshape(SSM_L, nt, gl, SSM_STATE, SSM_GROUP).transpose(1, 0, 2, 4, 3)
        return a5[:, :, :, :, None, :] * eye[None, None, :, None, :, None]
    pstack = jnp.stack([p_part(abb_re), p_part(abb_im)], axis=4)
    pstack = pstack.reshape(nt, SSM_L * LANES, 2 * STATE_LANES)

    ca_re = cr[None] * pw_re[1:, :, None, :] - ci[None] * pw_im[1:, :, None, :]
    ca_im = cr[None] * pw_im[1:, :, None, :] + ci[None] * pw_re[1:, :, None, :]

    def q_part(ca):
        c5 = ca.reshape(SSM_L, nt, gl, SSM_GROUP, SSM_STATE).transpose(1, 2, 4, 0, 3)
        return c5[:, :, :, :, None, :] * eye[None, :, None, None, :, None]
    qstack = jnp.stack([q_part(ca_re), q_part(-ca_im)], axis=1)
    qstack = qstack.reshape(nt, 2 * STATE_LANES, SSM_L * LANES)

    a_chunk = jnp.stack([pw_re[SSM_L].reshape(nt, STATE_LANES),
                         pw_im[SSM_L].reshape(nt, STATE_LANES)], axis=1)
    return dstack.astype(BF16), pstack.astype(BF16), qstack.astype(BF16), a_chunk


def _glu_kernel(y_ref, w_ref, b_ref, o_ref):
    y = y_ref[...]
    z = jnp.dot(y.astype(BF16), w_ref[...], preferred_element_type=F32) + b_ref[...]
    o_ref[...] = (y * jax.nn.sigmoid(z)).astype(o_ref.dtype)


def _glu(y, w, b, tm):
    m, n = y.shape
    tm = min(tm, m)
    return pl.pallas_call(
        _glu_kernel,
        grid=(m // tm,),
        in_specs=[pl.BlockSpec((tm, n), lambda i: (i, 0)),
                  pl.BlockSpec((n, n), lambda i: (0, 0)),
                  pl.BlockSpec((1, n), lambda i: (0, 0))],
        out_specs=pl.BlockSpec((tm, n), lambda i: (i, 0)),
        out_shape=jax.ShapeDtypeStruct((m, n), BF16),
        compiler_params=_cparams(1),
        name="glu",
    )(y, w, b)


ATT_TQ = 256
ATT_TK = 256


def _attn_kernel(q_ref, k_ref, v_ref, bias_ref, lq1_ref, lk1_ref, lq2_ref, lk2_ref, g_ref,
                 o_ref, m_ref, l_ref, acc_ref):
    i = pl.program_id(1)
    tq, tk = ATT_TQ, ATT_TK
    blocks_per_q = tq // tk
    n_near = blocks_per_q + 1

    m_ref[...] = jnp.full_like(m_ref, NEG_INF)
    l_ref[...] = jnp.zeros_like(l_ref)
    acc_ref[...] = jnp.zeros_like(acc_ref)

    def visit(j, bias):
        start = pl.multiple_of(j * tk, tk)
        kb = k_ref[pl.ds(start, tk), :]
        vb = v_ref[pl.ds(start, tk), :]
        for c in range(2):
            q = q_ref[:, c * ATT_QK_DIM:(c + 1) * ATT_QK_DIM]
            kc = kb[:, c * ATT_QK_DIM:(c + 1) * ATT_QK_DIM]
            s = lax.dot_general(q, kc, (((1,), (1,)), ((), ())), preferred_element_type=F32)
            if bias is not None:
                s = s + bias
            m_prev = m_ref[c][:, 0:1]
            m_new = jnp.maximum(m_prev, jnp.max(s, axis=1, keepdims=True))
            scale = jnp.exp(m_prev - m_new)
            p = jnp.exp(s - m_new)
            l_ref[c] = jnp.broadcast_to(scale * l_ref[c][:, 0:1] + jnp.sum(p, axis=1, keepdims=True),
                                        (tq, LANES))
            m_ref[c] = jnp.broadcast_to(m_new, (tq, LANES))
            acc_ref[c] = scale * acc_ref[c] + jnp.dot(p.astype(BF16), vb, preferred_element_type=F32)

    first_near = i * blocks_per_q - 1

    def far_step(j, carry):
        visit(j, None)
        return carry

    lax.fori_loop(0, jnp.maximum(first_near, 0), far_step, 0)

    for r in range(n_near):
        if r == 0:
            @pl.when(first_near >= 0)
            def _():
                visit(first_near, bias_ref[0])
        else:
            visit(first_near + r, bias_ref[r])

    lam = (jnp.exp(jnp.sum(lq1_ref[...] * lk1_ref[...], axis=1, keepdims=True))
           - jnp.exp(jnp.sum(lq2_ref[...] * lk2_ref[...], axis=1, keepdims=True)) + LAMBDA_INIT)
    o = acc_ref[0] / l_ref[0][:, 0:1] - lam * (acc_ref[1] / l_ref[1][:, 0:1])
    o = o * lax.rsqrt(jnp.mean(o * o, axis=1, keepdims=True) + LN_EPS) * g_ref[...]
    o_ref[...] = (o * (1.0 - LAMBDA_INIT)).astype(o_ref.dtype)


def _t5_bucket(rel):
    half = REL_BUCKETS // 2
    max_exact = half // 2
    ret = jnp.where(rel > 0, half, 0)
    n = jnp.abs(rel)
    nf = jnp.maximum(n, 1).astype(jnp.float32)
    large = max_exact + (jnp.log(nf / max_exact) / math.log(REL_MAX_DIST / max_exact)
                         * (half - max_exact)).astype(jnp.int32)
    large = jnp.minimum(large, half - 1)
    return ret + jnp.where(n < max_exact, n, large)


def _near_bias(rel_bias):
    tq, tk = ATT_TQ, ATT_TK
    n_near = tq // tk + 1
    qpos = np.arange(tq)[:, None]
    tiles = []
    for r in range(n_near):
        kpos = (r - 1) * tk + np.arange(tk)[None, :]
        rel = jnp.asarray(kpos - qpos, jnp.int32)
        visible = jnp.asarray((kpos // CHUNK) <= (qpos // CHUNK))
        b = rel_bias[_t5_bucket(rel)].astype(F32) - rel_bias[REL_BUCKETS // 2 - 1].astype(F32)
        tiles.append(jnp.where(visible[:, :, None], b, NEG_INF))
    return jnp.stack(tiles).transpose(3, 0, 1, 2)


def _diff_attention(q, k, v, bias, lq1, lk1, lq2, lk2, subln_g):
    seq = q.shape[0]
    tq = ATT_TQ
    n_near = bias.shape[1]
    head_blk = 2 * ATT_QK_DIM
    vec = pl.BlockSpec((1, ATT_QK_DIM), lambda h, i: (0, 0))
    return pl.pallas_call(
        _attn_kernel,
        grid=(ATT_HEADS, seq // tq),
        in_specs=[pl.BlockSpec((tq, head_blk), lambda h, i: (i, h)),
                  pl.BlockSpec((seq, head_blk), lambda h, i: (0, h)),
                  pl.BlockSpec((seq, ATT_V_DIM), lambda h, i: (0, h)),
                  pl.BlockSpec((None, n_near, tq, ATT_TK), lambda h, i: (h, 0, 0, 0)),
                  vec, vec, vec, vec,
                  pl.BlockSpec((1, ATT_V_DIM), lambda h, i: (0, 0))],
        out_specs=pl.BlockSpec((tq, ATT_V_DIM), lambda h, i: (i, h)),
        out_shape=jax.ShapeDtypeStruct((seq, ATT_WIDTH), BF16),
        scratch_shapes=[pltpu.VMEM((2, tq, LANES), F32),
                        pltpu.VMEM((2, tq, LANES), F32),
                        pltpu.VMEM((2, tq, ATT_V_DIM), F32)],
        compiler_params=_cparams(2),
        name="diff_attention",
    )(q, k, v, bias, lq1, lk1, lq2, lk2, subln_g)


def _layer_norm_rows(r, g, b):
    mu = jnp.mean(r, axis=1, keepdims=True)
    xc = r - mu
    var = jnp.mean(xc * xc, axis=1, keepdims=True)
    return xc * lax.rsqrt(var + LN_EPS) * g + b


def _outproj_kernel(ys_ref, ya_ref, wt_ref, wb_ref, x_ref, g_ref, b_ref, h_ref, hb_ref, acc_ref):
    k = pl.program_id(1)

    @pl.when(k == 0)
    def _():
        acc_ref[...] = jnp.zeros_like(acc_ref)

    acc_ref[...] += (jnp.dot(ys_ref[...], wt_ref[...], preferred_element_type=F32)
                     + jnp.dot(ya_ref[...], wb_ref[...], preferred_element_type=F32))

    @pl.when(k == pl.num_programs(1) - 1)
    def _():
        h = _layer_norm_rows(ALPHA * x_ref[...] + acc_ref[...], g_ref[...], b_ref[...])
        h_ref[...] = h
        hb_ref[...] = h.astype(BF16)


def _outproj_ln(ys, ya, w_out, x, g, b, tm, tk):
    m, half = ys.shape
    d = w_out.shape[1]
    tm = min(tm, m)
    nk = half // tk
    row = pl.BlockSpec((1, d), lambda i, k: (0, 0))
    return pl.pallas_call(
        _outproj_kernel,
        grid=(m // tm, nk),
        in_specs=[pl.BlockSpec((tm, tk), lambda i, k: (i, k)),
                  pl.BlockSpec((tm, tk), lambda i, k: (i, k)),
                  pl.BlockSpec((tk, d), lambda i, k: (k, 0)),
                  pl.BlockSpec((tk, d), lambda i, k: (k + nk, 0)),
                  pl.BlockSpec((tm, d), lambda i, k: (i, 0)),
                  row, row],
        out_specs=[pl.BlockSpec((tm, d), lambda i, k: (i, 0)),
                   pl.BlockSpec((tm, d), lambda i, k: (i, 0))],
        out_shape=[jax.ShapeDtypeStruct((m, d), F32), jax.ShapeDtypeStruct((m, d), BF16)],
        scratch_shapes=[pltpu.VMEM((tm, d), F32)],
        compiler_params=_cparams(2),
        name="outproj_ln",
    )(ys, ya, w_out, w_out, x, g, b)


def _ffn_down_kernel(a_ref, g_ref, halo_ref, cw_ref, cb_ref, wd_ref, h_ref, lg_ref, lb_ref,
                     o_ref, acc_ref):
    i = pl.program_id(0)
    k = pl.program_id(1)

    @pl.when(k == 0)
    def _():
        acc_ref[...] = jnp.zeros_like(acc_ref)

    g = g_ref[...].astype(F32)
    halo = jnp.where(i == 0, 0.0, halo_ref[...].astype(F32))
    row = lax.broadcasted_iota(jnp.int32, g.shape, 0)
    prev1 = jnp.where(row == 0, halo[7:8, :], pltpu.roll(g, 1, 0))
    prev2 = jnp.where(row == 0, halo[6:7, :],
                      jnp.where(row == 1, halo[7:8, :], pltpu.roll(g, 2, 0)))
    gc = cb_ref[...] + prev2 * cw_ref[0:1, :] + prev1 * cw_ref[1:2, :] + g * cw_ref[2:3, :]
    act = (gc * jax.nn.sigmoid(gc)) * a_ref[...].astype(F32)
    acc_ref[...] += jnp.dot(act.astype(BF16), wd_ref[...], preferred_element_type=F32)

    @pl.when(k == pl.num_programs(1) - 1)
    def _():
        o_ref[...] = _layer_norm_rows(ALPHA * h_ref[...] + acc_ref[...], lg_ref[...], lb_ref[...])


def _ffn_down_ln(up, conv_w, conv_b, w_down, h1, g, b, tm, tk):
    m = up.shape[0]
    fp, d = w_down.shape
    tm = min(tm, m)
    nk = fp // tk
    halo_rows = 8
    row = pl.BlockSpec((1, d), lambda i, k: (0, 0))
    return pl.pallas_call(
        _ffn_down_kernel,
        grid=(m // tm, nk),
        in_specs=[pl.BlockSpec((tm, tk), lambda i, k: (i, k)),
                  pl.BlockSpec((tm, tk), lambda i, k: (i, k + nk)),
                  pl.BlockSpec((halo_rows, tk),
                               lambda i, k: (jnp.maximum(i * (tm // halo_rows) - 1, 0), k + nk)),
                  pl.BlockSpec((3, tk), lambda i, k: (0, k)),
                  pl.BlockSpec((1, tk), lambda i, k: (0, k)),
                  pl.BlockSpec((tk, d), lambda i, k: (k, 0)),
                  pl.BlockSpec((tm, d), lambda i, k: (i, 0)),
                  row, row],
        out_specs=pl.BlockSpec((tm, d), lambda i, k: (i, 0)),
        out_shape=jax.ShapeDtypeStruct((m, d), F32),
        scratch_shapes=[pltpu.VMEM((tm, d), F32)],
        compiler_params=_cparams(2),
        name="ffn_down_ln",
    )(up, up, up, conv_w, conv_b, w_down, h1, g, b)


def kernel(x, w_in, ssm_log_step, ssm_lambda_re, ssm_lambda_im, ssm_b_re, ssm_b_im, ssm_c_re, ssm_c_im, ssm_d, ssm_w_glu, ssm_b_glu, att_lambda_q1, att_lambda_k1, att_lambda_q2, att_lambda_k2, att_subln_g, rel_bias, w_out, ln1_g, ln1_b, ffn_w_up, ffn_conv_w, ffn_conv_b, ffn_w_down, ln2_g, ln2_b):
    bsz, seq, _ = x.shape
    assert bsz == 1 and DEPTH == 1
    l = 0
    xs = x[0]

    qk_scale = ATT_QK_DIM ** -0.5
    w_u = w_in[l][:, :SSM_WIDTH].astype(BF16)
    w_q = (w_in[l][:, SSM_WIDTH:SSM_WIDTH + QK_WIDTH] * qk_scale).astype(BF16)
    w_kv = w_in[l][:, SSM_WIDTH + QK_WIDTH:].astype(BF16)
    w_qkv = jnp.concatenate([w_q, w_kv], axis=1)
    x_bf = xs.astype(BF16)

    u = _matmul(x_bf, w_u, F32, 1024, 1024, 1024)
    qkv = _matmul(x_bf, w_qkv, BF16, 1024, 1024, 1024)

    dstack, pstack, qstack, a_chunk = _ssm_operators(
        ssm_log_step[l], ssm_lambda_re[l], ssm_lambda_im[l], ssm_b_re[l], ssm_b_im[l],
        ssm_c_re[l], ssm_c_im[l])
    dskip = ssm_d[l].astype(F32).reshape(N_LANE_TILES, 1, LANES)
    y = _ssm_mixer(u, dstack, pstack, qstack, a_chunk, dskip)
    y_ssm = _glu(y, ssm_w_glu[l].astype(BF16), ssm_b_glu[l].astype(F32).reshape(1, SSM_WIDTH), 512)

    q = qkv[:, :QK_WIDTH]
    k = qkv[:, QK_WIDTH:2 * QK_WIDTH]
    v = qkv[:, 2 * QK_WIDTH:]
    vec = lambda a: a.astype(F32).reshape(1, ATT_QK_DIM)
    y_att = _diff_attention(q, k, v, _near_bias(rel_bias),
                            vec(att_lambda_q1[l]), vec(att_lambda_k1[l]),
                            vec(att_lambda_q2[l]), vec(att_lambda_k2[l]),
                            att_subln_g[l].astype(F32).reshape(1, ATT_V_DIM))

    h1, h1_bf = _outproj_ln(y_ssm, y_att, w_out[l].astype(BF16), xs,
                            ln1_g[l].reshape(1, D_MODEL), ln1_b[l].reshape(1, D_MODEL), 256, 512)

    pad = D_FF_PAD - D_FF
    w_up = ffn_w_up[l].astype(BF16)
    w_up = jnp.concatenate([jnp.pad(w_up[:, :D_FF], ((0, 0), (0, pad))),
                            jnp.pad(w_up[:, D_FF:], ((0, 0), (0, pad)))], axis=1)
    conv_w = jnp.pad(ffn_conv_w[l].astype(F32), ((0, 0), (0, pad)))
    conv_b = jnp.pad(ffn_conv_b[l].astype(F32), ((0, pad),)).reshape(1, D_FF_PAD)
    w_down = jnp.pad(ffn_w_down[l].astype(BF16), ((0, pad), (0, 0)))
    up = _matmul(h1_bf, w_up, BF16, 1024, 1024, 1024)
    out = _ffn_down_ln(up, conv_w, conv_b, w_down, h1,
                       ln2_g[l].reshape(1, D_MODEL), ln2_b[l].reshape(1, D_MODEL), 256, 512)
    return out[None]
```

```python
import functools
import math

import numpy as np
import jax
import jax.numpy as jnp
from jax import lax
from jax.experimental import pallas as pl
from jax.experimental.pallas import tpu as pltpu

F32 = jnp.float32
BF16 = jnp.bfloat16

D_MODEL = 4096
CHUNK = 64
SSM_WIDTH = 2048
SSM_GROUP = 16
SSM_GROUPS = SSM_WIDTH // SSM_GROUP
SSM_STATE = 64
ATT_QK_DIM = 128
ATT_V_DIM = 256
ATT_WIDTH = 2048
ATT_HEADS = 8
QK_WIDTH = 2048
D_FF = 11008
REL_BUCKETS = 32
REL_MAX_DIST = 128
DEPTH = 1
ALPHA = (2 * DEPTH) ** 0.25
LN_EPS = 1e-5
NEG_INF = -1e30
LAMBDA_INIT = 0.8 - 0.6 * math.exp(-0.3 * 0)

LANES = 128
SSM_L = 16
GROUPS_PER_TILE = LANES // SSM_GROUP
N_LANE_TILES = SSM_WIDTH // LANES
STATE_LANES = GROUPS_PER_TILE * SSM_STATE
D_FF_PAD = 11264
VMEM_LIMIT = 56 * 1024 * 1024


def _cparams(n_axes, vmem=VMEM_LIMIT):
    return pltpu.CompilerParams(dimension_semantics=("arbitrary",) * n_axes,
                                vmem_limit_bytes=vmem)


def _matmul_kernel(x_ref, w_ref, o_ref, acc_ref):
    k = pl.program_id(2)

    @pl.when(k == 0)
    def _():
        acc_ref[...] = jnp.zeros_like(acc_ref)

    acc_ref[...] += jnp.dot(x_ref[...], w_ref[...], preferred_element_type=F32)

    @pl.when(k == pl.num_programs(2) - 1)
    def _():
        o_ref[...] = acc_ref[...].astype(o_ref.dtype)


def _matmul(x, w, out_dtype, tm, tn, tk):
    m, kdim = x.shape
    n = w.shape[1]
    tm, tn, tk = min(tm, m), min(tn, n), min(tk, kdim)
    assert m % tm == 0 and n % tn == 0 and kdim % tk == 0
    return pl.pallas_call(
        _matmul_kernel,
        grid=(m // tm, n // tn, kdim // tk),
        in_specs=[pl.BlockSpec((tm, tk), lambda i, j, k: (i, k)),
                  pl.BlockSpec((tk, tn), lambda i, j, k: (k, j))],
        out_specs=pl.BlockSpec((tm, tn), lambda i, j, k: (i, j)),
        out_shape=jax.ShapeDtypeStruct((m, n), out_dtype),
        scratch_shapes=[pltpu.VMEM((tm, tn), F32)],
        compiler_params=_cparams(3),
        name="matmul",
    )(x, w)


def _gelu_tanh(x):
    c = math.sqrt(2.0 / math.pi)
    return 0.5 * x * (1.0 + jnp.tanh(c * (x + 0.044715 * (x * x * x))))


def _ssm_kernel(u_ref, d_ref, p_ref, q_ref, a_ref, dskip_ref, y_ref,
                urev_ref, e_ref, hin_ref):
    n_chunks = u_ref.shape[0] // SSM_L

    for i in range(SSM_L):
        rows = u_ref[pl.ds(i, n_chunks, stride=SSM_L), :]
        urev_ref[:, (SSM_L - 1 - i) * LANES:(SSM_L - i) * LANES] = rows.astype(BF16)

    e_ref[...] = jnp.dot(urev_ref[...], p_ref[...], preferred_element_type=F32)

    a_re = a_ref[0:1, :]
    a_im = a_ref[1:2, :]

    def chunk_step(c, carry):
        h_re, h_im = carry
        hin_ref[pl.ds(c, 1), 0:STATE_LANES] = h_re
        hin_ref[pl.ds(c, 1), STATE_LANES:2 * STATE_LANES] = h_im
        e_re = e_ref[pl.ds(c, 1), 0:STATE_LANES]
        e_im = e_ref[pl.ds(c, 1), STATE_LANES:2 * STATE_LANES]
        return (a_re * h_re - a_im * h_im + e_re,
                a_re * h_im + a_im * h_re + e_im)

    zero = jnp.zeros((1, STATE_LANES), F32)
    lax.fori_loop(0, n_chunks, chunk_step, (zero, zero))

    carry_in = jnp.dot(hin_ref[...].astype(BF16), q_ref[...], preferred_element_type=F32)

    dskip = dskip_ref[...]
    for i in range(SSM_L):
        intra = jnp.dot(urev_ref[:, (SSM_L - 1 - i) * LANES:], d_ref[0:(i + 1) * LANES, :],
                        preferred_element_type=F32)
        u_i = u_ref[pl.ds(i, n_chunks, stride=SSM_L), :]
        y = intra + carry_in[:, i * LANES:(i + 1) * LANES] + dskip * u_i
        y_ref[pl.ds(i, n_chunks, stride=SSM_L), :] = _gelu_tanh(y)


def _ssm_mixer(u, dstack, pstack, qstack, a_chunk, dskip):
    seq = u.shape[0]
    n_chunks = seq // SSM_L
    return pl.pallas_call(
        _ssm_kernel,
        grid=(N_LANE_TILES,),
        in_specs=[pl.BlockSpec((seq, LANES), lambda j: (0, j)),
                  pl.BlockSpec((None, SSM_L * LANES, LANES), lambda j: (j, 0, 0)),
                  pl.BlockSpec((None, SSM_L * LANES, 2 * STATE_LANES), lambda j: (j, 0, 0)),
                  pl.BlockSpec((None, 2 * STATE_LANES, SSM_L * LANES), lambda j: (j, 0, 0)),
                  pl.BlockSpec((None, 2, STATE_LANES), lambda j: (j, 0, 0)),
                  pl.BlockSpec((None, 1, LANES), lambda j: (j, 0, 0))],
        out_specs=pl.BlockSpec((seq, LANES), lambda j: (0, j)),
        out_shape=jax.ShapeDtypeStruct((seq, SSM_WIDTH), F32),
        scratch_shapes=[pltpu.VMEM((n_chunks, SSM_L * LANES), BF16),
                        pltpu.VMEM((n_chunks, 2 * STATE_LANES), F32),
                        pltpu.VMEM((n_chunks, 2 * STATE_LANES), F32)],
        compiler_params=_cparams(1),
        name="ssm_mixer",
    )(u, dstack, pstack, qstack, a_chunk, dskip)


def _ssm_operators(log_step, lam_re, lam_im, b_re, b_im, c_re, c_im):
    hi = lax.Precision.HIGHEST
    step = jnp.exp(log_step.astype(F32))[:, None]
    lr = lam_re.astype(F32)
    li = lam_im.astype(F32)
    mag = jnp.exp(lr * step)
    ab_re = mag * jnp.cos(li * step)
    ab_im = mag * jnp.sin(li * step)
    den = lr * lr + li * li
    nr = ab_re - 1.0
    ni = ab_im
    z_re = (nr * lr + ni * li) / den
    z_im = (ni * lr - nr * li) / den
    br = b_re.astype(F32)
    bi = b_im.astype(F32)
    bb_re = z_re[..., None] * br - z_im[..., None] * bi
    bb_im = z_re[..., None] * bi + z_im[..., None] * br

    pw_re = [jnp.ones_like(ab_re)]
    pw_im = [jnp.zeros_like(ab_im)]
    for _ in range(SSM_L):
        r, m = pw_re[-1], pw_im[-1]
        pw_re.append(ab_re * r - ab_im * m)
        pw_im.append(ab_re * m + ab_im * r)
    pw_re = jnp.stack(pw_re)
    pw_im = jnp.stack(pw_im)

    abb_re = pw_re[:SSM_L, :, :, None] * bb_re - pw_im[:SSM_L, :, :, None] * bb_im
    abb_im = pw_re[:SSM_L, :, :, None] * bb_im + pw_im[:SSM_L, :, :, None] * bb_re
    cr = c_re.astype(F32)
    ci = c_im.astype(F32)
    kern = (jnp.einsum('gpn,tgnh->tghp', cr, abb_re, precision=hi)
            - jnp.einsum('gpn,tgnh->tghp', ci, abb_im, precision=hi))

    eye = jnp.eye(GROUPS_PER_TILE, dtype=F32)
    nt, gl = N_LANE_TILES, GROUPS_PER_TILE

    k5 = kern.reshape(SSM_L, nt, gl, SSM_GROUP, SSM_GROUP).transpose(1, 0, 2, 3, 4)
    dstack = (k5[:, :, :, :, None, :] * eye[None, None, :, None, :, None]
              ).reshape(nt, SSM_L * LANES, LANES)

    def p_part(abb):
        a5 = abb.reshape(SSM_L, nt, gl, SSM_STATE, SSM_GROUP).transpose(1, 0, 2, 4, 3)
        return a5[:, :, :, :, None, :] * eye[None, None, :, None, :, None]
    pstack = jnp.stack([p_part(abb_re), p_part(abb_im)], axis=4)
    pstack = pstack.reshape(nt, SSM_L * LANES, 2 * STATE_LANES)

    ca_re = cr[None] * pw_re[1:, :, None, :] - ci[None] * pw_im[1:, :, None, :]
    ca_im = cr[None] * pw_im[1:, :, None, :] + ci[None] * pw_re[1:, :, None, :]

    def q_part(ca):
        c5 = ca.reshape(SSM_L, nt, gl, SSM_GROUP, SSM_STATE).transpose(1, 2, 4, 0, 3)
        return c5[:, :, :, :, None, :] * eye[None, :, None, None, :, None]
    qstack = jnp.stack([q_part(ca_re), q_part(-ca_im)], axis=1)
    qstack = qstack.reshape(nt, 2 * STATE_LANES, SSM_L * LANES)

    a_chunk = jnp.stack([pw_re[SSM_L].reshape(nt, STATE_LANES),
                         pw_im[SSM_L].reshape(nt, STATE_LANES)], axis=1)
    return dstack.astype(BF16), pstack.astype(BF16), qstack.astype(BF16), a_chunk


def _glu_kernel(y_ref, w_ref, b_ref, o_ref):
    y = y_ref[...]
    z = jnp.dot(y.astype(BF16), w_ref[...], preferred_element_type=F32) + b_ref[...]
    o_ref[...] = (y * jax.nn.sigmoid(z)).astype(o_ref.dtype)


def _glu(y, w, b, tm):
    m, n = y.shape
    tm = min(tm, m)
    return pl.pallas_call(
        _glu_kernel,
        grid=(m // tm,),
        in_specs=[pl.BlockSpec((tm, n), lambda i: (i, 0)),
                  pl.BlockSpec((n, n), lambda i: (0, 0)),
                  pl.BlockSpec((1, n), lambda i: (0, 0))],
        out_specs=pl.BlockSpec((tm, n), lambda i: (i, 0)),
        out_shape=jax.ShapeDtypeStruct((m, n), BF16),
        compiler_params=_cparams(1),
        name="glu",
    )(y, w, b)


ATT_TQ = 512
LOG2_E = math.log2(math.e)


def _attn_kernel(q_ref, k_ref, v_ref, bias_ref, lq1_ref, lk1_ref, lq2_ref, lk2_ref, g_ref,
                 o_ref, m_ref, l_ref, acc_ref):
    i = pl.program_id(1)
    tq = ATT_TQ

    m_ref[...] = jnp.full_like(m_ref, NEG_INF)
    l_ref[...] = jnp.zeros_like(l_ref)
    acc_ref[...] = jnp.zeros_like(acc_ref)

    def visit(start, width, bias):
        kb = k_ref[pl.ds(start, width), :]
        vb = v_ref[pl.ds(start, width), :]
        for c in range(2):
            q = q_ref[:, c * ATT_QK_DIM:(c + 1) * ATT_QK_DIM]
            kc = kb[:, c * ATT_QK_DIM:(c + 1) * ATT_QK_DIM]
            s = lax.dot_general(q, kc, (((1,), (1,)), ((), ())), preferred_element_type=F32)
            if bias is not None:
                s = s + bias
            m_prev = m_ref[c]
            m_new = jnp.maximum(m_prev, jnp.max(s, axis=1, keepdims=True))
            scale = jnp.exp2(m_prev - m_new)
            p = jnp.exp2(s - m_new)
            l_ref[c] = scale * l_ref[c] + jnp.sum(p, axis=1, keepdims=True)
            m_ref[c] = m_new
            acc_ref[c] = scale * acc_ref[c] + jnp.dot(p.astype(BF16), vb, preferred_element_type=F32)

    n_far = jnp.maximum(i - 1, 0)

    def far_pair(jj, carry):
        visit(pl.multiple_of(jj * (2 * tq), 2 * tq), 2 * tq, None)
        return carry

    lax.fori_loop(0, lax.shift_right_logical(n_far, 1), far_pair, 0)

    @pl.when(lax.rem(n_far, 2) == 1)
    def _():
        visit(pl.multiple_of((n_far - 1) * tq, tq), tq, None)

    @pl.when(i >= 1)
    def _():
        visit(pl.multiple_of((i - 1) * tq, tq), 2 * tq, bias_ref[...])

    @pl.when(i == 0)
    def _():
        visit(0, tq, bias_ref[:, tq:2 * tq])

    lam = (jnp.exp(jnp.sum(lq1_ref[...] * lk1_ref[...], axis=1, keepdims=True))
           - jnp.exp(jnp.sum(lq2_ref[...] * lk2_ref[...], axis=1, keepdims=True)) + LAMBDA_INIT)
    o = acc_ref[0] / l_ref[0] - lam * (acc_ref[1] / l_ref[1])
    o = o * lax.rsqrt(jnp.mean(o * o, axis=1, keepdims=True) + LN_EPS) * g_ref[...]
    o_ref[...] = (o * (1.0 - LAMBDA_INIT)).astype(o_ref.dtype)


def _t5_bucket(rel):
    half = REL_BUCKETS // 2
    max_exact = half // 2
    ret = jnp.where(rel > 0, half, 0)
    n = jnp.abs(rel)
    nf = jnp.maximum(n, 1).astype(jnp.float32)
    large = max_exact + (jnp.log(nf / max_exact) / math.log(REL_MAX_DIST / max_exact)
                         * (half - max_exact)).astype(jnp.int32)
    large = jnp.minimum(large, half - 1)
    return ret + jnp.where(n < max_exact, n, large)


def _near_bias(rel_bias):
    tq = ATT_TQ
    qpos = np.arange(tq)[:, None]
    kpos = np.arange(-tq, tq)[None, :]
    rel = jnp.asarray(kpos - qpos, jnp.int32)
    visible = jnp.asarray((kpos // CHUNK) <= (qpos // CHUNK))
    far_bucket = REL_BUCKETS // 2 - 1
    b = (rel_bias[_t5_bucket(rel)].astype(F32) - rel_bias[far_bucket].astype(F32)) * LOG2_E
    return jnp.where(visible[:, :, None], b, NEG_INF).transpose(2, 0, 1)


def _diff_attention(q, k, v, bias, lq1, lk1, lq2, lk2, subln_g):
    seq = q.shape[0]
    tq = ATT_TQ
    head_blk = 2 * ATT_QK_DIM
    vec = pl.BlockSpec((1, ATT_QK_DIM), lambda h, i: (0, 0))
    return pl.pallas_call(
        _attn_kernel,
        grid=(ATT_HEADS, seq // tq),
        in_specs=[pl.BlockSpec((tq, head_blk), lambda h, i: (i, h)),
                  pl.BlockSpec((seq, head_blk), lambda h, i: (0, h)),
                  pl.BlockSpec((seq, ATT_V_DIM), lambda h, i: (0, h)),
                  pl.BlockSpec((None, tq, 2 * tq), lambda h, i: (h, 0, 0)),
                  vec, vec, vec, vec,
                  pl.BlockSpec((1, ATT_V_DIM), lambda h, i: (0, 0))],
        out_specs=pl.BlockSpec((tq, ATT_V_DIM), lambda h, i: (i, h)),
        out_shape=jax.ShapeDtypeStruct((seq, ATT_WIDTH), BF16),
        scratch_shapes=[pltpu.VMEM((2, tq, 1), F32),
                        pltpu.VMEM((2, tq, 1), F32),
                        pltpu.VMEM((2, tq, ATT_V_DIM), F32)],
        compiler_params=_cparams(2),
        name="diff_attention",
    )(q, k, v, bias, lq1, lk1, lq2, lk2, subln_g)


def _layer_norm_rows(r, g, b):
    mu = jnp.mean(r, axis=1, keepdims=True)
    xc = r - mu
    var = jnp.mean(xc * xc, axis=1, keepdims=True)
    return xc * lax.rsqrt(var + LN_EPS) * g + b


def _outproj_kernel(ys_ref, ya_ref, wt_ref, wb_ref, x_ref, g_ref, b_ref, h_ref, hb_ref, acc_ref):
    k = pl.program_id(1)

    @pl.when(k == 0)
    def _():
        acc_ref[...] = jnp.zeros_like(acc_ref)

    acc_ref[...] += (jnp.dot(ys_ref[...], wt_ref[...], preferred_element_type=F32)
                     + jnp.dot(ya_ref[...], wb_ref[...], preferred_element_type=F32))

    @pl.when(k == pl.num_programs(1) - 1)
    def _():
        h = _layer_norm_rows(ALPHA * x_ref[...] + acc_ref[...], g_ref[...], b_ref[...])
        h_ref[...] = h
        hb_ref[...] = h.astype(BF16)


def _outproj_ln(ys, ya, w_out, x, g, b, tm, tk):
    m, half = ys.shape
    d = w_out.shape[1]
    tm = min(tm, m)
    nk = half // tk
    row = pl.BlockSpec((1, d), lambda i, k: (0, 0))
    return pl.pallas_call(
        _outproj_kernel,
        grid=(m // tm, nk),
        in_specs=[pl.BlockSpec((tm, tk), lambda i, k: (i, k)),
                  pl.BlockSpec((tm, tk), lambda i, k: (i, k)),
                  pl.BlockSpec((tk, d), lambda i, k: (k, 0)),
                  pl.BlockSpec((tk, d), lambda i, k: (k + nk, 0)),
                  pl.BlockSpec((tm, d), lambda i, k: (i, 0)),
                  row, row],
        out_specs=[pl.BlockSpec((tm, d), lambda i, k: (i, 0)),
                   pl.BlockSpec((tm, d), lambda i, k: (i, 0))],
        out_shape=[jax.ShapeDtypeStruct((m, d), F32), jax.ShapeDtypeStruct((m, d), BF16)],
        scratch_shapes=[pltpu.VMEM((tm, d), F32)],
        compiler_params=_cparams(2),
        name="outproj_ln",
    )(ys, ya, w_out, w_out, x, g, b)


def _ffn_down_kernel(a_ref, g_ref, halo_ref, cw_ref, cb_ref, wd_ref, h_ref, lg_ref, lb_ref,
                     o_ref, acc_ref):
    i = pl.program_id(0)
    k = pl.program_id(1)

    @pl.when(k == 0)
    def _():
        acc_ref[...] = jnp.zeros_like(acc_ref)

    g = g_ref[...].astype(F32)
    halo = jnp.where(i == 0, 0.0, halo_ref[...].astype(F32))
    row = lax.broadcasted_iota(jnp.int32, g.shape, 0)
    prev1 = jnp.where(row == 0, halo[7:8, :], pltpu.roll(g, 1, 0))
    prev2 = jnp.where(row == 0, halo[6:7, :],
                      jnp.where(row == 1, halo[7:8, :], pltpu.roll(g, 2, 0)))
    gc = cb_ref[...] + prev2 * cw_ref[0:1, :] + prev1 * cw_ref[1:2, :] + g * cw_ref[2:3, :]
    act = (gc * jax.nn.sigmoid(gc)) * a_ref[...].astype(F32)
    acc_ref[...] += jnp.dot(act.astype(BF16), wd_ref[...], preferred_element_type=F32)

    @pl.when(k == pl.num_programs(1) - 1)
    def _():
        o_ref[...] = _layer_norm_rows(ALPHA * h_ref[...] + acc_ref[...], lg_ref[...], lb_ref[...])


def _ffn_down_ln(up, conv_w, conv_b, w_down, h1, g, b, tm, tk):
    m = up.shape[0]
    fp, d = w_down.shape
    tm = min(tm, m)
    nk = fp // tk
    halo_rows = 8
    row = pl.BlockSpec((1, d), lambda i, k: (0, 0))
    return pl.pallas_call(
        _ffn_down_kernel,
        grid=(m // tm, nk),
        in_specs=[pl.BlockSpec((tm, tk), lambda i, k: (i, k)),
                  pl.BlockSpec((tm, tk), lambda i, k: (i, k + nk)),
                  pl.BlockSpec((halo_rows, tk),
                               lambda i, k: (jnp.maximum(i * (tm // halo_rows) - 1, 0), k + nk)),
                  pl.BlockSpec((3, tk), lambda i, k: (0, k)),
                  pl.BlockSpec((1, tk), lambda i, k: (0, k)),
                  pl.BlockSpec((tk, d), lambda i, k: (k, 0)),
                  pl.BlockSpec((tm, d), lambda i, k: (i, 0)),
                  row, row],
        out_specs=pl.BlockSpec((tm, d), lambda i, k: (i, 0)),
        out_shape=jax.ShapeDtypeStruct((m, d), F32),
        scratch_shapes=[pltpu.VMEM((tm, d), F32)],
        compiler_params=_cparams(2),
        name="ffn_down_ln",
    )(up, up, up, conv_w, conv_b, w_down, h1, g, b)


def kernel(x, w_in, ssm_log_step, ssm_lambda_re, ssm_lambda_im, ssm_b_re, ssm_b_im, ssm_c_re, ssm_c_im, ssm_d, ssm_w_glu, ssm_b_glu, att_lambda_q1, att_lambda_k1, att_lambda_q2, att_lambda_k2, att_subln_g, rel_bias, w_out, ln1_g, ln1_b, ffn_w_up, ffn_conv_w, ffn_conv_b, ffn_w_down, ln2_g, ln2_b):
    bsz, seq, _ = x.shape
    assert bsz == 1 and DEPTH == 1
    l = 0
    xs = x[0]

    qk_scale = ATT_QK_DIM ** -0.5 * LOG2_E
    w_u = w_in[l][:, :SSM_WIDTH].astype(BF16)
    w_q = (w_in[l][:, SSM_WIDTH:SSM_WIDTH + QK_WIDTH] * qk_scale).astype(BF16)
    w_kv = w_in[l][:, SSM_WIDTH + QK_WIDTH:].astype(BF16)
    w_qkv = jnp.concatenate([w_q, w_kv], axis=1)
    x_bf = xs.astype(BF16)

    u = _matmul(x_bf, w_u, F32, 1024, 1024, 1024)
    qkv = _matmul(x_bf, w_qkv, BF16, 1024, 1024, 1024)

    dstack, pstack, qstack, a_chunk = _ssm_operators(
        ssm_log_step[l], ssm_lambda_re[l], ssm_lambda_im[l], ssm_b_re[l], ssm_b_im[l],
        ssm_c_re[l], ssm_c_im[l])
    dskip = ssm_d[l].astype(F32).reshape(N_LANE_TILES, 1, LANES)
    y = _ssm_mixer(u, dstack, pstack, qstack, a_chunk, dskip)
    y_ssm = _glu(y, ssm_w_glu[l].astype(BF16), ssm_b_glu[l].astype(F32).reshape(1, SSM_WIDTH), 512)

    q = qkv[:, :QK_WIDTH]
    k = qkv[:, QK_WIDTH:2 * QK_WIDTH]
    v = qkv[:, 2 * QK_WIDTH:]
    vec = lambda a: a.astype(F32).reshape(1, ATT_QK_DIM)
    y_att = _diff_attention(q, k, v, _near_bias(rel_bias),
                            vec(att_lambda_q1[l]), vec(att_lambda_k1[l]),
                            vec(att_lambda_q2[l]), vec(att_lambda_k2[l]),
                            att_subln_g[l].astype(F32).reshape(1, ATT_V_DIM))

    h1, h1_bf = _outproj_ln(y_ssm, y_att, w_out[l].astype(BF16), xs,
                            ln1_g[l].reshape(1, D_MODEL), ln1_b[l].reshape(1, D_MODEL), 256, 512)

    pad = D_FF_PAD - D_FF
    w_up = ffn_w_up[l].astype(BF16)
    w_up = jnp.concatenate([jnp.pad(w_up[:, :D_FF], ((0, 0), (0, pad))),
                            jnp.pad(w_up[:, D_FF:], ((0, 0), (0, pad)))], axis=1)
    conv_w = jnp.pad(ffn_conv_w[l].astype(F32), ((0, 0), (0, pad)))
    conv_b = jnp.pad(ffn_conv_b[l].astype(F32), ((0, pad),)).reshape(1, D_FF_PAD)
    w_down = jnp.pad(ffn_w_down[l].astype(BF16), ((0, pad), (0, 0)))
    up = _matmul(h1_bf, w_up, BF16, 1024, 1024, 1024)
    out = _ffn_down_ln(up, conv_w, conv_b, w_down, h1,
                       ln2_g[l].reshape(1, D_MODEL), ln2_b[l].reshape(1, D_MODEL), 256, 512)
    return out[None]
```

```python
import functools
import math

import numpy as np
import jax
import jax.numpy as jnp
from jax import lax
from jax.experimental import pallas as pl
from jax.experimental.pallas import tpu as pltpu

F32 = jnp.float32
BF16 = jnp.bfloat16

D_MODEL = 4096
CHUNK = 64
SSM_WIDTH = 2048
SSM_GROUP = 16
SSM_GROUPS = SSM_WIDTH // SSM_GROUP
SSM_STATE = 64
ATT_QK_DIM = 128
ATT_V_DIM = 256
ATT_WIDTH = 2048
ATT_HEADS = 8
QK_WIDTH = 2048
D_FF = 11008
REL_BUCKETS = 32
REL_MAX_DIST = 128
DEPTH = 1
ALPHA = (2 * DEPTH) ** 0.25
LN_EPS = 1e-5
NEG_INF = -1e30
LAMBDA_INIT = 0.8 - 0.6 * math.exp(-0.3 * 0)

LANES = 128
SSM_L = 16
GROUPS_PER_TILE = LANES // SSM_GROUP
N_LANE_TILES = SSM_WIDTH // LANES
STATE_LANES = GROUPS_PER_TILE * SSM_STATE
VMEM_LIMIT = 56 * 1024 * 1024


def _cparams(n_axes, vmem=VMEM_LIMIT):
    return pltpu.CompilerParams(dimension_semantics=("arbitrary",) * n_axes,
                                vmem_limit_bytes=vmem)


def _matmul_kernel(x_ref, w_ref, o_ref, acc_ref):
    k = pl.program_id(2)

    @pl.when(k == 0)
    def _():
        acc_ref[...] = jnp.zeros_like(acc_ref)

    acc_ref[...] += jnp.dot(x_ref[...], w_ref[...], preferred_element_type=F32)

    @pl.when(k == pl.num_programs(2) - 1)
    def _():
        o_ref[...] = acc_ref[...].astype(o_ref.dtype)


def _matmul(x, w, out_dtype, tm, tn, tk, col_start=0, n_cols=None):
    m, kdim = x.shape
    n = w.shape[1] - col_start if n_cols is None else n_cols
    tm, tn, tk = min(tm, m), min(tn, n), min(tk, kdim)
    assert m % tm == 0 and n % tn == 0 and kdim % tk == 0 and col_start % tn == 0
    j0 = col_start // tn
    return pl.pallas_call(
        _matmul_kernel,
        grid=(m // tm, n // tn, kdim // tk),
        in_specs=[pl.BlockSpec((tm, tk), lambda i, j, k: (i, k)),
                  pl.BlockSpec((tk, tn), lambda i, j, k: (k, j + j0))],
        out_specs=pl.BlockSpec((tm, tn), lambda i, j, k: (i, j)),
        out_shape=jax.ShapeDtypeStruct((m, n), out_dtype),
        scratch_shapes=[pltpu.VMEM((tm, tn), F32)],
        compiler_params=_cparams(3),
        name="matmul",
    )(x, w)


def _gelu_tanh(x):
    c = math.sqrt(2.0 / math.pi)
    return 0.5 * x * (1.0 + jnp.tanh(c * (x + 0.044715 * (x * x * x))))


def _ssm_kernel(u_ref, d_ref, pc_ref, qc_ref, rep_ref, a_ref, dskip_ref, y_ref,
                urev_ref, e_ref, hin_ref, p_ref, q_ref):
    n_chunks = u_ref.shape[0] // SSM_L

    row_group = (lax.broadcasted_iota(jnp.int32, (SSM_L * LANES, LANES), 0) // SSM_GROUP) % GROUPS_PER_TILE
    for part in range(2):
        compact = pc_ref[:, part * LANES:(part + 1) * LANES]
        for t in range(STATE_LANES // LANES):
            col = part * STATE_LANES + t * LANES
            p_ref[:, col:col + LANES] = jnp.where(row_group // 2 == t, compact, jnp.zeros_like(compact))
    q_rows = 2 * STATE_LANES
    q_chunk = 4 * LANES
    q_row_group = (lax.broadcasted_iota(jnp.int32, (q_rows, q_chunk), 0) // SSM_STATE) % GROUPS_PER_TILE
    q_col_group = (lax.broadcasted_iota(jnp.int32, (q_rows, q_chunk), 1) // SSM_GROUP) % GROUPS_PER_TILE
    for cc in range(SSM_L * LANES // q_chunk):
        rep = jnp.dot(qc_ref[...], rep_ref[:, cc * q_chunk:(cc + 1) * q_chunk], preferred_element_type=F32)
        q_ref[:, cc * q_chunk:(cc + 1) * q_chunk] = jnp.where(q_row_group == q_col_group, rep, 0.0).astype(BF16)

    for i in range(SSM_L):
        rows = u_ref[pl.ds(i, n_chunks, stride=SSM_L), :]
        urev_ref[:, (SSM_L - 1 - i) * LANES:(SSM_L - i) * LANES] = rows.astype(BF16)

    e_ref[...] = jnp.dot(urev_ref[...], p_ref[...], preferred_element_type=F32)

    a_re = a_ref[0:1, :]
    a_im = a_ref[1:2, :]

    def chunk_step(c, carry):
        h_re, h_im = carry
        hin_ref[pl.ds(c, 1), 0:STATE_LANES] = h_re
        hin_ref[pl.ds(c, 1), STATE_LANES:2 * STATE_LANES] = h_im
        e_re = e_ref[pl.ds(c, 1), 0:STATE_LANES]
        e_im = e_ref[pl.ds(c, 1), STATE_LANES:2 * STATE_LANES]
        return (a_re * h_re - a_im * h_im + e_re,
                a_re * h_im + a_im * h_re + e_im)

    zero = jnp.zeros((1, STATE_LANES), F32)
    lax.fori_loop(0, n_chunks, chunk_step, (zero, zero))

    carry_in = jnp.dot(hin_ref[...].astype(BF16), q_ref[...], preferred_element_type=F32)

    dskip = dskip_ref[...]
    for i in range(SSM_L):
        intra = jnp.dot(urev_ref[:, (SSM_L - 1 - i) * LANES:], d_ref[0:(i + 1) * LANES, :],
                        preferred_element_type=F32)
        u_i = u_ref[pl.ds(i, n_chunks, stride=SSM_L), :]
        y = intra + carry_in[:, i * LANES:(i + 1) * LANES] + dskip * u_i
        y_ref[pl.ds(i, n_chunks, stride=SSM_L), :] = _gelu_tanh(y)


def _group_replicator():
    rep = np.zeros((SSM_L, SSM_GROUP, SSM_L, GROUPS_PER_TILE, SSM_GROUP), np.float32)
    for i in range(SSM_L):
        for h in range(SSM_GROUP):
            rep[i, h, i, :, h] = 1.0
    return jnp.asarray(rep.reshape(SSM_L * SSM_GROUP, SSM_L * LANES), BF16)


def _ssm_mixer(u, dstack, pcompact, qcompact, a_chunk, dskip):
    seq = u.shape[0]
    n_chunks = seq // SSM_L
    return pl.pallas_call(
        _ssm_kernel,
        grid=(N_LANE_TILES,),
        in_specs=[pl.BlockSpec((seq, LANES), lambda j: (0, j)),
                  pl.BlockSpec((None, SSM_L * LANES, LANES), lambda j: (j, 0, 0)),
                  pl.BlockSpec((None, SSM_L * LANES, 2 * LANES), lambda j: (j, 0, 0)),
                  pl.BlockSpec((None, 2 * STATE_LANES, SSM_L * SSM_GROUP), lambda j: (j, 0, 0)),
                  pl.BlockSpec((SSM_L * SSM_GROUP, SSM_L * LANES), lambda j: (0, 0)),
                  pl.BlockSpec((None, 2, STATE_LANES), lambda j: (j, 0, 0)),
                  pl.BlockSpec((None, 1, LANES), lambda j: (j, 0, 0))],
        out_specs=pl.BlockSpec((seq, LANES), lambda j: (0, j)),
        out_shape=jax.ShapeDtypeStruct((seq, SSM_WIDTH), F32),
        scratch_shapes=[pltpu.VMEM((n_chunks, SSM_L * LANES), BF16),
                        pltpu.VMEM((n_chunks, 2 * STATE_LANES), F32),
                        pltpu.VMEM((n_chunks, 2 * STATE_LANES), F32),
                        pltpu.VMEM((SSM_L * LANES, 2 * STATE_LANES), BF16),
                        pltpu.VMEM((2 * STATE_LANES, SSM_L * LANES), BF16)],
        compiler_params=_cparams(1),
        name="ssm_mixer",
    )(u, dstack, pcompact, qcompact, _group_replicator(), a_chunk, dskip)


def _ssm_operators(log_step, lam_re, lam_im, b_re, b_im, c_re, c_im):
    hi = lax.Precision.HIGHEST
    step = jnp.exp(log_step.astype(F32))[:, None]
    lr = lam_re.astype(F32)
    li = lam_im.astype(F32)
    mag = jnp.exp(lr * step)
    ab_re = mag * jnp.cos(li * step)
    ab_im = mag * jnp.sin(li * step)
    den = lr * lr + li * li
    nr = ab_re - 1.0
    ni = ab_im
    z_re = (nr * lr + ni * li) / den
    z_im = (ni * lr - nr * li) / den
    br = b_re.astype(F32)
    bi = b_im.astype(F32)
    bb_re = z_re[..., None] * br - z_im[..., None] * bi
    bb_im = z_re[..., None] * bi + z_im[..., None] * br

    pw_re = [jnp.ones_like(ab_re)]
    pw_im = [jnp.zeros_like(ab_im)]
    for _ in range(SSM_L):
        r, m = pw_re[-1], pw_im[-1]
        pw_re.append(ab_re * r - ab_im * m)
        pw_im.append(ab_re * m + ab_im * r)
    pw_re = jnp.stack(pw_re)
    pw_im = jnp.stack(pw_im)

    abb_re = pw_re[:SSM_L, :, :, None] * bb_re - pw_im[:SSM_L, :, :, None] * bb_im
    abb_im = pw_re[:SSM_L, :, :, None] * bb_im + pw_im[:SSM_L, :, :, None] * bb_re
    cr = c_re.astype(F32)
    ci = c_im.astype(F32)
    kern = (jnp.einsum('gpn,tgnh->tghp', cr, abb_re, precision=hi)
            - jnp.einsum('gpn,tgnh->tghp', ci, abb_im, precision=hi))

    eye = jnp.eye(GROUPS_PER_TILE, dtype=F32)
    nt, gl = N_LANE_TILES, GROUPS_PER_TILE

    k5 = kern.reshape(SSM_L, nt, gl, SSM_GROUP, SSM_GROUP).transpose(1, 0, 2, 3, 4)
    dstack = (k5[:, :, :, :, None, :] * eye[None, None, :, None, :, None]
              ).reshape(nt, SSM_L * LANES, LANES)

    parity = jnp.asarray(np.arange(gl)[:, None] % 2 == np.arange(2)[None, :], F32)

    def p_part(abb):
        a5 = abb.reshape(SSM_L, nt, gl, SSM_STATE, SSM_GROUP).transpose(1, 0, 2, 4, 3)
        return a5[:, :, :, :, None, :] * parity[None, None, :, None, :, None]
    pcompact = jnp.stack([p_part(abb_re), p_part(abb_im)], axis=4)
    pcompact = pcompact.reshape(nt, SSM_L * LANES, 2 * LANES)

    ca_re = cr[None] * pw_re[1:, :, None, :] - ci[None] * pw_im[1:, :, None, :]
    ca_im = cr[None] * pw_im[1:, :, None, :] + ci[None] * pw_re[1:, :, None, :]

    def q_part(ca):
        return ca.reshape(SSM_L, nt, gl, SSM_GROUP, SSM_STATE).transpose(1, 2, 4, 0, 3)
    qcompact = jnp.stack([q_part(ca_re), q_part(-ca_im)], axis=1)
    qcompact = qcompact.reshape(nt, 2 * STATE_LANES, SSM_L * SSM_GROUP)

    a_chunk = jnp.stack([pw_re[SSM_L].reshape(nt, STATE_LANES),
                         pw_im[SSM_L].reshape(nt, STATE_LANES)], axis=1)
    return dstack.astype(BF16), pcompact.astype(BF16), qcompact.astype(BF16), a_chunk


def _glu_kernel(y_ref, w_ref, b_ref, o_ref):
    y = y_ref[...]
    z = jnp.dot(y.astype(BF16), w_ref[...], preferred_element_type=F32) + b_ref[...]
    o_ref[...] = (y * jax.nn.sigmoid(z)).astype(o_ref.dtype)


def _glu(y, w, b, tm):
    m, n = y.shape
    tm = min(tm, m)
    return pl.pallas_call(
        _glu_kernel,
        grid=(m // tm,),
        in_specs=[pl.BlockSpec((tm, n), lambda i: (i, 0)),
                  pl.BlockSpec((n, n), lambda i: (0, 0)),
                  pl.BlockSpec((1, n), lambda i: (0, 0))],
        out_specs=pl.BlockSpec((tm, n), lambda i: (i, 0)),
        out_shape=jax.ShapeDtypeStruct((m, n), BF16),
        compiler_params=_cparams(1),
        name="glu",
    )(y, w, b)


ATT_TQ = 512
LOG2_E = math.log2(math.e)
FAR_BUCKET = REL_BUCKETS // 2 - 1


def _attn_kernel(rel_ref, q_ref, k_ref, v_ref, bucket_ref, lq1_ref, lk1_ref, lq2_ref, lk2_ref, g_ref,
                 o_ref, m_ref, l_ref, acc_ref, bias_ref):
    h = pl.program_id(0)
    i = pl.program_id(1)
    tq = ATT_TQ

    @pl.when(i == 0)
    def _():
        far = rel_ref[FAR_BUCKET, h]

        def rows(r, carry):
            r0 = pl.multiple_of(r * 8, 8)
            bucket = bucket_ref[pl.ds(r0, 8), :]
            tile = jnp.where(bucket < 0, NEG_INF, 0.0)
            for b in range(REL_BUCKETS):
                tile = jnp.where(bucket == b, (rel_ref[b, h] - far) * LOG2_E, tile)
            bias_ref[pl.ds(r0, 8), :] = tile
            return carry

        lax.fori_loop(0, tq // 8, rows, 0)

    m_ref[...] = jnp.full_like(m_ref, NEG_INF)
    l_ref[...] = jnp.zeros_like(l_ref)
    acc_ref[...] = jnp.zeros_like(acc_ref)

    def visit(start, width, bias):
        kb = k_ref[pl.ds(start, width), :]
        vb = v_ref[pl.ds(start, width), :]
        for c in range(2):
            q = q_ref[:, c * ATT_QK_DIM:(c + 1) * ATT_QK_DIM]
            kc = kb[:, c * ATT_QK_DIM:(c + 1) * ATT_QK_DIM]
            s = lax.dot_general(q, kc, (((1,), (1,)), ((), ())), preferred_element_type=F32)
            if bias is not None:
                s = s + bias
            m_prev = m_ref[c]
            m_new = jnp.maximum(m_prev, jnp.max(s, axis=1, keepdims=True))
            scale = jnp.exp2(m_prev - m_new)
            p = jnp.exp2(s - m_new)
            l_ref[c] = scale * l_ref[c] + jnp.sum(p, axis=1, keepdims=True)
            m_ref[c] = m_new
            acc_ref[c] = scale * acc_ref[c] + jnp.dot(p.astype(BF16), vb, preferred_element_type=F32)

    n_far = jnp.maximum(i - 1, 0)

    def far_pair(jj, carry):
        visit(pl.multiple_of(jj * (2 * tq), 2 * tq), 2 * tq, None)
        return carry

    lax.fori_loop(0, lax.shift_right_logical(n_far, 1), far_pair, 0)

    @pl.when(lax.rem(n_far, 2) == 1)
    def _():
        visit(pl.multiple_of((n_far - 1) * tq, tq), tq, None)

    @pl.when(i >= 1)
    def _():
        visit(pl.multiple_of((i - 1) * tq, tq), 2 * tq, bias_ref[...])

    @pl.when(i == 0)
    def _():
        visit(0, tq, bias_ref[:, tq:2 * tq])

    lam = (jnp.exp(jnp.sum(lq1_ref[...] * lk1_ref[...], axis=1, keepdims=True))
           - jnp.exp(jnp.sum(lq2_ref[...] * lk2_ref[...], axis=1, keepdims=True)) + LAMBDA_INIT)
    o = acc_ref[0] / l_ref[0] - lam * (acc_ref[1] / l_ref[1])
    o = o * lax.rsqrt(jnp.mean(o * o, axis=1, keepdims=True) + LN_EPS) * g_ref[...]
    o_ref[...] = (o * (1.0 - LAMBDA_INIT)).astype(o_ref.dtype)


def _t5_bucket(rel):
    half = REL_BUCKETS // 2
    max_exact = half // 2
    ret = jnp.where(rel > 0, half, 0)
    n = jnp.abs(rel)
    nf = jnp.maximum(n, 1).astype(jnp.float32)
    large = max_exact + (jnp.log(nf / max_exact) / math.log(REL_MAX_DIST / max_exact)
                         * (half - max_exact)).astype(jnp.int32)
    large = jnp.minimum(large, half - 1)
    return ret + jnp.where(n < max_exact, n, large)


def _near_buckets():
    tq = ATT_TQ
    qpos = np.arange(tq)[:, None]
    kpos = np.arange(-tq, tq)[None, :]
    rel = jnp.asarray(kpos - qpos, jnp.int32)
    visible = jnp.asarray((kpos // CHUNK) <= (qpos // CHUNK))
    return jnp.where(visible, _t5_bucket(rel), -1).astype(jnp.int32)


def _diff_attention(qkv, rel_bias, lq1, lk1, lq2, lk2, subln_g):
    seq = qkv.shape[0]
    tq = ATT_TQ
    head_blk = 2 * ATT_QK_DIM
    vec = pl.BlockSpec((1, ATT_QK_DIM), lambda h, i: (0, 0))
    return pl.pallas_call(
        _attn_kernel,
        grid=(ATT_HEADS, seq // tq),
        in_specs=[pl.BlockSpec(memory_space=pltpu.SMEM),
                  pl.BlockSpec((tq, head_blk), lambda h, i: (i, h)),
                  pl.BlockSpec((seq, head_blk), lambda h, i: (0, ATT_HEADS + h)),
                  pl.BlockSpec((seq, ATT_V_DIM), lambda h, i: (0, 2 * ATT_HEADS + h)),
                  pl.BlockSpec((tq, 2 * tq), lambda h, i: (0, 0)),
                  vec, vec, vec, vec,
                  pl.BlockSpec((1, ATT_V_DIM), lambda h, i: (0, 0))],
        out_specs=pl.BlockSpec((tq, ATT_V_DIM), lambda h, i: (i, h)),
        out_shape=jax.ShapeDtypeStruct((seq, ATT_WIDTH), BF16),
        scratch_shapes=[pltpu.VMEM((2, tq, 1), F32),
                        pltpu.VMEM((2, tq, 1), F32),
                        pltpu.VMEM((2, tq, ATT_V_DIM), F32),
                        pltpu.VMEM((tq, 2 * tq), F32)],
        compiler_params=_cparams(2),
        name="diff_attention",
    )(rel_bias.astype(F32), qkv, qkv, qkv, _near_buckets(), lq1, lk1, lq2, lk2, subln_g)


def _layer_norm_rows(r, g, b):
    mu = jnp.mean(r, axis=1, keepdims=True)
    xc = r - mu
    var = jnp.mean(xc * xc, axis=1, keepdims=True)
    return xc * lax.rsqrt(var + LN_EPS) * g + b


def _outproj_kernel(ys_ref, ya_ref, wt_ref, wb_ref, x_ref, g_ref, b_ref, h_ref, hb_ref, acc_ref):
    k = pl.program_id(1)

    @pl.when(k == 0)
    def _():
        acc_ref[...] = jnp.zeros_like(acc_ref)

    acc_ref[...] += (jnp.dot(ys_ref[...], wt_ref[...], preferred_element_type=F32)
                     + jnp.dot(ya_ref[...], wb_ref[...], preferred_element_type=F32))

    @pl.when(k == pl.num_programs(1) - 1)
    def _():
        h = _layer_norm_rows(ALPHA * x_ref[...] + acc_ref[...], g_ref[...], b_ref[...])
        h_ref[...] = h
        hb_ref[...] = h.astype(BF16)


def _outproj_ln(ys, ya, w_out, x, g, b, tm, tk):
    m, half = ys.shape
    d = w_out.shape[1]
    tm = min(tm, m)
    nk = half // tk
    row = pl.BlockSpec((1, d), lambda i, k: (0, 0))
    return pl.pallas_call(
        _outproj_kernel,
        grid=(m // tm, nk),
        in_specs=[pl.BlockSpec((tm, tk), lambda i, k: (i, k)),
                  pl.BlockSpec((tm, tk), lambda i, k: (i, k)),
                  pl.BlockSpec((tk, d), lambda i, k: (k, 0)),
                  pl.BlockSpec((tk, d), lambda i, k: (k + nk, 0)),
                  pl.BlockSpec((tm, d), lambda i, k: (i, 0)),
                  row, row],
        out_specs=[pl.BlockSpec((tm, d), lambda i, k: (i, 0)),
                   pl.BlockSpec((tm, d), lambda i, k: (i, 0))],
        out_shape=[jax.ShapeDtypeStruct((m, d), F32), jax.ShapeDtypeStruct((m, d), BF16)],
        scratch_shapes=[pltpu.VMEM((tm, d), F32)],
        compiler_params=_cparams(2),
        name="outproj_ln",
    )(ys, ya, w_out, w_out, x, g, b)


def _ffn_down_kernel(a_ref, g_ref, halo_ref, cw_ref, cb_ref, wd_ref, h_ref, lg_ref, lb_ref,
                     o_ref, acc_ref):
    i = pl.program_id(0)
    k = pl.program_id(1)

    @pl.when(k == 0)
    def _():
        acc_ref[...] = jnp.zeros_like(acc_ref)

    g = g_ref[...].astype(F32)
    halo = jnp.where(i == 0, 0.0, halo_ref[...].astype(F32))
    row = lax.broadcasted_iota(jnp.int32, g.shape, 0)
    prev1 = jnp.where(row == 0, halo[7:8, :], pltpu.roll(g, 1, 0))
    prev2 = jnp.where(row == 0, halo[6:7, :],
                      jnp.where(row == 1, halo[7:8, :], pltpu.roll(g, 2, 0)))
    gc = cb_ref[...] + prev2 * cw_ref[0:1, :] + prev1 * cw_ref[1:2, :] + g * cw_ref[2:3, :]
    act = (gc * jax.nn.sigmoid(gc)) * a_ref[...].astype(F32)
    acc_ref[...] += jnp.dot(act.astype(BF16), wd_ref[...], preferred_element_type=F32)

    @pl.when(k == pl.num_programs(1) - 1)
    def _():
        o_ref[...] = _layer_norm_rows(ALPHA * h_ref[...] + acc_ref[...], lg_ref[...], lb_ref[...])


def _ffn_down_ln(up, conv_w, conv_b, w_down, h1, g, b, tm, tk):
    m = up.shape[0]
    fp, d = w_down.shape
    tm = min(tm, m)
    nk = fp // tk
    halo_rows = 8
    row = pl.BlockSpec((1, d), lambda i, k: (0, 0))
    return pl.pallas_call(
        _ffn_down_kernel,
        grid=(m // tm, nk),
        in_specs=[pl.BlockSpec((tm, tk), lambda i, k: (i, k)),
                  pl.BlockSpec((tm, tk), lambda i, k: (i, k + nk)),
                  pl.BlockSpec((halo_rows, tk),
                               lambda i, k: (jnp.maximum(i * (tm // halo_rows) - 1, 0), k + nk)),
                  pl.BlockSpec((3, tk), lambda i, k: (0, k)),
                  pl.BlockSpec((1, tk), lambda i, k: (0, k)),
                  pl.BlockSpec((tk, d), lambda i, k: (k, 0)),
                  pl.BlockSpec((tm, d), lambda i, k: (i, 0)),
                  row, row],
        out_specs=pl.BlockSpec((tm, d), lambda i, k: (i, 0)),
        out_shape=jax.ShapeDtypeStruct((m, d), F32),
        scratch_shapes=[pltpu.VMEM((tm, d), F32)],
        compiler_params=_cparams(2),
        name="ffn_down_ln",
    )(up, up, up, conv_w, conv_b, w_down, h1, g, b)


def kernel(x, w_in, ssm_log_step, ssm_lambda_re, ssm_lambda_im, ssm_b_re, ssm_b_im, ssm_c_re, ssm_c_im, ssm_d, ssm_w_glu, ssm_b_glu, att_lambda_q1, att_lambda_k1, att_lambda_q2, att_lambda_k2, att_subln_g, rel_bias, w_out, ln1_g, ln1_b, ffn_w_up, ffn_conv_w, ffn_conv_b, ffn_w_down, ln2_g, ln2_b):
    bsz, seq, _ = x.shape
    assert bsz == 1 and DEPTH == 1
    l = 0
    xs = x[0]

    qk_scale = ATT_QK_DIM ** -0.5 * LOG2_E
    col_scale = np.ones((1, w_in.shape[2]), np.float32)
    col_scale[:, SSM_WIDTH:SSM_WIDTH + QK_WIDTH] = qk_scale
    w_in_bf = (w_in[l] * jnp.asarray(col_scale)).astype(BF16)
    x_bf = xs.astype(BF16)

    u = _matmul(x_bf, w_in_bf, F32, 1024, 1024, 1024, 0, SSM_WIDTH)
    qkv = _matmul(x_bf, w_in_bf, BF16, 1024, 1024, 1024, SSM_WIDTH, w_in.shape[2] - SSM_WIDTH)

    dstack, pcompact, qcompact, a_chunk = _ssm_operators(
        ssm_log_step[l], ssm_lambda_re[l], ssm_lambda_im[l], ssm_b_re[l], ssm_b_im[l],
        ssm_c_re[l], ssm_c_im[l])
    dskip = ssm_d[l].astype(F32).reshape(N_LANE_TILES, 1, LANES)
    y = _ssm_mixer(u, dstack, pcompact, qcompact, a_chunk, dskip)
    y_ssm = _glu(y, ssm_w_glu[l].astype(BF16), ssm_b_glu[l].astype(F32).reshape(1, SSM_WIDTH), 512)

    vec = lambda a: a.astype(F32).reshape(1, ATT_QK_DIM)
    y_att = _diff_attention(qkv, rel_bias,
                            vec(att_lambda_q1[l]), vec(att_lambda_k1[l]),
                            vec(att_lambda_q2[l]), vec(att_lambda_k2[l]),
                            att_subln_g[l].astype(F32).reshape(1, ATT_V_DIM))

    h1, h1_bf = _outproj_ln(y_ssm, y_att, w_out[l].astype(BF16), xs,
                            ln1_g[l].reshape(1, D_MODEL), ln1_b[l].reshape(1, D_MODEL), 256, 512)

    up = _matmul(h1_bf, ffn_w_up[l].astype(BF16), BF16, 1024, 512, 1024)
    out = _ffn_down_ln(up, ffn_conv_w[l].astype(F32), ffn_conv_b[l].astype(F32).reshape(1, D_FF),
                       ffn_w_down[l].astype(BF16), h1,
                       ln2_g[l].reshape(1, D_MODEL), ln2_b[l].reshape(1, D_MODEL), 512, 256)
    return out[None]
```

```python
import functools
import math

import numpy as np
import jax
import jax.numpy as jnp
from jax import lax
from jax.experimental import pallas as pl
from jax.experimental.pallas import tpu as pltpu

F32 = jnp.float32
BF16 = jnp.bfloat16

D_MODEL = 4096
CHUNK = 64
SSM_WIDTH = 2048
SSM_GROUP = 16
SSM_GROUPS = SSM_WIDTH // SSM_GROUP
SSM_STATE = 64
ATT_QK_DIM = 128
ATT_V_DIM = 256
ATT_WIDTH = 2048
ATT_HEADS = 8
QK_WIDTH = 2048
D_FF = 11008
REL_BUCKETS = 32
REL_MAX_DIST = 128
DEPTH = 1
ALPHA = (2 * DEPTH) ** 0.25
LN_EPS = 1e-5
NEG_INF = -1e30
LAMBDA_INIT = 0.8 - 0.6 * math.exp(-0.3 * 0)

LANES = 128
SSM_L = 16
GROUPS_PER_TILE = LANES // SSM_GROUP
N_LANE_TILES = SSM_WIDTH // LANES
STATE_LANES = GROUPS_PER_TILE * SSM_STATE
VMEM_LIMIT = 56 * 1024 * 1024
VMEM_LIMIT_LARGE = 62 * 1024 * 1024


def _cparams(n_axes, vmem=VMEM_LIMIT):
    return pltpu.CompilerParams(dimension_semantics=("arbitrary",) * n_axes,
                                vmem_limit_bytes=vmem)


def _matmul_kernel(x_ref, w_ref, o_ref, acc_ref):
    k = pl.program_id(2)

    @pl.when(k == 0)
    def _():
        acc_ref[...] = jnp.zeros_like(acc_ref)

    acc_ref[...] += jnp.dot(x_ref[...], w_ref[...], preferred_element_type=F32)

    @pl.when(k == pl.num_programs(2) - 1)
    def _():
        o_ref[...] = acc_ref[...].astype(o_ref.dtype)


def _matmul(x, w, out_dtype, tm, tn, tk, col_start=0, n_cols=None):
    m, kdim = x.shape
    n = w.shape[1] - col_start if n_cols is None else n_cols
    tm, tn, tk = min(tm, m), min(tn, n), min(tk, kdim)
    assert m % tm == 0 and n % tn == 0 and kdim % tk == 0 and col_start % tn == 0
    j0 = col_start // tn
    return pl.pallas_call(
        _matmul_kernel,
        grid=(m // tm, n // tn, kdim // tk),
        in_specs=[pl.BlockSpec((tm, tk), lambda i, j, k: (i, k)),
                  pl.BlockSpec((tk, tn), lambda i, j, k: (k, j + j0))],
        out_specs=pl.BlockSpec((tm, tn), lambda i, j, k: (i, j)),
        out_shape=jax.ShapeDtypeStruct((m, n), out_dtype),
        scratch_shapes=[pltpu.VMEM((tm, tn), F32)],
        compiler_params=_cparams(3),
        name="matmul",
    )(x, w)


def _gelu_tanh(x):
    c = math.sqrt(2.0 / math.pi)
    return 0.5 * x * (1.0 + jnp.tanh(c * (x + 0.044715 * (x * x * x))))


def _ssm_kernel(u_ref, d_ref, pc_ref, qc_ref, rep_ref, a_ref, dskip_ref, y_ref,
                urev_ref, e_ref, hin_ref, p_ref, q_ref):
    n_chunks = u_ref.shape[0] // SSM_L

    row_group = (lax.broadcasted_iota(jnp.int32, (SSM_L * LANES, LANES), 0) // SSM_GROUP) % GROUPS_PER_TILE
    for part in range(2):
        compact = pc_ref[:, part * LANES:(part + 1) * LANES]
        for t in range(STATE_LANES // LANES):
            col = part * STATE_LANES + t * LANES
            p_ref[:, col:col + LANES] = jnp.where(row_group // 2 == t, compact, jnp.zeros_like(compact))
    q_rows = 2 * STATE_LANES
    q_chunk = 4 * LANES
    q_row_group = (lax.broadcasted_iota(jnp.int32, (q_rows, q_chunk), 0) // SSM_STATE) % GROUPS_PER_TILE
    q_col_group = (lax.broadcasted_iota(jnp.int32, (q_rows, q_chunk), 1) // SSM_GROUP) % GROUPS_PER_TILE
    for cc in range(SSM_L * LANES // q_chunk):
        rep = jnp.dot(qc_ref[...], rep_ref[:, cc * q_chunk:(cc + 1) * q_chunk], preferred_element_type=F32)
        q_ref[:, cc * q_chunk:(cc + 1) * q_chunk] = jnp.where(q_row_group == q_col_group, rep, 0.0).astype(BF16)

    for i in range(SSM_L):
        rows = u_ref[pl.ds(i, n_chunks, stride=SSM_L), :]
        urev_ref[:, (SSM_L - 1 - i) * LANES:(SSM_L - i) * LANES] = rows.astype(BF16)

    e_ref[...] = jnp.dot(urev_ref[...], p_ref[...], preferred_element_type=F32)

    a_re = a_ref[0:1, :]
    a_im = a_ref[1:2, :]

    def chunk_step(c, carry):
        h_re, h_im = carry
        hin_ref[pl.ds(c, 1), 0:STATE_LANES] = h_re
        hin_ref[pl.ds(c, 1), STATE_LANES:2 * STATE_LANES] = h_im
        e_re = e_ref[pl.ds(c, 1), 0:STATE_LANES]
        e_im = e_ref[pl.ds(c, 1), STATE_LANES:2 * STATE_LANES]
        return (a_re * h_re - a_im * h_im + e_re,
                a_re * h_im + a_im * h_re + e_im)

    zero = jnp.zeros((1, STATE_LANES), F32)
    lax.fori_loop(0, n_chunks, chunk_step, (zero, zero))

    carry_in = jnp.dot(hin_ref[...].astype(BF16), q_ref[...], preferred_element_type=F32)

    dskip = dskip_ref[...]
    for i in range(SSM_L):
        intra = jnp.dot(urev_ref[:, (SSM_L - 1 - i) * LANES:], d_ref[0:(i + 1) * LANES, :],
                        preferred_element_type=F32)
        u_i = u_ref[pl.ds(i, n_chunks, stride=SSM_L), :]
        y = intra + carry_in[:, i * LANES:(i + 1) * LANES] + dskip * u_i
        y_ref[pl.ds(i, n_chunks, stride=SSM_L), :] = _gelu_tanh(y)


def _group_replicator():
    rep = np.zeros((SSM_L, SSM_GROUP, SSM_L, GROUPS_PER_TILE, SSM_GROUP), np.float32)
    for i in range(SSM_L):
        for h in range(SSM_GROUP):
            rep[i, h, i, :, h] = 1.0
    return jnp.asarray(rep.reshape(SSM_L * SSM_GROUP, SSM_L * LANES), BF16)


def _ssm_mixer(u, dstack, pcompact, qcompact, a_chunk, dskip):
    seq = u.shape[0]
    n_chunks = seq // SSM_L
    return pl.pallas_call(
        _ssm_kernel,
        grid=(N_LANE_TILES,),
        in_specs=[pl.BlockSpec((seq, LANES), lambda j: (0, j)),
                  pl.BlockSpec((None, SSM_L * LANES, LANES), lambda j: (j, 0, 0)),
                  pl.BlockSpec((None, SSM_L * LANES, 2 * LANES), lambda j: (j, 0, 0)),
                  pl.BlockSpec((None, 2 * STATE_LANES, SSM_L * SSM_GROUP), lambda j: (j, 0, 0)),
                  pl.BlockSpec((SSM_L * SSM_GROUP, SSM_L * LANES), lambda j: (0, 0)),
                  pl.BlockSpec((None, 2, STATE_LANES), lambda j: (j, 0, 0)),
                  pl.BlockSpec((None, 1, LANES), lambda j: (j, 0, 0))],
        out_specs=pl.BlockSpec((seq, LANES), lambda j: (0, j)),
        out_shape=jax.ShapeDtypeStruct((seq, SSM_WIDTH), F32),
        scratch_shapes=[pltpu.VMEM((n_chunks, SSM_L * LANES), BF16),
                        pltpu.VMEM((n_chunks, 2 * STATE_LANES), F32),
                        pltpu.VMEM((n_chunks, 2 * STATE_LANES), F32),
                        pltpu.VMEM((SSM_L * LANES, 2 * STATE_LANES), BF16),
                        pltpu.VMEM((2 * STATE_LANES, SSM_L * LANES), BF16)],
        compiler_params=_cparams(1),
        name="ssm_mixer",
    )(u, dstack, pcompact, qcompact, _group_replicator(), a_chunk, dskip)


def _ssm_operators(log_step, lam_re, lam_im, b_re, b_im, c_re, c_im):
    hi = lax.Precision.HIGHEST
    step = jnp.exp(log_step.astype(F32))[:, None]
    lr = lam_re.astype(F32)
    li = lam_im.astype(F32)
    mag = jnp.exp(lr * step)
    ab_re = mag * jnp.cos(li * step)
    ab_im = mag * jnp.sin(li * step)
    den = lr * lr + li * li
    nr = ab_re - 1.0
    ni = ab_im
    z_re = (nr * lr + ni * li) / den
    z_im = (ni * lr - nr * li) / den
    br = b_re.astype(F32)
    bi = b_im.astype(F32)
    bb_re = z_re[..., None] * br - z_im[..., None] * bi
    bb_im = z_re[..., None] * bi + z_im[..., None] * br

    pw_re = [jnp.ones_like(ab_re)]
    pw_im = [jnp.zeros_like(ab_im)]
    for _ in range(SSM_L):
        r, m = pw_re[-1], pw_im[-1]
        pw_re.append(ab_re * r - ab_im * m)
        pw_im.append(ab_re * m + ab_im * r)
    pw_re = jnp.stack(pw_re)
    pw_im = jnp.stack(pw_im)

    abb_re = pw_re[:SSM_L, :, :, None] * bb_re - pw_im[:SSM_L, :, :, None] * bb_im
    abb_im = pw_re[:SSM_L, :, :, None] * bb_im + pw_im[:SSM_L, :, :, None] * bb_re
    cr = c_re.astype(F32)
    ci = c_im.astype(F32)
    kern = (jnp.einsum('gpn,tgnh->tghp', cr, abb_re, precision=hi)
            - jnp.einsum('gpn,tgnh->tghp', ci, abb_im, precision=hi))

    eye = jnp.eye(GROUPS_PER_TILE, dtype=F32)
    nt, gl = N_LANE_TILES, GROUPS_PER_TILE

    k5 = kern.reshape(SSM_L, nt, gl, SSM_GROUP, SSM_GROUP).transpose(1, 0, 2, 3, 4)
    dstack = (k5[:, :, :, :, None, :] * eye[None, None, :, None, :, None]
              ).reshape(nt, SSM_L * LANES, LANES)

    parity = jnp.asarray(np.arange(gl)[:, None] % 2 == np.arange(2)[None, :], F32)

    def p_part(abb):
        a5 = abb.reshape(SSM_L, nt, gl, SSM_STATE, SSM_GROUP).transpose(1, 0, 2, 4, 3)
        return a5[:, :, :, :, None, :] * parity[None, None, :, None, :, None]
    pcompact = jnp.stack([p_part(abb_re), p_part(abb_im)], axis=4)
    pcompact = pcompact.reshape(nt, SSM_L * LANES, 2 * LANES)

    ca_re = cr[None] * pw_re[1:, :, None, :] - ci[None] * pw_im[1:, :, None, :]
    ca_im = cr[None] * pw_im[1:, :, None, :] + ci[None] * pw_re[1:, :, None, :]

    def q_part(ca):
        return ca.reshape(SSM_L, nt, gl, SSM_GROUP, SSM_STATE).transpose(1, 2, 4, 0, 3)
    qcompact = jnp.stack([q_part(ca_re), q_part(-ca_im)], axis=1)
    qcompact = qcompact.reshape(nt, 2 * STATE_LANES, SSM_L * SSM_GROUP)

    a_chunk = jnp.stack([pw_re[SSM_L].reshape(nt, STATE_LANES),
                         pw_im[SSM_L].reshape(nt, STATE_LANES)], axis=1)
    return dstack.astype(BF16), pcompact.astype(BF16), qcompact.astype(BF16), a_chunk


def _glu_kernel(y_ref, w_ref, b_ref, o_ref):
    y = y_ref[...]
    z = jnp.dot(y.astype(BF16), w_ref[...], preferred_element_type=F32) + b_ref[...]
    o_ref[...] = (y * jax.nn.sigmoid(z)).astype(o_ref.dtype)


def _glu(y, w, b, tm):
    m, n = y.shape
    tm = min(tm, m)
    return pl.pallas_call(
        _glu_kernel,
        grid=(m // tm,),
        in_specs=[pl.BlockSpec((tm, n), lambda i: (i, 0)),
                  pl.BlockSpec((n, n), lambda i: (0, 0)),
                  pl.BlockSpec((1, n), lambda i: (0, 0))],
        out_specs=pl.BlockSpec((tm, n), lambda i: (i, 0)),
        out_shape=jax.ShapeDtypeStruct((m, n), BF16),
        compiler_params=_cparams(1),
        name="glu",
    )(y, w, b)


ATT_TQ = 512
LOG2_E = math.log2(math.e)
FAR_BUCKET = REL_BUCKETS // 2 - 1


def _attn_kernel(rel_ref, q_ref, k_ref, v_ref, bucket_ref, lq1_ref, lk1_ref, lq2_ref, lk2_ref, g_ref,
                 o_ref, m_ref, l_ref, acc_ref, bias_ref):
    h = pl.program_id(0)
    i = pl.program_id(1)
    tq = ATT_TQ

    @pl.when(i == 0)
    def _():
        far = rel_ref[FAR_BUCKET, h]

        def rows(r, carry):
            r0 = pl.multiple_of(r * 8, 8)
            bucket = bucket_ref[pl.ds(r0, 8), :]
            tile = jnp.where(bucket < 0, NEG_INF, 0.0)
            for b in range(REL_BUCKETS):
                tile = jnp.where(bucket == b, (rel_ref[b, h] - far) * LOG2_E, tile)
            bias_ref[pl.ds(r0, 8), :] = tile
            return carry

        lax.fori_loop(0, tq // 8, rows, 0)

    m_ref[...] = jnp.full_like(m_ref, NEG_INF)
    l_ref[...] = jnp.zeros_like(l_ref)
    acc_ref[...] = jnp.zeros_like(acc_ref)

    def visit(start, width, bias):
        kb = k_ref[pl.ds(start, width), :]
        vb = v_ref[pl.ds(start, width), :]
        for c in range(2):
            q = q_ref[:, c * ATT_QK_DIM:(c + 1) * ATT_QK_DIM]
            kc = kb[:, c * ATT_QK_DIM:(c + 1) * ATT_QK_DIM]
            s = lax.dot_general(q, kc, (((1,), (1,)), ((), ())), preferred_element_type=F32)
            if bias is not None:
                s = s + bias
            m_prev = m_ref[c]
            m_new = jnp.maximum(m_prev, jnp.max(s, axis=1, keepdims=True))
            scale = jnp.exp2(m_prev - m_new)
            p = jnp.exp2(s - m_new)
            l_ref[c] = scale * l_ref[c] + jnp.sum(p, axis=1, keepdims=True)
            m_ref[c] = m_new
            acc_ref[c] = scale * acc_ref[c] + jnp.dot(p.astype(BF16), vb, preferred_element_type=F32)

    n_far = jnp.maximum(i - 1, 0)

    def far_pair(jj, carry):
        visit(pl.multiple_of(jj * (2 * tq), 2 * tq), 2 * tq, None)
        return carry

    lax.fori_loop(0, lax.shift_right_logical(n_far, 1), far_pair, 0)

    @pl.when(lax.rem(n_far, 2) == 1)
    def _():
        visit(pl.multiple_of((n_far - 1) * tq, tq), tq, None)

    @pl.when(i >= 1)
    def _():
        visit(pl.multiple_of((i - 1) * tq, tq), 2 * tq, bias_ref[...])

    @pl.when(i == 0)
    def _():
        visit(0, tq, bias_ref[:, tq:2 * tq])

    lam = (jnp.exp(jnp.sum(lq1_ref[...] * lk1_ref[...], axis=1, keepdims=True))
           - jnp.exp(jnp.sum(lq2_ref[...] * lk2_ref[...], axis=1, keepdims=True)) + LAMBDA_INIT)
    o = acc_ref[0] / l_ref[0] - lam * (acc_ref[1] / l_ref[1])
    o = o * lax.rsqrt(jnp.mean(o * o, axis=1, keepdims=True) + LN_EPS) * g_ref[...]
    o_ref[...] = (o * (1.0 - LAMBDA_INIT)).astype(o_ref.dtype)


def _t5_bucket(rel):
    half = REL_BUCKETS // 2
    max_exact = half // 2
    ret = jnp.where(rel > 0, half, 0)
    n = jnp.abs(rel)
    nf = jnp.maximum(n, 1).astype(jnp.float32)
    large = max_exact + (jnp.log(nf / max_exact) / math.log(REL_MAX_DIST / max_exact)
                         * (half - max_exact)).astype(jnp.int32)
    large = jnp.minimum(large, half - 1)
    return ret + jnp.where(n < max_exact, n, large)


def _near_buckets():
    tq = ATT_TQ
    qpos = np.arange(tq)[:, None]
    kpos = np.arange(-tq, tq)[None, :]
    rel = jnp.asarray(kpos - qpos, jnp.int32)
    visible = jnp.asarray((kpos // CHUNK) <= (qpos // CHUNK))
    return jnp.where(visible, _t5_bucket(rel), -1).astype(jnp.int32)


def _diff_attention(qkv, rel_bias, lq1, lk1, lq2, lk2, subln_g):
    seq = qkv.shape[0]
    tq = ATT_TQ
    head_blk = 2 * ATT_QK_DIM
    vec = pl.BlockSpec((1, ATT_QK_DIM), lambda h, i: (0, 0))
    return pl.pallas_call(
        _attn_kernel,
        grid=(ATT_HEADS, seq // tq),
        in_specs=[pl.BlockSpec(memory_space=pltpu.SMEM),
                  pl.BlockSpec((tq, head_blk), lambda h, i: (i, h)),
                  pl.BlockSpec((seq, head_blk), lambda h, i: (0, ATT_HEADS + h)),
                  pl.BlockSpec((seq, ATT_V_DIM), lambda h, i: (0, 2 * ATT_HEADS + h)),
                  pl.BlockSpec((tq, 2 * tq), lambda h, i: (0, 0)),
                  vec, vec, vec, vec,
                  pl.BlockSpec((1, ATT_V_DIM), lambda h, i: (0, 0))],
        out_specs=pl.BlockSpec((tq, ATT_V_DIM), lambda h, i: (i, h)),
        out_shape=jax.ShapeDtypeStruct((seq, ATT_WIDTH), BF16),
        scratch_shapes=[pltpu.VMEM((2, tq, 1), F32),
                        pltpu.VMEM((2, tq, 1), F32),
                        pltpu.VMEM((2, tq, ATT_V_DIM), F32),
                        pltpu.VMEM((tq, 2 * tq), F32)],
        compiler_params=_cparams(2),
        name="diff_attention",
    )(rel_bias.astype(F32), qkv, qkv, qkv, _near_buckets(), lq1, lk1, lq2, lk2, subln_g)


MM_COL_CHUNK = 1024
LN_ROW_CHUNK = 128


def _residual_layer_norm(res_ref, acc_ref, g_ref, b_ref, out_refs):
    g = g_ref[...]
    b = b_ref[...]
    for r0 in range(0, acc_ref.shape[0], LN_ROW_CHUNK):
        rows = slice(r0, r0 + LN_ROW_CHUNK)
        r = ALPHA * res_ref[rows, :] + acc_ref[rows, :]
        mu = jnp.mean(r, axis=1, keepdims=True)
        xc = r - mu
        var = jnp.mean(xc * xc, axis=1, keepdims=True)
        h = xc * lax.rsqrt(var + LN_EPS) * g + b
        for o_ref in out_refs:
            o_ref[rows, :] = h.astype(o_ref.dtype)


def _outproj_kernel(ys_ref, ya_ref, wt_ref, wb_ref, x_ref, g_ref, b_ref, h_ref, hb_ref):
    k = pl.program_id(1)

    @pl.when(k == 0)
    def _():
        h_ref[...] = jnp.zeros_like(h_ref)

    for c in range(0, h_ref.shape[1], MM_COL_CHUNK):
        cols = slice(c, c + MM_COL_CHUNK)
        h_ref[:, cols] += (jnp.dot(ys_ref[...], wt_ref[:, cols], preferred_element_type=F32)
                           + jnp.dot(ya_ref[...], wb_ref[:, cols], preferred_element_type=F32))

    @pl.when(k == pl.num_programs(1) - 1)
    def _():
        _residual_layer_norm(x_ref, h_ref, g_ref, b_ref, (h_ref, hb_ref))


def _outproj_ln(ys, ya, w_out, x, g, b, tm, tk):
    m, half = ys.shape
    d = w_out.shape[1]
    tm = min(tm, m)
    nk = half // tk
    row = pl.BlockSpec((1, d), lambda i, k: (0, 0))
    return pl.pallas_call(
        _outproj_kernel,
        grid=(m // tm, nk),
        in_specs=[pl.BlockSpec((tm, tk), lambda i, k: (i, k)),
                  pl.BlockSpec((tm, tk), lambda i, k: (i, k)),
                  pl.BlockSpec((tk, d), lambda i, k: (k, 0)),
                  pl.BlockSpec((tk, d), lambda i, k: (k + nk, 0)),
                  pl.BlockSpec((tm, d), lambda i, k: (i, 0), pipeline_mode=pl.Buffered(1)),
                  row, row],
        out_specs=[pl.BlockSpec((tm, d), lambda i, k: (i, 0)),
                   pl.BlockSpec((tm, d), lambda i, k: (i, 0))],
        out_shape=[jax.ShapeDtypeStruct((m, d), F32), jax.ShapeDtypeStruct((m, d), BF16)],
        compiler_params=_cparams(2, VMEM_LIMIT_LARGE),
        name="outproj_ln",
    )(ys, ya, w_out, w_out, x, g, b)


FFN_TF = 256
FFN_HALO = 8


def _ffn_kernel(h_ref, wa_ref, wg_ref, cw_ref, cb_ref, wd_ref, r_ref, lg_ref, lb_ref,
                o_ref, halo_ref):
    i = pl.program_id(0)
    f = pl.program_id(1)
    tm = h_ref.shape[0]

    h = h_ref[...]
    value = jnp.dot(h, wa_ref[...], preferred_element_type=F32)
    gate = jnp.dot(h, wg_ref[...], preferred_element_type=F32)

    halo = jnp.where(i == 0, 0.0, halo_ref[f])
    halo_ref[f] = gate[tm - FFN_HALO:tm, :]
    row = lax.broadcasted_iota(jnp.int32, gate.shape, 0)
    prev1 = jnp.where(row == 0, halo[FFN_HALO - 1:FFN_HALO, :], pltpu.roll(gate, 1, 0))
    prev2 = jnp.where(row == 0, halo[FFN_HALO - 2:FFN_HALO - 1, :],
                      jnp.where(row == 1, halo[FFN_HALO - 1:FFN_HALO, :], pltpu.roll(gate, 2, 0)))
    gc = cb_ref[...] + prev2 * cw_ref[0:1, :] + prev1 * cw_ref[1:2, :] + gate * cw_ref[2:3, :]
    act = ((gc * jax.nn.sigmoid(gc)) * value).astype(BF16)

    @pl.when(f == 0)
    def _():
        o_ref[...] = jnp.zeros_like(o_ref)

    for c in range(0, o_ref.shape[1], MM_COL_CHUNK):
        cols = slice(c, c + MM_COL_CHUNK)
        o_ref[:, cols] += jnp.dot(act, wd_ref[:, cols], preferred_element_type=F32)

    @pl.when(f == pl.num_programs(1) - 1)
    def _():
        _residual_layer_norm(r_ref, o_ref, lg_ref, lb_ref, (o_ref,))


def _ffn_ln(h_bf, w_up, conv_w, conv_b, w_down, h1, g, b, tm):
    m, d = h_bf.shape
    tm = min(tm, m)
    nf = D_FF // FFN_TF
    row = pl.BlockSpec((1, d), lambda i, f: (0, 0))
    return pl.pallas_call(
        _ffn_kernel,
        grid=(m // tm, nf),
        in_specs=[pl.BlockSpec((tm, d), lambda i, f: (i, 0)),
                  pl.BlockSpec((d, FFN_TF), lambda i, f: (0, f)),
                  pl.BlockSpec((d, FFN_TF), lambda i, f: (0, f + nf)),
                  pl.BlockSpec((3, FFN_TF), lambda i, f: (0, f)),
                  pl.BlockSpec((1, FFN_TF), lambda i, f: (0, f)),
                  pl.BlockSpec((FFN_TF, d), lambda i, f: (f, 0)),
                  pl.BlockSpec((tm, d), lambda i, f: (i, 0), pipeline_mode=pl.Buffered(1)),
                  row, row],
        out_specs=pl.BlockSpec((tm, d), lambda i, f: (i, 0)),
        out_shape=jax.ShapeDtypeStruct((m, d), F32),
        scratch_shapes=[pltpu.VMEM((nf, FFN_HALO, FFN_TF), F32)],
        compiler_params=_cparams(2),
        name="ffn_ln",
    )(h_bf, w_up, w_up, conv_w, conv_b, w_down, h1, g, b)


def kernel(x, w_in, ssm_log_step, ssm_lambda_re, ssm_lambda_im, ssm_b_re, ssm_b_im, ssm_c_re, ssm_c_im, ssm_d, ssm_w_glu, ssm_b_glu, att_lambda_q1, att_lambda_k1, att_lambda_q2, att_lambda_k2, att_subln_g, rel_bias, w_out, ln1_g, ln1_b, ffn_w_up, ffn_conv_w, ffn_conv_b, ffn_w_down, ln2_g, ln2_b):
    bsz, seq, _ = x.shape
    assert bsz == 1 and DEPTH == 1
    l = 0
    xs = x[0]

    qk_scale = ATT_QK_DIM ** -0.5 * LOG2_E
    col_scale = np.ones((1, w_in.shape[2]), np.float32)
    col_scale[:, SSM_WIDTH:SSM_WIDTH + QK_WIDTH] = qk_scale
    w_in_bf = (w_in[l] * jnp.asarray(col_scale)).astype(BF16)
    x_bf = xs.astype(BF16)

    u = _matmul(x_bf, w_in_bf, F32, 1024, 1024, 1024, 0, SSM_WIDTH)
    qkv = _matmul(x_bf, w_in_bf, BF16, 1024, 1024, 1024, SSM_WIDTH, w_in.shape[2] - SSM_WIDTH)

    dstack, pcompact, qcompact, a_chunk = _ssm_operators(
        ssm_log_step[l], ssm_lambda_re[l], ssm_lambda_im[l], ssm_b_re[l], ssm_b_im[l],
        ssm_c_re[l], ssm_c_im[l])
    dskip = ssm_d[l].astype(F32).reshape(N_LANE_TILES, 1, LANES)
    y = _ssm_mixer(u, dstack, pcompact, qcompact, a_chunk, dskip)
    y_ssm = _glu(y, ssm_w_glu[l].astype(BF16), ssm_b_glu[l].astype(F32).reshape(1, SSM_WIDTH), 512)

    vec = lambda a: a.astype(F32).reshape(1, ATT_QK_DIM)
    y_att = _diff_attention(qkv, rel_bias,
                            vec(att_lambda_q1[l]), vec(att_lambda_k1[l]),
                            vec(att_lambda_q2[l]), vec(att_lambda_k2[l]),
                            att_subln_g[l].astype(F32).reshape(1, ATT_V_DIM))

    h1, h1_bf = _outproj_ln(y_ssm, y_att, w_out[l].astype(BF16), xs,
                            ln1_g[l].reshape(1, D_MODEL), ln1_b[l].reshape(1, D_MODEL), 512, 512)

    out = _ffn_ln(h1_bf, ffn_w_up[l].astype(BF16), ffn_conv_w[l].astype(F32),
                  ffn_conv_b[l].astype(F32).reshape(1, D_FF), ffn_w_down[l].astype(BF16), h1,
                  ln2_g[l].reshape(1, D_MODEL), ln2_b[l].reshape(1, D_MODEL), 512)
    return out[None]
```

```python
import functools
import math

import numpy as np
import jax
import jax.numpy as jnp
from jax import lax
from jax.experimental import pallas as pl
from jax.experimental.pallas import tpu as pltpu

F32 = jnp.float32
BF16 = jnp.bfloat16

D_MODEL = 4096
CHUNK = 64
SSM_WIDTH = 2048
SSM_GROUP = 16
SSM_GROUPS = SSM_WIDTH // SSM_GROUP
SSM_STATE = 64
ATT_QK_DIM = 128
ATT_V_DIM = 256
ATT_WIDTH = 2048
ATT_HEADS = 8
QK_WIDTH = 2048
D_FF = 11008
REL_BUCKETS = 32
REL_MAX_DIST = 128
DEPTH = 1
ALPHA = (2 * DEPTH) ** 0.25
LN_EPS = 1e-5
NEG_INF = -1e30
LAMBDA_INIT = 0.8 - 0.6 * math.exp(-0.3 * 0)

LANES = 128
SSM_L = 16
GROUPS_PER_TILE = LANES // SSM_GROUP
N_LANE_TILES = SSM_WIDTH // LANES
STATE_LANES = GROUPS_PER_TILE * SSM_STATE
VMEM_LIMIT = 56 * 1024 * 1024
VMEM_LIMIT_LARGE = 62 * 1024 * 1024


def _cparams(n_axes, vmem=VMEM_LIMIT):
    return pltpu.CompilerParams(dimension_semantics=("arbitrary",) * n_axes,
                                vmem_limit_bytes=vmem)


def _matmul_kernel(x_ref, w_ref, o_ref, acc_ref):
    k = pl.program_id(2)

    @pl.when(k == 0)
    def _():
        acc_ref[...] = jnp.zeros_like(acc_ref)

    acc_ref[...] += jnp.dot(x_ref[...], w_ref[...], preferred_element_type=F32)

    @pl.when(k == pl.num_programs(2) - 1)
    def _():
        o_ref[...] = acc_ref[...].astype(o_ref.dtype)


def _matmul(x, w, out_dtype, tm, tn, tk, col_start=0, n_cols=None):
    m, kdim = x.shape
    n = w.shape[1] - col_start if n_cols is None else n_cols
    tm, tn, tk = min(tm, m), min(tn, n), min(tk, kdim)
    assert m % tm == 0 and n % tn == 0 and kdim % tk == 0 and col_start % tn == 0
    j0 = col_start // tn
    return pl.pallas_call(
        _matmul_kernel,
        grid=(m // tm, n // tn, kdim // tk),
        in_specs=[pl.BlockSpec((tm, tk), lambda i, j, k: (i, k)),
                  pl.BlockSpec((tk, tn), lambda i, j, k: (k, j + j0))],
        out_specs=pl.BlockSpec((tm, tn), lambda i, j, k: (i, j)),
        out_shape=jax.ShapeDtypeStruct((m, n), out_dtype),
        scratch_shapes=[pltpu.VMEM((tm, tn), F32)],
        compiler_params=_cparams(3),
        name="matmul",
    )(x, w)


def _gelu_tanh(x):
    c = math.sqrt(2.0 / math.pi)
    return 0.5 * x * (1.0 + jnp.tanh(c * (x + 0.044715 * (x * x * x))))


def _ssm_kernel(u_ref, d_ref, pc_ref, qc_ref, rep_ref, a_ref, dskip_ref, y_ref,
                urev_ref, e_ref, hin_ref, p_ref, q_ref):
    n_chunks = u_ref.shape[0] // SSM_L

    row_group = (lax.broadcasted_iota(jnp.int32, (SSM_L * LANES, LANES), 0) // SSM_GROUP) % GROUPS_PER_TILE
    for part in range(2):
        compact = pc_ref[:, part * LANES:(part + 1) * LANES]
        for t in range(STATE_LANES // LANES):
            col = part * STATE_LANES + t * LANES
            p_ref[:, col:col + LANES] = jnp.where(row_group // 2 == t, compact, jnp.zeros_like(compact))
    q_rows = 2 * STATE_LANES
    q_chunk = 4 * LANES
    q_row_group = (lax.broadcasted_iota(jnp.int32, (q_rows, q_chunk), 0) // SSM_STATE) % GROUPS_PER_TILE
    q_col_group = (lax.broadcasted_iota(jnp.int32, (q_rows, q_chunk), 1) // SSM_GROUP) % GROUPS_PER_TILE
    for cc in range(SSM_L * LANES // q_chunk):
        rep = jnp.dot(qc_ref[...], rep_ref[:, cc * q_chunk:(cc + 1) * q_chunk], preferred_element_type=F32)
        q_ref[:, cc * q_chunk:(cc + 1) * q_chunk] = jnp.where(q_row_group == q_col_group, rep, 0.0).astype(BF16)

    for i in range(SSM_L):
        rows = u_ref[pl.ds(i, n_chunks, stride=SSM_L), :]
        urev_ref[:, (SSM_L - 1 - i) * LANES:(SSM_L - i) * LANES] = rows.astype(BF16)

    e_ref[...] = jnp.dot(urev_ref[...], p_ref[...], preferred_element_type=F32)

    a_re = a_ref[0:1, :]
    a_im = a_ref[1:2, :]

    def chunk_step(c, carry):
        h_re, h_im = carry
        hin_ref[pl.ds(c, 1), 0:STATE_LANES] = h_re
        hin_ref[pl.ds(c, 1), STATE_LANES:2 * STATE_LANES] = h_im
        e_re = e_ref[pl.ds(c, 1), 0:STATE_LANES]
        e_im = e_ref[pl.ds(c, 1), STATE_LANES:2 * STATE_LANES]
        return (a_re * h_re - a_im * h_im + e_re,
                a_re * h_im + a_im * h_re + e_im)

    zero = jnp.zeros((1, STATE_LANES), F32)
    lax.fori_loop(0, n_chunks, chunk_step, (zero, zero))

    carry_in = jnp.dot(hin_ref[...].astype(BF16), q_ref[...], preferred_element_type=F32)

    dskip = dskip_ref[...]
    for i in range(SSM_L):
        intra = jnp.dot(urev_ref[:, (SSM_L - 1 - i) * LANES:], d_ref[0:(i + 1) * LANES, :],
                        preferred_element_type=F32)
        u_i = u_ref[pl.ds(i, n_chunks, stride=SSM_L), :]
        y = intra + carry_in[:, i * LANES:(i + 1) * LANES] + dskip * u_i
        y_ref[pl.ds(i, n_chunks, stride=SSM_L), :] = _gelu_tanh(y)


def _group_replicator():
    rep = np.zeros((SSM_L, SSM_GROUP, SSM_L, GROUPS_PER_TILE, SSM_GROUP), np.float32)
    for i in range(SSM_L):
        for h in range(SSM_GROUP):
            rep[i, h, i, :, h] = 1.0
    return jnp.asarray(rep.reshape(SSM_L * SSM_GROUP, SSM_L * LANES), BF16)


def _ssm_mixer(u, dstack, pcompact, qcompact, a_chunk, dskip):
    seq = u.shape[0]
    n_chunks = seq // SSM_L
    return pl.pallas_call(
        _ssm_kernel,
        grid=(N_LANE_TILES,),
        in_specs=[pl.BlockSpec((seq, LANES), lambda j: (0, j)),
                  pl.BlockSpec((None, SSM_L * LANES, LANES), lambda j: (j, 0, 0)),
                  pl.BlockSpec((None, SSM_L * LANES, 2 * LANES), lambda j: (j, 0, 0)),
                  pl.BlockSpec((None, 2 * STATE_LANES, SSM_L * SSM_GROUP), lambda j: (j, 0, 0)),
                  pl.BlockSpec((SSM_L * SSM_GROUP, SSM_L * LANES), lambda j: (0, 0)),
                  pl.BlockSpec((None, 2, STATE_LANES), lambda j: (j, 0, 0)),
                  pl.BlockSpec((None, 1, LANES), lambda j: (j, 0, 0))],
        out_specs=pl.BlockSpec((seq, LANES), lambda j: (0, j)),
        out_shape=jax.ShapeDtypeStruct((seq, SSM_WIDTH), F32),
        scratch_shapes=[pltpu.VMEM((n_chunks, SSM_L * LANES), BF16),
                        pltpu.VMEM((n_chunks, 2 * STATE_LANES), F32),
                        pltpu.VMEM((n_chunks, 2 * STATE_LANES), F32),
                        pltpu.VMEM((SSM_L * LANES, 2 * STATE_LANES), BF16),
                        pltpu.VMEM((2 * STATE_LANES, SSM_L * LANES), BF16)],
        compiler_params=_cparams(1),
        name="ssm_mixer",
    )(u, dstack, pcompact, qcompact, _group_replicator(), a_chunk, dskip)


def _ssm_operators(log_step, lam_re, lam_im, b_re, b_im, c_re, c_im):
    hi = lax.Precision.HIGHEST
    step = jnp.exp(log_step.astype(F32))[:, None]
    lr = lam_re.astype(F32)
    li = lam_im.astype(F32)
    mag = jnp.exp(lr * step)
    ab_re = mag * jnp.cos(li * step)
    ab_im = mag * jnp.sin(li * step)
    den = lr * lr + li * li
    nr = ab_re - 1.0
    ni = ab_im
    z_re = (nr * lr + ni * li) / den
    z_im = (ni * lr - nr * li) / den
    br = b_re.astype(F32)
    bi = b_im.astype(F32)
    bb_re = z_re[..., None] * br - z_im[..., None] * bi
    bb_im = z_re[..., None] * bi + z_im[..., None] * br

    pw_re = [jnp.ones_like(ab_re)]
    pw_im = [jnp.zeros_like(ab_im)]
    for _ in range(SSM_L):
        r, m = pw_re[-1], pw_im[-1]
        pw_re.append(ab_re * r - ab_im * m)
        pw_im.append(ab_re * m + ab_im * r)
    pw_re = jnp.stack(pw_re)
    pw_im = jnp.stack(pw_im)

    abb_re = pw_re[:SSM_L, :, :, None] * bb_re - pw_im[:SSM_L, :, :, None] * bb_im
    abb_im = pw_re[:SSM_L, :, :, None] * bb_im + pw_im[:SSM_L, :, :, None] * bb_re
    cr = c_re.astype(F32)
    ci = c_im.astype(F32)
    kern = (jnp.einsum('gpn,tgnh->tghp', cr, abb_re, precision=hi)
            - jnp.einsum('gpn,tgnh->tghp', ci, abb_im, precision=hi))

    eye = jnp.eye(GROUPS_PER_TILE, dtype=F32)
    nt, gl = N_LANE_TILES, GROUPS_PER_TILE

    k5 = kern.reshape(SSM_L, nt, gl, SSM_GROUP, SSM_GROUP).transpose(1, 0, 2, 3, 4)
    dstack = (k5[:, :, :, :, None, :] * eye[None, None, :, None, :, None]
              ).reshape(nt, SSM_L * LANES, LANES)

    parity = jnp.asarray(np.arange(gl)[:, None] % 2 == np.arange(2)[None, :], F32)

    def p_part(abb):
        a5 = abb.reshape(SSM_L, nt, gl, SSM_STATE, SSM_GROUP).transpose(1, 0, 2, 4, 3)
        return a5[:, :, :, :, None, :] * parity[None, None, :, None, :, None]
    pcompact = jnp.stack([p_part(abb_re), p_part(abb_im)], axis=4)
    pcompact = pcompact.reshape(nt, SSM_L * LANES, 2 * LANES)

    ca_re = cr[None] * pw_re[1:, :, None, :] - ci[None] * pw_im[1:, :, None, :]
    ca_im = cr[None] * pw_im[1:, :, None, :] + ci[None] * pw_re[1:, :, None, :]

    def q_part(ca):
        return ca.reshape(SSM_L, nt, gl, SSM_GROUP, SSM_STATE).transpose(1, 2, 4, 0, 3)
    qcompact = jnp.stack([q_part(ca_re), q_part(-ca_im)], axis=1)
    qcompact = qcompact.reshape(nt, 2 * STATE_LANES, SSM_L * SSM_GROUP)

    a_chunk = jnp.stack([pw_re[SSM_L].reshape(nt, STATE_LANES),
                         pw_im[SSM_L].reshape(nt, STATE_LANES)], axis=1)
    return dstack.astype(BF16), pcompact.astype(BF16), qcompact.astype(BF16), a_chunk


def _glu_kernel(y_ref, w_ref, b_ref, o_ref):
    y = y_ref[...]
    z = jnp.dot(y.astype(BF16), w_ref[...], preferred_element_type=F32) + b_ref[...]
    o_ref[...] = (y * jax.nn.sigmoid(z)).astype(o_ref.dtype)


def _glu(y, w, b, tm):
    m, n = y.shape
    tm = min(tm, m)
    return pl.pallas_call(
        _glu_kernel,
        grid=(m // tm,),
        in_specs=[pl.BlockSpec((tm, n), lambda i: (i, 0)),
                  pl.BlockSpec((n, n), lambda i: (0, 0)),
                  pl.BlockSpec((1, n), lambda i: (0, 0))],
        out_specs=pl.BlockSpec((tm, n), lambda i: (i, 0)),
        out_shape=jax.ShapeDtypeStruct((m, n), BF16),
        compiler_params=_cparams(1),
        name="glu",
    )(y, w, b)


ATT_TQ = 512
LOG2_E = math.log2(math.e)
FAR_BUCKET = REL_BUCKETS // 2 - 1


def _attn_kernel(rel_ref, q_ref, k_ref, v_ref, bucket_ref, lq1_ref, lk1_ref, lq2_ref, lk2_ref, g_ref,
                 o_ref, m_ref, l_ref, acc_ref, bias_ref, s0_ref, s1_ref):
    h = pl.program_id(0)
    i = pl.program_id(1)
    tq = ATT_TQ

    @pl.when(i == 0)
    def _():
        far = rel_ref[FAR_BUCKET, h]

        def rows(r, carry):
            r0 = pl.multiple_of(r * 8, 8)
            bucket = bucket_ref[pl.ds(r0, 8), :]
            tile = jnp.where(bucket < 0, NEG_INF, 0.0)
            for b in range(REL_BUCKETS):
                tile = jnp.where(bucket == b, (rel_ref[b, h] - far) * LOG2_E, tile)
            bias_ref[pl.ds(r0, 8), :] = tile
            return carry

        lax.fori_loop(0, tq // 8, rows, 0)

    m_ref[...] = jnp.full_like(m_ref, NEG_INF)
    l_ref[...] = jnp.zeros_like(l_ref)
    acc_ref[...] = jnp.zeros_like(acc_ref)

    def scores(block, s_ref, bias):
        start = pl.multiple_of(block * tq, tq)
        kb = k_ref[pl.ds(start, tq), :]
        for c in range(2):
            q = q_ref[:, c * ATT_QK_DIM:(c + 1) * ATT_QK_DIM]
            kc = kb[:, c * ATT_QK_DIM:(c + 1) * ATT_QK_DIM]
            s = lax.dot_general(q, kc, (((1,), (1,)), ((), ())), preferred_element_type=F32)
            s_ref[c] = s if bias is None else s + bias

    def absorb(block, s_ref):
        start = pl.multiple_of(block * tq, tq)
        vb = v_ref[pl.ds(start, tq), :]
        for c in range(2):
            tiles = [s_ref[c, :, t * LANES:(t + 1) * LANES] for t in range(tq // LANES)]
            m_tile = functools.reduce(jnp.maximum, tiles)
            m_prev = m_ref[c]
            m_new = jnp.maximum(m_prev, jnp.max(m_tile, axis=1, keepdims=True))
            scale = jnp.exp2(m_prev - m_new)
            ps = [jnp.exp2(t - m_new) for t in tiles]
            l_ref[c] = scale * l_ref[c] + functools.reduce(jnp.add, ps)
            m_ref[c] = m_new
            pv = jnp.dot(jnp.concatenate(ps, axis=1).astype(BF16), vb, preferred_element_type=F32)
            for t in range(ATT_V_DIM // LANES):
                cols = slice(t * LANES, (t + 1) * LANES)
                acc_ref[c, :, cols] = scale * acc_ref[c, :, cols] + pv[:, cols]

    scores(i, s0_ref, bias_ref[:, tq:2 * tq])

    @pl.when(i == 0)
    def _():
        absorb(i, s0_ref)

    @pl.when(i >= 1)
    def _():
        absorb(i, s0_ref)
        scores(i - 1, s1_ref, bias_ref[:, 0:tq])
        n_far = i - 1
        n_pairs = lax.shift_right_logical(n_far, 1)

        def far_pair(p, carry):
            absorb(jnp.where(p == 0, i - 1, 2 * p - 1), s1_ref)
            scores(2 * p, s0_ref, None)
            absorb(2 * p, s0_ref)
            scores(2 * p + 1, s1_ref, None)
            return carry

        lax.fori_loop(0, n_pairs, far_pair, 0)
        in_s1 = jnp.where(n_pairs == 0, i - 1, 2 * n_pairs - 1)

        @pl.when(lax.rem(n_far, 2) == 1)
        def _():
            absorb(in_s1, s1_ref)
            scores(n_far - 1, s0_ref, None)
            absorb(n_far - 1, s0_ref)

        @pl.when(lax.rem(n_far, 2) == 0)
        def _():
            absorb(in_s1, s1_ref)

    lam = (jnp.exp(jnp.sum(lq1_ref[...] * lk1_ref[...], axis=1, keepdims=True))
           - jnp.exp(jnp.sum(lq2_ref[...] * lk2_ref[...], axis=1, keepdims=True)) + LAMBDA_INIT)
    l0 = jnp.sum(l_ref[0], axis=1, keepdims=True)
    l1 = jnp.sum(l_ref[1], axis=1, keepdims=True)
    o = acc_ref[0] / l0 - lam * (acc_ref[1] / l1)
    o = o * lax.rsqrt(jnp.mean(o * o, axis=1, keepdims=True) + LN_EPS) * g_ref[...]
    o_ref[...] = (o * (1.0 - LAMBDA_INIT)).astype(o_ref.dtype)


def _t5_bucket(rel):
    half = REL_BUCKETS // 2
    max_exact = half // 2
    ret = jnp.where(rel > 0, half, 0)
    n = jnp.abs(rel)
    nf = jnp.maximum(n, 1).astype(jnp.float32)
    large = max_exact + (jnp.log(nf / max_exact) / math.log(REL_MAX_DIST / max_exact)
                         * (half - max_exact)).astype(jnp.int32)
    large = jnp.minimum(large, half - 1)
    return ret + jnp.where(n < max_exact, n, large)


def _near_buckets():
    tq = ATT_TQ
    qpos = np.arange(tq)[:, None]
    kpos = np.arange(-tq, tq)[None, :]
    rel = jnp.asarray(kpos - qpos, jnp.int32)
    visible = jnp.asarray((kpos // CHUNK) <= (qpos // CHUNK))
    return jnp.where(visible, _t5_bucket(rel), -1).astype(jnp.int32)


def _diff_attention(qkv, rel_bias, lq1, lk1, lq2, lk2, subln_g):
    seq = qkv.shape[0]
    tq = ATT_TQ
    head_blk = 2 * ATT_QK_DIM
    vec = pl.BlockSpec((1, ATT_QK_DIM), lambda h, i: (0, 0))
    return pl.pallas_call(
        _attn_kernel,
        grid=(ATT_HEADS, seq // tq),
        in_specs=[pl.BlockSpec(memory_space=pltpu.SMEM),
                  pl.BlockSpec((tq, head_blk), lambda h, i: (i, h)),
                  pl.BlockSpec((seq, head_blk), lambda h, i: (0, ATT_HEADS + h)),
                  pl.BlockSpec((seq, ATT_V_DIM), lambda h, i: (0, 2 * ATT_HEADS + h)),
                  pl.BlockSpec((tq, 2 * tq), lambda h, i: (0, 0)),
                  vec, vec, vec, vec,
                  pl.BlockSpec((1, ATT_V_DIM), lambda h, i: (0, 0))],
        out_specs=pl.BlockSpec((tq, ATT_V_DIM), lambda h, i: (i, h)),
        out_shape=jax.ShapeDtypeStruct((seq, ATT_WIDTH), BF16),
        scratch_shapes=[pltpu.VMEM((2, tq, LANES), F32),
                        pltpu.VMEM((2, tq, LANES), F32),
                        pltpu.VMEM((2, tq, ATT_V_DIM), F32),
                        pltpu.VMEM((tq, 2 * tq), F32),
                        pltpu.VMEM((2, tq, tq), F32),
                        pltpu.VMEM((2, tq, tq), F32)],
        compiler_params=_cparams(2),
        name="diff_attention",
    )(rel_bias.astype(F32), qkv, qkv, qkv, _near_buckets(), lq1, lk1, lq2, lk2, subln_g)


MM_COL_CHUNK = 1024
LN_ROW_CHUNK = 128


def _residual_layer_norm(res_ref, acc_ref, g_ref, b_ref, out_refs):
    g = g_ref[...]
    b = b_ref[...]
    for r0 in range(0, acc_ref.shape[0], LN_ROW_CHUNK):
        rows = slice(r0, r0 + LN_ROW_CHUNK)
        r = ALPHA * res_ref[rows, :] + acc_ref[rows, :]
        mu = jnp.mean(r, axis=1, keepdims=True)
        xc = r - mu
        var = jnp.mean(xc * xc, axis=1, keepdims=True)
        h = xc * lax.rsqrt(var + LN_EPS) * g + b
        for o_ref in out_refs:
            o_ref[rows, :] = h.astype(o_ref.dtype)


def _outproj_kernel(ys_ref, ya_ref, wt_ref, wb_ref, x_ref, g_ref, b_ref, h_ref, hb_ref):
    k = pl.program_id(1)

    @pl.when(k == 0)
    def _():
        h_ref[...] = jnp.zeros_like(h_ref)

    for c in range(0, h_ref.shape[1], MM_COL_CHUNK):
        cols = slice(c, c + MM_COL_CHUNK)
        h_ref[:, cols] += (jnp.dot(ys_ref[...], wt_ref[:, cols], preferred_element_type=F32)
                           + jnp.dot(ya_ref[...], wb_ref[:, cols], preferred_element_type=F32))

    @pl.when(k == pl.num_programs(1) - 1)
    def _():
        _residual_layer_norm(x_ref, h_ref, g_ref, b_ref, (h_ref, hb_ref))


def _outproj_ln(ys, ya, w_out, x, g, b, tm, tk):
    m, half = ys.shape
    d = w_out.shape[1]
    tm = min(tm, m)
    nk = half // tk
    row = pl.BlockSpec((1, d), lambda i, k: (0, 0))
    return pl.pallas_call(
        _outproj_kernel,
        grid=(m // tm, nk),
        in_specs=[pl.BlockSpec((tm, tk), lambda i, k: (i, k)),
                  pl.BlockSpec((tm, tk), lambda i, k: (i, k)),
                  pl.BlockSpec((tk, d), lambda i, k: (k, 0)),
                  pl.BlockSpec((tk, d), lambda i, k: (k + nk, 0)),
                  pl.BlockSpec((tm, d), lambda i, k: (i, 0), pipeline_mode=pl.Buffered(1)),
                  row, row],
        out_specs=[pl.BlockSpec((tm, d), lambda i, k: (i, 0)),
                   pl.BlockSpec((tm, d), lambda i, k: (i, 0))],
        out_shape=[jax.ShapeDtypeStruct((m, d), F32), jax.ShapeDtypeStruct((m, d), BF16)],
        compiler_params=_cparams(2, VMEM_LIMIT_LARGE),
        name="outproj_ln",
    )(ys, ya, w_out, w_out, x, g, b)


FFN_TF = 256
FFN_HALO = 8


def _ffn_kernel(h_ref, wa_ref, wg_ref, cw_ref, cb_ref, wd_ref, r_ref, lg_ref, lb_ref,
                o_ref, halo_ref):
    i = pl.program_id(0)
    f = pl.program_id(1)
    tm = h_ref.shape[0]

    h = h_ref[...]
    value = jnp.dot(h, wa_ref[...], preferred_element_type=F32)
    gate = jnp.dot(h, wg_ref[...], preferred_element_type=F32)

    halo = jnp.where(i == 0, 0.0, halo_ref[f])
    halo_ref[f] = gate[tm - FFN_HALO:tm, :]
    row = lax.broadcasted_iota(jnp.int32, gate.shape, 0)
    prev1 = jnp.where(row == 0, halo[FFN_HALO - 1:FFN_HALO, :], pltpu.roll(gate, 1, 0))
    prev2 = jnp.where(row == 0, halo[FFN_HALO - 2:FFN_HALO - 1, :],
                      jnp.where(row == 1, halo[FFN_HALO - 1:FFN_HALO, :], pltpu.roll(gate, 2, 0)))
    gc = cb_ref[...] + prev2 * cw_ref[0:1, :] + prev1 * cw_ref[1:2, :] + gate * cw_ref[2:3, :]
    act = ((gc * jax.nn.sigmoid(gc)) * value).astype(BF16)

    @pl.when(f == 0)
    def _():
        o_ref[...] = jnp.zeros_like(o_ref)

    for c in range(0, o_ref.shape[1], MM_COL_CHUNK):
        cols = slice(c, c + MM_COL_CHUNK)
        o_ref[:, cols] += jnp.dot(act, wd_ref[:, cols], preferred_element_type=F32)

    @pl.when(f == pl.num_programs(1) - 1)
    def _():
        _residual_layer_norm(r_ref, o_ref, lg_ref, lb_ref, (o_ref,))


def _ffn_ln(h_bf, w_up, conv_w, conv_b, w_down, h1, g, b, tm):
    m, d = h_bf.shape
    tm = min(tm, m)
    nf = D_FF // FFN_TF
    row = pl.BlockSpec((1, d), lambda i, f: (0, 0))
    return pl.pallas_call(
        _ffn_kernel,
        grid=(m // tm, nf),
        in_specs=[pl.BlockSpec((tm, d), lambda i, f: (i, 0)),
                  pl.BlockSpec((d, FFN_TF), lambda i, f: (0, f)),
                  pl.BlockSpec((d, FFN_TF), lambda i, f: (0, f + nf)),
                  pl.BlockSpec((3, FFN_TF), lambda i, f: (0, f)),
                  pl.BlockSpec((1, FFN_TF), lambda i, f: (0, f)),
                  pl.BlockSpec((FFN_TF, d), lambda i, f: (f, 0)),
                  pl.BlockSpec((tm, d), lambda i, f: (i, 0), pipeline_mode=pl.Buffered(1)),
                  row, row],
        out_specs=pl.BlockSpec((tm, d), lambda i, f: (i, 0)),
        out_shape=jax.ShapeDtypeStruct((m, d), F32),
        scratch_shapes=[pltpu.VMEM((nf, FFN_HALO, FFN_TF), F32)],
        compiler_params=_cparams(2),
        name="ffn_ln",
    )(h_bf, w_up, w_up, conv_w, conv_b, w_down, h1, g, b)


def kernel(x, w_in, ssm_log_step, ssm_lambda_re, ssm_lambda_im, ssm_b_re, ssm_b_im, ssm_c_re, ssm_c_im, ssm_d, ssm_w_glu, ssm_b_glu, att_lambda_q1, att_lambda_k1, att_lambda_q2, att_lambda_k2, att_subln_g, rel_bias, w_out, ln1_g, ln1_b, ffn_w_up, ffn_conv_w, ffn_conv_b, ffn_w_down, ln2_g, ln2_b):
    bsz, seq, _ = x.shape
    assert bsz == 1 and DEPTH == 1
    l = 0
    xs = x[0]

    qk_scale = ATT_QK_DIM ** -0.5 * LOG2_E
    col_scale = np.ones((1, w_in.shape[2]), np.float32)
    col_scale[:, SSM_WIDTH:SSM_WIDTH + QK_WIDTH] = qk_scale
    w_in_bf = (w_in[l] * jnp.asarray(col_scale)).astype(BF16)
    x_bf = xs.astype(BF16)

    u = _matmul(x_bf, w_in_bf, F32, 1024, 1024, 1024, 0, SSM_WIDTH)
    qkv = _matmul(x_bf, w_in_bf, BF16, 1024, 1024, 1024, SSM_WIDTH, w_in.shape[2] - SSM_WIDTH)

    dstack, pcompact, qcompact, a_chunk = _ssm_operators(
        ssm_log_step[l], ssm_lambda_re[l], ssm_lambda_im[l], ssm_b_re[l], ssm_b_im[l],
        ssm_c_re[l], ssm_c_im[l])
    dskip = ssm_d[l].astype(F32).reshape(N_LANE_TILES, 1, LANES)
    y = _ssm_mixer(u, dstack, pcompact, qcompact, a_chunk, dskip)
    y_ssm = _glu(y, ssm_w_glu[l].astype(BF16), ssm_b_glu[l].astype(F32).reshape(1, SSM_WIDTH), 512)

    vec = lambda a: a.astype(F32).reshape(1, ATT_QK_DIM)
    y_att = _diff_attention(qkv, rel_bias,
                            vec(att_lambda_q1[l]), vec(att_lambda_k1[l]),
                            vec(att_lambda_q2[l]), vec(att_lambda_k2[l]),
                            att_subln_g[l].astype(F32).reshape(1, ATT_V_DIM))

    h1, h1_bf = _outproj_ln(y_ssm, y_att, w_out[l].astype(BF16), xs,
                            ln1_g[l].reshape(1, D_MODEL), ln1_b[l].reshape(1, D_MODEL), 512, 512)

    out = _ffn_ln(h1_bf, ffn_w_up[l].astype(BF16), ffn_conv_w[l].astype(F32),
                  ffn_conv_b[l].astype(F32).reshape(1, D_FF), ffn_w_down[l].astype(BF16), h1,
                  ln2_g[l].reshape(1, D_MODEL), ln2_b[l].reshape(1, D_MODEL), 512)
    return out[None]
```

```python
import functools
import math

import numpy as np
import jax
import jax.numpy as jnp
from jax import lax
from jax.experimental import pallas as pl
from jax.experimental.pallas import tpu as pltpu

F32 = jnp.float32
BF16 = jnp.bfloat16

D_MODEL = 4096
CHUNK = 64
SSM_WIDTH = 2048
SSM_GROUP = 16
SSM_GROUPS = SSM_WIDTH // SSM_GROUP
SSM_STATE = 64
ATT_QK_DIM = 128
ATT_V_DIM = 256
ATT_WIDTH = 2048
ATT_HEADS = 8
QK_WIDTH = 2048
D_FF = 11008
REL_BUCKETS = 32
REL_MAX_DIST = 128
DEPTH = 1
ALPHA = (2 * DEPTH) ** 0.25
LN_EPS = 1e-5
NEG_INF = -1e30
LAMBDA_INIT = 0.8 - 0.6 * math.exp(-0.3 * 0)

LANES = 128
SSM_L = 16
GROUPS_PER_TILE = LANES // SSM_GROUP
N_LANE_TILES = SSM_WIDTH // LANES
STATE_LANES = GROUPS_PER_TILE * SSM_STATE
VMEM_LIMIT = 56 * 1024 * 1024
VMEM_LIMIT_LARGE = 62 * 1024 * 1024


def _cparams(n_axes, vmem=VMEM_LIMIT):
    return pltpu.CompilerParams(dimension_semantics=("arbitrary",) * n_axes,
                                vmem_limit_bytes=vmem)


def _matmul_kernel(x_ref, w_ref, o_ref, acc_ref):
    k = pl.program_id(2)

    @pl.when(k == 0)
    def _():
        acc_ref[...] = jnp.zeros_like(acc_ref)

    acc_ref[...] += jnp.dot(x_ref[...], w_ref[...], preferred_element_type=F32)

    @pl.when(k == pl.num_programs(2) - 1)
    def _():
        o_ref[...] = acc_ref[...].astype(o_ref.dtype)


def _matmul(x, w, out_dtype, tm, tn, tk, col_start=0, n_cols=None):
    m, kdim = x.shape
    n = w.shape[1] - col_start if n_cols is None else n_cols
    tm, tn, tk = min(tm, m), min(tn, n), min(tk, kdim)
    assert m % tm == 0 and n % tn == 0 and kdim % tk == 0 and col_start % tn == 0
    j0 = col_start // tn
    return pl.pallas_call(
        _matmul_kernel,
        grid=(m // tm, n // tn, kdim // tk),
        in_specs=[pl.BlockSpec((tm, tk), lambda i, j, k: (i, k)),
                  pl.BlockSpec((tk, tn), lambda i, j, k: (k, j + j0))],
        out_specs=pl.BlockSpec((tm, tn), lambda i, j, k: (i, j)),
        out_shape=jax.ShapeDtypeStruct((m, n), out_dtype),
        scratch_shapes=[pltpu.VMEM((tm, tn), F32)],
        compiler_params=_cparams(3),
        name="matmul",
    )(x, w)


def _gelu_tanh(x):
    c = math.sqrt(2.0 / math.pi)
    return 0.5 * x * (1.0 + jnp.tanh(c * (x + 0.044715 * (x * x * x))))


def _ssm_kernel(u_ref, lam_ref, b_ref, c_ref, dskip_ref, y_ref,
                urev_ref, e_ref, hin_ref, p_ref, qt_ref, d_ref):
    n_chunks = u_ref.shape[0] // SSM_L
    contract_last = (((1,), (1,)), ((), ()))

    lr = lam_ref[0:1, :]
    li = lam_ref[1:2, :]
    step = lam_ref[2:3, :]
    mag = jnp.exp(lr * step)
    a_re = mag * jnp.cos(li * step)
    a_im = mag * jnp.sin(li * step)
    den = lr * lr + li * li
    z_re = ((a_re - 1.0) * lr + a_im * li) / den
    z_im = (a_im * lr - (a_re - 1.0) * li) / den

    b_re, b_im = b_ref[0], b_ref[1]
    c_re, c_im = c_ref[0], c_ref[1]
    c0 = jnp.concatenate([c_re, -c_im], axis=1).astype(BF16)
    w_re, w_im = z_re, z_im
    pw_re, pw_im = a_re, a_im
    for s in range(SSM_L):
        rows = slice(s * LANES, (s + 1) * LANES)
        p_ref[rows, 0:STATE_LANES] = (w_re * b_re - w_im * b_im).astype(BF16)
        p_ref[rows, STATE_LANES:2 * STATE_LANES] = (w_re * b_im + w_im * b_re).astype(BF16)
        qt_ref[rows, 0:STATE_LANES] = (c_re * pw_re - c_im * pw_im).astype(BF16)
        qt_ref[rows, STATE_LANES:2 * STATE_LANES] = (-(c_re * pw_im + c_im * pw_re)).astype(BF16)
        w_re, w_im = w_re * a_re - w_im * a_im, w_re * a_im + w_im * a_re
        if s + 1 < SSM_L:
            pw_re, pw_im = pw_re * a_re - pw_im * a_im, pw_re * a_im + pw_im * a_re
    d_ref[...] = lax.dot_general(p_ref[...], c0, contract_last, preferred_element_type=F32).astype(BF16)

    for i in range(SSM_L):
        rows = u_ref[pl.ds(i, n_chunks, stride=SSM_L), :]
        urev_ref[:, (SSM_L - 1 - i) * LANES:(SSM_L - i) * LANES] = rows.astype(BF16)

    e_ref[...] = jnp.dot(urev_ref[...], p_ref[...], preferred_element_type=F32)

    def chunk_step(c, carry):
        h_re, h_im = carry
        hin_ref[pl.ds(c, 1), 0:STATE_LANES] = h_re
        hin_ref[pl.ds(c, 1), STATE_LANES:2 * STATE_LANES] = h_im
        e_re = e_ref[pl.ds(c, 1), 0:STATE_LANES]
        e_im = e_ref[pl.ds(c, 1), STATE_LANES:2 * STATE_LANES]
        return (pw_re * h_re - pw_im * h_im + e_re,
                pw_re * h_im + pw_im * h_re + e_im)

    zero = jnp.zeros((1, STATE_LANES), F32)
    lax.fori_loop(0, n_chunks, chunk_step, (zero, zero))

    carry_in = lax.dot_general(hin_ref[...].astype(BF16), qt_ref[...], contract_last,
                               preferred_element_type=F32)

    dskip = dskip_ref[...]
    for i in range(SSM_L):
        intra = jnp.dot(urev_ref[:, (SSM_L - 1 - i) * LANES:], d_ref[0:(i + 1) * LANES, :],
                        preferred_element_type=F32)
        u_i = u_ref[pl.ds(i, n_chunks, stride=SSM_L), :]
        y = intra + carry_in[:, i * LANES:(i + 1) * LANES] + dskip * u_i
        y_ref[pl.ds(i, n_chunks, stride=SSM_L), :] = _gelu_tanh(y)


def _ssm_mixer(u, lam, b_blocks, c_blocks, dskip):
    seq = u.shape[0]
    n_chunks = seq // SSM_L
    return pl.pallas_call(
        _ssm_kernel,
        grid=(N_LANE_TILES,),
        in_specs=[pl.BlockSpec((seq, LANES), lambda j: (0, j)),
                  pl.BlockSpec((None, 3, STATE_LANES), lambda j: (j, 0, 0)),
                  pl.BlockSpec((None, 2, LANES, STATE_LANES), lambda j: (j, 0, 0, 0)),
                  pl.BlockSpec((None, 2, LANES, STATE_LANES), lambda j: (j, 0, 0, 0)),
                  pl.BlockSpec((None, 1, LANES), lambda j: (j, 0, 0))],
        out_specs=pl.BlockSpec((seq, LANES), lambda j: (0, j)),
        out_shape=jax.ShapeDtypeStruct((seq, SSM_WIDTH), F32),
        scratch_shapes=[pltpu.VMEM((n_chunks, SSM_L * LANES), BF16),
                        pltpu.VMEM((n_chunks, 2 * STATE_LANES), F32),
                        pltpu.VMEM((n_chunks, 2 * STATE_LANES), F32),
                        pltpu.VMEM((SSM_L * LANES, 2 * STATE_LANES), BF16),
                        pltpu.VMEM((SSM_L * LANES, 2 * STATE_LANES), BF16),
                        pltpu.VMEM((SSM_L * LANES, LANES), BF16)],
        compiler_params=_cparams(1),
        name="ssm_mixer",
    )(u, lam, b_blocks, c_blocks, dskip)


def _ssm_parameter_layout(log_step, lam_re, lam_im, b_re, b_im, c_re, c_im):
    nt, gl = N_LANE_TILES, GROUPS_PER_TILE
    step = jnp.broadcast_to(jnp.exp(log_step.astype(F32))[:, None], lam_re.shape)
    lam = jnp.stack([lam_re.astype(F32).reshape(nt, STATE_LANES),
                     lam_im.astype(F32).reshape(nt, STATE_LANES),
                     step.reshape(nt, STATE_LANES)], axis=1)
    eye = jnp.eye(gl, dtype=F32)

    def blocks(x):
        x5 = x.astype(F32).reshape(nt, gl, SSM_GROUP, SSM_STATE)
        return (x5[:, :, :, None, :] * eye[None, :, None, :, None]).reshape(nt, LANES, STATE_LANES)

    b_blocks = jnp.stack([blocks(b_re.transpose(0, 2, 1)), blocks(b_im.transpose(0, 2, 1))], axis=1)
    c_blocks = jnp.stack([blocks(c_re), blocks(c_im)], axis=1)
    return lam, b_blocks, c_blocks


def _glu_kernel(y_ref, w_ref, b_ref, o_ref):
    y = y_ref[...]
    z = jnp.dot(y.astype(BF16), w_ref[...], preferred_element_type=F32) + b_ref[...]
    o_ref[...] = (y * jax.nn.sigmoid(z)).astype(o_ref.dtype)


def _glu(y, w, b, tm):
    m, n = y.shape
    tm = min(tm, m)
    return pl.pallas_call(
        _glu_kernel,
        grid=(m // tm,),
        in_specs=[pl.BlockSpec((tm, n), lambda i: (i, 0)),
                  pl.BlockSpec((n, n), lambda i: (0, 0)),
                  pl.BlockSpec((1, n), lambda i: (0, 0))],
        out_specs=pl.BlockSpec((tm, n), lambda i: (i, 0)),
        out_shape=jax.ShapeDtypeStruct((m, n), BF16),
        compiler_params=_cparams(1),
        name="glu",
    )(y, w, b)


ATT_TQ = 512
LOG2_E = math.log2(math.e)
FAR_BUCKET = REL_BUCKETS // 2 - 1


def _attn_kernel(rel_ref, q_ref, k_ref, v_ref, bucket_ref, lq1_ref, lk1_ref, lq2_ref, lk2_ref, g_ref,
                 o_ref, m_ref, l_ref, acc_ref, bias_ref, s0_ref, s1_ref):
    h = pl.program_id(0)
    i = pl.program_id(1)
    tq = ATT_TQ

    @pl.when(i == 0)
    def _():
        far = rel_ref[FAR_BUCKET, h]

        def rows(r, carry):
            r0 = pl.multiple_of(r * 8, 8)
            bucket = bucket_ref[pl.ds(r0, 8), :]
            tile = jnp.where(bucket < 0, NEG_INF, 0.0)
            for b in range(REL_BUCKETS):
                tile = jnp.where(bucket == b, (rel_ref[b, h] - far) * LOG2_E, tile)
            bias_ref[pl.ds(r0, 8), :] = tile
            return carry

        lax.fori_loop(0, tq // 8, rows, 0)

    m_ref[...] = jnp.full_like(m_ref, NEG_INF)
    l_ref[...] = jnp.zeros_like(l_ref)
    acc_ref[...] = jnp.zeros_like(acc_ref)

    def scores(block, s_ref, bias):
        start = pl.multiple_of(block * tq, tq)
        kb = k_ref[pl.ds(start, tq), :]
        for c in range(2):
            q = q_ref[:, c * ATT_QK_DIM:(c + 1) * ATT_QK_DIM]
            kc = kb[:, c * ATT_QK_DIM:(c + 1) * ATT_QK_DIM]
            s = lax.dot_general(q, kc, (((1,), (1,)), ((), ())), preferred_element_type=F32)
            s_ref[c] = s if bias is None else s + bias

    def absorb(block, s_ref):
        start = pl.multiple_of(block * tq, tq)
        vb = v_ref[pl.ds(start, tq), :]
        for c in range(2):
            tiles = [s_ref[c, :, t * LANES:(t + 1) * LANES] for t in range(tq // LANES)]
            m_tile = functools.reduce(jnp.maximum, tiles)
            m_prev = m_ref[c]
            m_new = jnp.maximum(m_prev, jnp.max(m_tile, axis=1, keepdims=True))
            scale = jnp.exp2(m_prev - m_new)
            ps = [jnp.exp2(t - m_new) for t in tiles]
            l_ref[c] = scale * l_ref[c] + functools.reduce(jnp.add, ps)
            m_ref[c] = m_new
            pv = jnp.dot(jnp.concatenate(ps, axis=1).astype(BF16), vb, preferred_element_type=F32)
            for t in range(ATT_V_DIM // LANES):
                cols = slice(t * LANES, (t + 1) * LANES)
                acc_ref[c, :, cols] = scale * acc_ref[c, :, cols] + pv[:, cols]

    scores(i, s0_ref, bias_ref[:, tq:2 * tq])

    @pl.when(i == 0)
    def _():
        absorb(i, s0_ref)

    @pl.when(i >= 1)
    def _():
        absorb(i, s0_ref)
        scores(i - 1, s1_ref, bias_ref[:, 0:tq])
        n_far = i - 1
        n_pairs = lax.shift_right_logical(n_far, 1)

        def far_pair(p, carry):
            absorb(jnp.where(p == 0, i - 1, 2 * p - 1), s1_ref)
            scores(2 * p, s0_ref, None)
            absorb(2 * p, s0_ref)
            scores(2 * p + 1, s1_ref, None)
            return carry

        lax.fori_loop(0, n_pairs, far_pair, 0)
        in_s1 = jnp.where(n_pairs == 0, i - 1, 2 * n_pairs - 1)

        @pl.when(lax.rem(n_far, 2) == 1)
        def _():
            absorb(in_s1, s1_ref)
            scores(n_far - 1, s0_ref, None)
            absorb(n_far - 1, s0_ref)

        @pl.when(lax.rem(n_far, 2) == 0)
        def _():
            absorb(in_s1, s1_ref)

    lam = (jnp.exp(jnp.sum(lq1_ref[...] * lk1_ref[...], axis=1, keepdims=True))
           - jnp.exp(jnp.sum(lq2_ref[...] * lk2_ref[...], axis=1, keepdims=True)) + LAMBDA_INIT)
    l0 = jnp.sum(l_ref[0], axis=1, keepdims=True)
    l1 = jnp.sum(l_ref[1], axis=1, keepdims=True)
    o = acc_ref[0] / l0 - lam * (acc_ref[1] / l1)
    o = o * lax.rsqrt(jnp.mean(o * o, axis=1, keepdims=True) + LN_EPS) * g_ref[...]
    o_ref[...] = (o * (1.0 - LAMBDA_INIT)).astype(o_ref.dtype)


def _t5_bucket(rel):
    half = REL_BUCKETS // 2
    max_exact = half // 2
    ret = jnp.where(rel > 0, half, 0)
    n = jnp.abs(rel)
    nf = jnp.maximum(n, 1).astype(jnp.float32)
    large = max_exact + (jnp.log(nf / max_exact) / math.log(REL_MAX_DIST / max_exact)
                         * (half - max_exact)).astype(jnp.int32)
    large = jnp.minimum(large, half - 1)
    return ret + jnp.where(n < max_exact, n, large)


def _near_buckets():
    tq = ATT_TQ
    qpos = np.arange(tq)[:, None]
    kpos = np.arange(-tq, tq)[None, :]
    rel = jnp.asarray(kpos - qpos, jnp.int32)
    visible = jnp.asarray((kpos // CHUNK) <= (qpos // CHUNK))
    return jnp.where(visible, _t5_bucket(rel), -1).astype(jnp.int32)


def _diff_attention(qkv, rel_bias, lq1, lk1, lq2, lk2, subln_g):
    seq = qkv.shape[0]
    tq = ATT_TQ
    head_blk = 2 * ATT_QK_DIM
    vec = pl.BlockSpec((1, ATT_QK_DIM), lambda h, i: (0, 0))
    return pl.pallas_call(
        _attn_kernel,
        grid=(ATT_HEADS, seq // tq),
        in_specs=[pl.BlockSpec(memory_space=pltpu.SMEM),
                  pl.BlockSpec((tq, head_blk), lambda h, i: (i, h)),
                  pl.BlockSpec((seq, head_blk), lambda h, i: (0, ATT_HEADS + h)),
                  pl.BlockSpec((seq, ATT_V_DIM), lambda h, i: (0, 2 * ATT_HEADS + h)),
                  pl.BlockSpec((tq, 2 * tq), lambda h, i: (0, 0)),
                  vec, vec, vec, vec,
                  pl.BlockSpec((1, ATT_V_DIM), lambda h, i: (0, 0))],
        out_specs=pl.BlockSpec((tq, ATT_V_DIM), lambda h, i: (i, h)),
        out_shape=jax.ShapeDtypeStruct((seq, ATT_WIDTH), BF16),
        scratch_shapes=[pltpu.VMEM((2, tq, LANES), F32),
                        pltpu.VMEM((2, tq, LANES), F32),
                        pltpu.VMEM((2, tq, ATT_V_DIM), F32),
                        pltpu.VMEM((tq, 2 * tq), F32),
                        pltpu.VMEM((2, tq, tq), F32),
                        pltpu.VMEM((2, tq, tq), F32)],
        compiler_params=_cparams(2),
        name="diff_attention",
    )(rel_bias.astype(F32), qkv, qkv, qkv, _near_buckets(), lq1, lk1, lq2, lk2, subln_g)


MM_COL_CHUNK = 1024
LN_ROW_CHUNK = 128


def _residual_layer_norm(res_ref, acc_ref, g_ref, b_ref, out_refs):
    g = g_ref[...]
    b = b_ref[...]
    for r0 in range(0, acc_ref.shape[0], LN_ROW_CHUNK):
        rows = slice(r0, r0 + LN_ROW_CHUNK)
        r = ALPHA * res_ref[rows, :] + acc_ref[rows, :]
        mu = jnp.mean(r, axis=1, keepdims=True)
        xc = r - mu
        var = jnp.mean(xc * xc, axis=1, keepdims=True)
        h = xc * lax.rsqrt(var + LN_EPS) * g + b
        for o_ref in out_refs:
            o_ref[rows, :] = h.astype(o_ref.dtype)


def _outproj_kernel(ys_ref, ya_ref, wt_ref, wb_ref, x_ref, g_ref, b_ref, h_ref, hb_ref):
    k = pl.program_id(1)

    @pl.when(k == 0)
    def _():
        h_ref[...] = jnp.zeros_like(h_ref)

    for c in range(0, h_ref.shape[1], MM_COL_CHUNK):
        cols = slice(c, c + MM_COL_CHUNK)
        h_ref[:, cols] += (jnp.dot(ys_ref[...], wt_ref[:, cols], preferred_element_type=F32)
                           + jnp.dot(ya_ref[...], wb_ref[:, cols], preferred_element_type=F32))

    @pl.when(k == pl.num_programs(1) - 1)
    def _():
        _residual_layer_norm(x_ref, h_ref, g_ref, b_ref, (h_ref, hb_ref))


def _outproj_ln(ys, ya, w_out, x, g, b, tm, tk):
    m, half = ys.shape
    d = w_out.shape[1]
    tm = min(tm, m)
    nk = half // tk
    row = pl.BlockSpec((1, d), lambda i, k: (0, 0))
    return pl.pallas_call(
        _outproj_kernel,
        grid=(m // tm, nk),
        in_specs=[pl.BlockSpec((tm, tk), lambda i, k: (i, k)),
                  pl.BlockSpec((tm, tk), lambda i, k: (i, k)),
                  pl.BlockSpec((tk, d), lambda i, k: (k, 0)),
                  pl.BlockSpec((tk, d), lambda i, k: (k + nk, 0)),
                  pl.BlockSpec((tm, d), lambda i, k: (i, 0), pipeline_mode=pl.Buffered(1)),
                  row, row],
        out_specs=[pl.BlockSpec((tm, d), lambda i, k: (i, 0)),
                   pl.BlockSpec((tm, d), lambda i, k: (i, 0))],
        out_shape=[jax.ShapeDtypeStruct((m, d), F32), jax.ShapeDtypeStruct((m, d), BF16)],
        compiler_params=_cparams(2, VMEM_LIMIT_LARGE),
        name="outproj_ln",
    )(ys, ya, w_out, w_out, x, g, b)


FFN_TF = 256
FFN_HALO = 8
FFN_ROW_SPLIT = 2


def _ffn_kernel(h_ref, wa_ref, wg_ref, cw_ref, cb_ref, wd_ref, r_ref, lg_ref, lb_ref,
                o_ref, halo_ref):
    i = pl.program_id(0)
    f = pl.program_id(1)
    tm = h_ref.shape[0]

    @pl.when(f == 0)
    def _():
        o_ref[...] = jnp.zeros_like(o_ref)

    halo = jnp.where(i == 0, 0.0, halo_ref[f])
    sub = tm // FFN_ROW_SPLIT
    row = lax.broadcasted_iota(jnp.int32, (sub, FFN_TF), 0)
    for r0 in range(0, tm, sub):
        h = h_ref[r0:r0 + sub, :]
        value = jnp.dot(h, wa_ref[...], preferred_element_type=F32)
        gate = jnp.dot(h, wg_ref[...], preferred_element_type=F32)
        prev1 = jnp.where(row == 0, halo[FFN_HALO - 1:FFN_HALO, :], pltpu.roll(gate, 1, 0))
        prev2 = jnp.where(row == 0, halo[FFN_HALO - 2:FFN_HALO - 1, :],
                          jnp.where(row == 1, halo[FFN_HALO - 1:FFN_HALO, :], pltpu.roll(gate, 2, 0)))
        halo = gate[sub - FFN_HALO:sub, :]
        gc = cb_ref[...] + prev2 * cw_ref[0:1, :] + prev1 * cw_ref[1:2, :] + gate * cw_ref[2:3, :]
        act = ((gc * jax.nn.sigmoid(gc)) * value).astype(BF16)
        for c in range(0, o_ref.shape[1], MM_COL_CHUNK):
            cols = slice(c, c + MM_COL_CHUNK)
            o_ref[r0:r0 + sub, cols] += jnp.dot(act, wd_ref[:, cols], preferred_element_type=F32)
    halo_ref[f] = halo

    @pl.when(f == pl.num_programs(1) - 1)
    def _():
        _residual_layer_norm(r_ref, o_ref, lg_ref, lb_ref, (o_ref,))


def _ffn_ln(h_bf, w_up_tiles, conv_w, conv_b, w_down, h1, g, b, tm):
    m, d = h_bf.shape
    tm = min(tm, m)
    nf = D_FF // FFN_TF
    row = pl.BlockSpec((1, d), lambda i, f: (0, 0))
    return pl.pallas_call(
        _ffn_kernel,
        grid=(m // tm, nf),
        in_specs=[pl.BlockSpec((tm, d), lambda i, f: (i, 0)),
                  pl.BlockSpec((None, d, FFN_TF), lambda i, f: (f, 0, 0)),
                  pl.BlockSpec((None, d, FFN_TF), lambda i, f: (f + nf, 0, 0)),
                  pl.BlockSpec((3, FFN_TF), lambda i, f: (0, f)),
                  pl.BlockSpec((1, FFN_TF), lambda i, f: (0, f)),
                  pl.BlockSpec((FFN_TF, d), lambda i, f: (f, 0)),
                  pl.BlockSpec((tm, d), lambda i, f: (i, 0), pipeline_mode=pl.Buffered(1)),
                  row, row],
        out_specs=pl.BlockSpec((tm, d), lambda i, f: (i, 0)),
        out_shape=jax.ShapeDtypeStruct((m, d), F32),
        scratch_shapes=[pltpu.VMEM((nf, FFN_HALO, FFN_TF), F32)],
        compiler_params=_cparams(2),
        name="ffn_ln",
    )(h_bf, w_up_tiles, w_up_tiles, conv_w, conv_b, w_down, h1, g, b)


def kernel(x, w_in, ssm_log_step, ssm_lambda_re, ssm_lambda_im, ssm_b_re, ssm_b_im, ssm_c_re, ssm_c_im, ssm_d, ssm_w_glu, ssm_b_glu, att_lambda_q1, att_lambda_k1, att_lambda_q2, att_lambda_k2, att_subln_g, rel_bias, w_out, ln1_g, ln1_b, ffn_w_up, ffn_conv_w, ffn_conv_b, ffn_w_down, ln2_g, ln2_b):
    bsz, seq, _ = x.shape
    assert bsz == 1 and DEPTH == 1
    l = 0
    xs = x[0]

    qk_scale = ATT_QK_DIM ** -0.5 * LOG2_E
    col_scale = np.ones((1, w_in.shape[2]), np.float32)
    col_scale[:, SSM_WIDTH:SSM_WIDTH + QK_WIDTH] = qk_scale
    w_in_bf = (w_in[l] * jnp.asarray(col_scale)).astype(BF16)
    x_bf = xs.astype(BF16)

    u = _matmul(x_bf, w_in_bf, F32, 1024, 1024, 1024, 0, SSM_WIDTH)
    qkv = _matmul(x_bf, w_in_bf, BF16, 1024, 1024, 1024, SSM_WIDTH, w_in.shape[2] - SSM_WIDTH)

    lam, b_blocks, c_blocks = _ssm_parameter_layout(
        ssm_log_step[l], ssm_lambda_re[l], ssm_lambda_im[l], ssm_b_re[l], ssm_b_im[l],
        ssm_c_re[l], ssm_c_im[l])
    dskip = ssm_d[l].astype(F32).reshape(N_LANE_TILES, 1, LANES)
    y = _ssm_mixer(u, lam, b_blocks, c_blocks, dskip)
    y_ssm = _glu(y, ssm_w_glu[l].astype(BF16), ssm_b_glu[l].astype(F32).reshape(1, SSM_WIDTH), 512)

    vec = lambda a: a.astype(F32).reshape(1, ATT_QK_DIM)
    y_att = _diff_attention(qkv, rel_bias,
                            vec(att_lambda_q1[l]), vec(att_lambda_k1[l]),
                            vec(att_lambda_q2[l]), vec(att_lambda_k2[l]),
                            att_subln_g[l].astype(F32).reshape(1, ATT_V_DIM))

    h1, h1_bf = _outproj_ln(y_ssm, y_att, w_out[l].astype(BF16), xs,
                            ln1_g[l].reshape(1, D_MODEL), ln1_b[l].reshape(1, D_MODEL), 512, 512)

    w_up_tiles = ffn_w_up[l].reshape(D_MODEL, 2 * D_FF // FFN_TF, FFN_TF).transpose(1, 0, 2).astype(BF16)
    out = _ffn_ln(h1_bf, w_up_tiles, ffn_conv_w[l].astype(F32),
                  ffn_conv_b[l].astype(F32).reshape(1, D_FF), ffn_w_down[l].astype(BF16), h1,
                  ln2_g[l].reshape(1, D_MODEL), ln2_b[l].reshape(1, D_MODEL), 512)
    return out[None]
```

```python
import functools
import math

import numpy as np
import jax
import jax.numpy as jnp
from jax import lax
from jax.experimental import pallas as pl
from jax.experimental.pallas import tpu as pltpu

F32 = jnp.float32
BF16 = jnp.bfloat16

D_MODEL = 4096
CHUNK = 64
SSM_WIDTH = 2048
SSM_GROUP = 16
SSM_GROUPS = SSM_WIDTH // SSM_GROUP
SSM_STATE = 64
ATT_QK_DIM = 128
ATT_V_DIM = 256
ATT_WIDTH = 2048
ATT_HEADS = 8
QK_WIDTH = 2048
D_FF = 11008
REL_BUCKETS = 32
REL_MAX_DIST = 128
DEPTH = 1
ALPHA = (2 * DEPTH) ** 0.25
LN_EPS = 1e-5
NEG_INF = -1e30
LAMBDA_INIT = 0.8 - 0.6 * math.exp(-0.3 * 0)

LANES = 128
SSM_L = 16
GROUPS_PER_TILE = LANES // SSM_GROUP
N_LANE_TILES = SSM_WIDTH // LANES
STATE_LANES = GROUPS_PER_TILE * SSM_STATE
VMEM_LIMIT = 56 * 1024 * 1024
VMEM_LIMIT_LARGE = 62 * 1024 * 1024


def _cparams(n_axes, vmem=VMEM_LIMIT):
    return pltpu.CompilerParams(dimension_semantics=("arbitrary",) * n_axes,
                                vmem_limit_bytes=vmem)


def _matmul_kernel(x_ref, w_ref, o_ref, acc_ref):
    k = pl.program_id(2)

    @pl.when(k == 0)
    def _():
        acc_ref[...] = jnp.zeros_like(acc_ref)

    acc_ref[...] += jnp.dot(x_ref[...], w_ref[...], preferred_element_type=F32)

    @pl.when(k == pl.num_programs(2) - 1)
    def _():
        o_ref[...] = acc_ref[...].astype(o_ref.dtype)


def _matmul(x, w, out_dtype, tm, tn, tk, col_start=0, n_cols=None):
    m, kdim = x.shape
    n = w.shape[1] - col_start if n_cols is None else n_cols
    tm, tn, tk = min(tm, m), min(tn, n), min(tk, kdim)
    assert m % tm == 0 and n % tn == 0 and kdim % tk == 0 and col_start % tn == 0
    j0 = col_start // tn
    return pl.pallas_call(
        _matmul_kernel,
        grid=(m // tm, n // tn, kdim // tk),
        in_specs=[pl.BlockSpec((tm, tk), lambda i, j, k: (i, k)),
                  pl.BlockSpec((tk, tn), lambda i, j, k: (k, j + j0))],
        out_specs=pl.BlockSpec((tm, tn), lambda i, j, k: (i, j)),
        out_shape=jax.ShapeDtypeStruct((m, n), out_dtype),
        scratch_shapes=[pltpu.VMEM((tm, tn), F32)],
        compiler_params=_cparams(3),
        name="matmul",
    )(x, w)


def _gelu_tanh(x):
    c = math.sqrt(2.0 / math.pi)
    return 0.5 * x * (1.0 + jnp.tanh(c * (x + 0.044715 * (x * x * x))))


def _ssm_kernel(u_ref, lam_ref, b_ref, c_ref, dskip_ref, y_ref,
                urev_ref, e_ref, hin_ref, p_ref, qt_ref, d_ref):
    n_chunks = u_ref.shape[0] // SSM_L
    contract_last = (((1,), (1,)), ((), ()))

    lr = lam_ref[0:1, :]
    li = lam_ref[1:2, :]
    step = lam_ref[2:3, :]
    mag = jnp.exp(lr * step)
    a_re = mag * jnp.cos(li * step)
    a_im = mag * jnp.sin(li * step)
    den = lr * lr + li * li
    z_re = ((a_re - 1.0) * lr + a_im * li) / den
    z_im = (a_im * lr - (a_re - 1.0) * li) / den

    b_re, b_im = b_ref[0], b_ref[1]
    c_re, c_im = c_ref[0], c_ref[1]
    c0 = jnp.concatenate([c_re, -c_im], axis=1).astype(BF16)
    w_re, w_im = z_re, z_im
    pw_re, pw_im = a_re, a_im
    for s in range(SSM_L):
        rows = slice(s * LANES, (s + 1) * LANES)
        p_ref[rows, 0:STATE_LANES] = (w_re * b_re - w_im * b_im).astype(BF16)
        p_ref[rows, STATE_LANES:2 * STATE_LANES] = (w_re * b_im + w_im * b_re).astype(BF16)
        qt_ref[rows, 0:STATE_LANES] = (c_re * pw_re - c_im * pw_im).astype(BF16)
        qt_ref[rows, STATE_LANES:2 * STATE_LANES] = (-(c_re * pw_im + c_im * pw_re)).astype(BF16)
        w_re, w_im = w_re * a_re - w_im * a_im, w_re * a_im + w_im * a_re
        if s + 1 < SSM_L:
            pw_re, pw_im = pw_re * a_re - pw_im * a_im, pw_re * a_im + pw_im * a_re
    d_ref[...] = lax.dot_general(p_ref[...], c0, contract_last, preferred_element_type=F32).astype(BF16)

    for i in range(SSM_L):
        rows = u_ref[pl.ds(i, n_chunks, stride=SSM_L), :]
        urev_ref[:, (SSM_L - 1 - i) * LANES:(SSM_L - i) * LANES] = rows.astype(BF16)

    e_ref[...] = jnp.dot(urev_ref[...], p_ref[...], preferred_element_type=F32)

    def chunk_step(c, carry):
        h_re, h_im = carry
        hin_ref[pl.ds(c, 1), 0:STATE_LANES] = h_re
        hin_ref[pl.ds(c, 1), STATE_LANES:2 * STATE_LANES] = h_im
        e_re = e_ref[pl.ds(c, 1), 0:STATE_LANES]
        e_im = e_ref[pl.ds(c, 1), STATE_LANES:2 * STATE_LANES]
        return (pw_re * h_re - pw_im * h_im + e_re,
                pw_re * h_im + pw_im * h_re + e_im)

    zero = jnp.zeros((1, STATE_LANES), F32)
    lax.fori_loop(0, n_chunks, chunk_step, (zero, zero))

    carry_in = lax.dot_general(hin_ref[...].astype(BF16), qt_ref[...], contract_last,
                               preferred_element_type=F32)

    dskip = dskip_ref[...]
    for i in range(SSM_L):
        intra = jnp.dot(urev_ref[:, (SSM_L - 1 - i) * LANES:], d_ref[0:(i + 1) * LANES, :],
                        preferred_element_type=F32)
        u_i = u_ref[pl.ds(i, n_chunks, stride=SSM_L), :]
        y = intra + carry_in[:, i * LANES:(i + 1) * LANES] + dskip * u_i
        y_ref[pl.ds(i, n_chunks, stride=SSM_L), :] = _gelu_tanh(y)


def _ssm_mixer(u, lam, b_blocks, c_blocks, dskip):
    seq = u.shape[0]
    n_chunks = seq // SSM_L
    return pl.pallas_call(
        _ssm_kernel,
        grid=(N_LANE_TILES,),
        in_specs=[pl.BlockSpec((seq, LANES), lambda j: (0, j)),
                  pl.BlockSpec((None, 3, STATE_LANES), lambda j: (j, 0, 0)),
                  pl.BlockSpec((None, 2, LANES, STATE_LANES), lambda j: (j, 0, 0, 0)),
                  pl.BlockSpec((None, 2, LANES, STATE_LANES), lambda j: (j, 0, 0, 0)),
                  pl.BlockSpec((None, 1, LANES), lambda j: (j, 0, 0))],
        out_specs=pl.BlockSpec((seq, LANES), lambda j: (0, j)),
        out_shape=jax.ShapeDtypeStruct((seq, SSM_WIDTH), F32),
        scratch_shapes=[pltpu.VMEM((n_chunks, SSM_L * LANES), BF16),
                        pltpu.VMEM((n_chunks, 2 * STATE_LANES), F32),
                        pltpu.VMEM((n_chunks, 2 * STATE_LANES), F32),
                        pltpu.VMEM((SSM_L * LANES, 2 * STATE_LANES), BF16),
                        pltpu.VMEM((SSM_L * LANES, 2 * STATE_LANES), BF16),
                        pltpu.VMEM((SSM_L * LANES, LANES), BF16)],
        compiler_params=_cparams(1),
        name="ssm_mixer",
    )(u, lam, b_blocks, c_blocks, dskip)


def _ssm_parameter_layout(log_step, lam_re, lam_im, b_re, b_im, c_re, c_im):
    nt, gl = N_LANE_TILES, GROUPS_PER_TILE
    step = jnp.broadcast_to(jnp.exp(log_step.astype(F32))[:, None], lam_re.shape)
    lam = jnp.stack([lam_re.astype(F32).reshape(nt, STATE_LANES),
                     lam_im.astype(F32).reshape(nt, STATE_LANES),
                     step.reshape(nt, STATE_LANES)], axis=1)
    eye = jnp.eye(gl, dtype=F32)

    def blocks(x):
        x5 = x.astype(F32).reshape(nt, gl, SSM_GROUP, SSM_STATE)
        return (x5[:, :, :, None, :] * eye[None, :, None, :, None]).reshape(nt, LANES, STATE_LANES)

    b_blocks = jnp.stack([blocks(b_re.transpose(0, 2, 1)), blocks(b_im.transpose(0, 2, 1))], axis=1)
    c_blocks = jnp.stack([blocks(c_re), blocks(c_im)], axis=1)
    return lam, b_blocks, c_blocks


def _glu_kernel(y_ref, w_ref, b_ref, o_ref):
    y = y_ref[...]
    z = jnp.dot(y.astype(BF16), w_ref[...], preferred_element_type=F32) + b_ref[...]
    o_ref[...] = (y * jax.nn.sigmoid(z)).astype(o_ref.dtype)


def _glu(y, w, b, tm):
    m, n = y.shape
    tm = min(tm, m)
    return pl.pallas_call(
        _glu_kernel,
        grid=(m // tm,),
        in_specs=[pl.BlockSpec((tm, n), lambda i: (i, 0)),
                  pl.BlockSpec((n, n), lambda i: (0, 0)),
                  pl.BlockSpec((1, n), lambda i: (0, 0))],
        out_specs=pl.BlockSpec((tm, n), lambda i: (i, 0)),
        out_shape=jax.ShapeDtypeStruct((m, n), BF16),
        compiler_params=_cparams(1),
        name="glu",
    )(y, w, b)


ATT_TQ = 512
LOG2_E = math.log2(math.e)
FAR_BUCKET = REL_BUCKETS // 2 - 1


def _attn_kernel(rel_ref, q_ref, k_ref, v_ref, bucket_ref, lq1_ref, lk1_ref, lq2_ref, lk2_ref, g_ref,
                 o_ref, m_ref, l_ref, acc_ref, bias_ref, s0_ref, s1_ref):
    h = pl.program_id(0)
    i = pl.program_id(1)
    tq = ATT_TQ

    @pl.when(i == 0)
    def _():
        far = rel_ref[FAR_BUCKET, h]

        def rows(r, carry):
            r0 = pl.multiple_of(r * 8, 8)
            bucket = bucket_ref[pl.ds(r0, 8), :]
            tile = jnp.where(bucket < 0, NEG_INF, 0.0)
            for b in range(REL_BUCKETS):
                tile = jnp.where(bucket == b, (rel_ref[b, h] - far) * LOG2_E, tile)
            bias_ref[pl.ds(r0, 8), :] = tile
            return carry

        lax.fori_loop(0, tq // 8, rows, 0)

    m_ref[...] = jnp.full_like(m_ref, NEG_INF)
    l_ref[...] = jnp.zeros_like(l_ref)
    acc_ref[...] = jnp.zeros_like(acc_ref)

    def scores(block, s_ref, bias):
        start = pl.multiple_of(block * tq, tq)
        kb = k_ref[pl.ds(start, tq), :]
        for c in range(2):
            q = q_ref[:, c * ATT_QK_DIM:(c + 1) * ATT_QK_DIM]
            kc = kb[:, c * ATT_QK_DIM:(c + 1) * ATT_QK_DIM]
            s = lax.dot_general(q, kc, (((1,), (1,)), ((), ())), preferred_element_type=F32)
            s_ref[c] = s if bias is None else s + bias

    def absorb(block, s_ref):
        start = pl.multiple_of(block * tq, tq)
        vb = v_ref[pl.ds(start, tq), :]
        for c in range(2):
            tiles = [s_ref[c, :, t * LANES:(t + 1) * LANES] for t in range(tq // LANES)]
            m_tile = functools.reduce(jnp.maximum, tiles)
            m_prev = m_ref[c]
            m_new = jnp.maximum(m_prev, jnp.max(m_tile, axis=1, keepdims=True))
            scale = jnp.exp2(m_prev - m_new)
            ps = [jnp.exp2(t - m_new) for t in tiles]
            l_ref[c] = scale * l_ref[c] + functools.reduce(jnp.add, ps)
            m_ref[c] = m_new
            pv = jnp.dot(jnp.concatenate(ps, axis=1).astype(BF16), vb, preferred_element_type=F32)
            for t in range(ATT_V_DIM // LANES):
                cols = slice(t * LANES, (t + 1) * LANES)
                acc_ref[c, :, cols] = scale * acc_ref[c, :, cols] + pv[:, cols]

    scores(i, s0_ref, bias_ref[:, tq:2 * tq])

    @pl.when(i == 0)
    def _():
        absorb(i, s0_ref)

    @pl.when(i >= 1)
    def _():
        absorb(i, s0_ref)
        scores(i - 1, s1_ref, bias_ref[:, 0:tq])
        n_far = i - 1
        n_pairs = lax.shift_right_logical(n_far, 1)

        def far_pair(p, carry):
            absorb(jnp.where(p == 0, i - 1, 2 * p - 1), s1_ref)
            scores(2 * p, s0_ref, None)
            absorb(2 * p, s0_ref)
            scores(2 * p + 1, s1_ref, None)
            return carry

        lax.fori_loop(0, n_pairs, far_pair, 0)
        in_s1 = jnp.where(n_pairs == 0, i - 1, 2 * n_pairs - 1)

        @pl.when(lax.rem(n_far, 2) == 1)
        def _():
            absorb(in_s1, s1_ref)
            scores(n_far - 1, s0_ref, None)
            absorb(n_far - 1, s0_ref)

        @pl.when(lax.rem(n_far, 2) == 0)
        def _():
            absorb(in_s1, s1_ref)

    lam = (jnp.exp(jnp.sum(lq1_ref[...] * lk1_ref[...], axis=1, keepdims=True))
           - jnp.exp(jnp.sum(lq2_ref[...] * lk2_ref[...], axis=1, keepdims=True)) + LAMBDA_INIT)
    l0 = jnp.sum(l_ref[0], axis=1, keepdims=True)
    l1 = jnp.sum(l_ref[1], axis=1, keepdims=True)
    o = acc_ref[0] / l0 - lam * (acc_ref[1] / l1)
    o = o * lax.rsqrt(jnp.mean(o * o, axis=1, keepdims=True) + LN_EPS) * g_ref[...]
    o_ref[...] = (o * (1.0 - LAMBDA_INIT)).astype(o_ref.dtype)


def _t5_bucket(rel):
    half = REL_BUCKETS // 2
    max_exact = half // 2
    ret = jnp.where(rel > 0, half, 0)
    n = jnp.abs(rel)
    nf = jnp.maximum(n, 1).astype(jnp.float32)
    large = max_exact + (jnp.log(nf / max_exact) / math.log(REL_MAX_DIST / max_exact)
                         * (half - max_exact)).astype(jnp.int32)
    large = jnp.minimum(large, half - 1)
    return ret + jnp.where(n < max_exact, n, large)


def _near_buckets():
    tq = ATT_TQ
    qpos = np.arange(tq)[:, None]
    kpos = np.arange(-tq, tq)[None, :]
    rel = jnp.asarray(kpos - qpos, jnp.int32)
    visible = jnp.asarray((kpos // CHUNK) <= (qpos // CHUNK))
    return jnp.where(visible, _t5_bucket(rel), -1).astype(jnp.int32)


def _diff_attention(qkv, rel_bias, lq1, lk1, lq2, lk2, subln_g):
    seq = qkv.shape[0]
    tq = ATT_TQ
    head_blk = 2 * ATT_QK_DIM
    vec = pl.BlockSpec((1, ATT_QK_DIM), lambda h, i: (0, 0))
    return pl.pallas_call(
        _attn_kernel,
        grid=(ATT_HEADS, seq // tq),
        in_specs=[pl.BlockSpec(memory_space=pltpu.SMEM),
                  pl.BlockSpec((tq, head_blk), lambda h, i: (i, h)),
                  pl.BlockSpec((seq, head_blk), lambda h, i: (0, ATT_HEADS + h)),
                  pl.BlockSpec((seq, ATT_V_DIM), lambda h, i: (0, 2 * ATT_HEADS + h)),
                  pl.BlockSpec((tq, 2 * tq), lambda h, i: (0, 0)),
                  vec, vec, vec, vec,
                  pl.BlockSpec((1, ATT_V_DIM), lambda h, i: (0, 0))],
        out_specs=pl.BlockSpec((tq, ATT_V_DIM), lambda h, i: (i, h)),
        out_shape=jax.ShapeDtypeStruct((seq, ATT_WIDTH), BF16),
        scratch_shapes=[pltpu.VMEM((2, tq, LANES), F32),
                        pltpu.VMEM((2, tq, LANES), F32),
                        pltpu.VMEM((2, tq, ATT_V_DIM), F32),
                        pltpu.VMEM((tq, 2 * tq), F32),
                        pltpu.VMEM((2, tq, tq), F32),
                        pltpu.VMEM((2, tq, tq), F32)],
        compiler_params=_cparams(2),
        name="diff_attention",
    )(rel_bias.astype(F32), qkv, qkv, qkv, _near_buckets(), lq1, lk1, lq2, lk2, subln_g)


MM_COL_CHUNK = 1024
LN_ROW_CHUNK = 128


def _residual_layer_norm(res_ref, acc_ref, g_ref, b_ref, out_refs):
    g = g_ref[...]
    b = b_ref[...]
    for r0 in range(0, acc_ref.shape[0], LN_ROW_CHUNK):
        rows = slice(r0, r0 + LN_ROW_CHUNK)
        r = ALPHA * res_ref[rows, :] + acc_ref[rows, :]
        mu = jnp.mean(r, axis=1, keepdims=True)
        xc = r - mu
        var = jnp.mean(xc * xc, axis=1, keepdims=True)
        h = xc * lax.rsqrt(var + LN_EPS) * g + b
        for o_ref in out_refs:
            o_ref[rows, :] = h.astype(o_ref.dtype)


def _outproj_kernel(ys_ref, ya_ref, wt_ref, wb_ref, x_ref, g_ref, b_ref, h_ref, hb_ref):
    k = pl.program_id(1)

    @pl.when(k == 0)
    def _():
        h_ref[...] = jnp.zeros_like(h_ref)

    for c in range(0, h_ref.shape[1], MM_COL_CHUNK):
        cols = slice(c, c + MM_COL_CHUNK)
        h_ref[:, cols] += (jnp.dot(ys_ref[...], wt_ref[:, cols], preferred_element_type=F32)
                           + jnp.dot(ya_ref[...], wb_ref[:, cols], preferred_element_type=F32))

    @pl.when(k == pl.num_programs(1) - 1)
    def _():
        _residual_layer_norm(x_ref, h_ref, g_ref, b_ref, (h_ref, hb_ref))


def _outproj_ln(ys, ya, w_out, x, g, b, tm, tk):
    m, half = ys.shape
    d = w_out.shape[1]
    tm = min(tm, m)
    nk = half // tk
    row = pl.BlockSpec((1, d), lambda i, k: (0, 0))
    return pl.pallas_call(
        _outproj_kernel,
        grid=(m // tm, nk),
        in_specs=[pl.BlockSpec((tm, tk), lambda i, k: (i, k)),
                  pl.BlockSpec((tm, tk), lambda i, k: (i, k)),
                  pl.BlockSpec((tk, d), lambda i, k: (k, 0)),
                  pl.BlockSpec((tk, d), lambda i, k: (k + nk, 0)),
                  pl.BlockSpec((tm, d), lambda i, k: (i, 0), pipeline_mode=pl.Buffered(1)),
                  row, row],
        out_specs=[pl.BlockSpec((tm, d), lambda i, k: (i, 0)),
                   pl.BlockSpec((tm, d), lambda i, k: (i, 0))],
        out_shape=[jax.ShapeDtypeStruct((m, d), F32), jax.ShapeDtypeStruct((m, d), BF16)],
        compiler_params=_cparams(2, VMEM_LIMIT_LARGE),
        name="outproj_ln",
    )(ys, ya, w_out, w_out, x, g, b)


FFN_TF = 256
FFN_TILES = D_FF // FFN_TF
FFN_TILES_PER_STEP = 2
FFN_STEPS = -(-FFN_TILES // FFN_TILES_PER_STEP)
FFN_HALO = 8


def _ffn_tile(f, k):
    return jnp.minimum(f * FFN_TILES_PER_STEP + k, FFN_TILES - 1)


def _ffn_kernel(*refs):
    n = FFN_TILES_PER_STEP
    h_ref = refs[0]
    wa_refs, wg_refs, cw_refs, cb_refs, wd_refs = (refs[1 + k * n:1 + (k + 1) * n] for k in range(5))
    r_ref, lg_ref, lb_ref, o_ref, halo_ref = refs[1 + 5 * n:]
    i = pl.program_id(0)
    f = pl.program_id(1)
    tm = h_ref.shape[0]

    @pl.when(f == 0)
    def _():
        o_ref[...] = jnp.zeros_like(o_ref)

    h = h_ref[...]
    row = lax.broadcasted_iota(jnp.int32, (tm, FFN_TF), 0)
    acts = []
    for k in range(n):
        tile = _ffn_tile(f, k)
        value = jnp.dot(h, wa_refs[k][...], preferred_element_type=F32)
        gate = jnp.dot(h, wg_refs[k][...], preferred_element_type=F32)
        halo = jnp.where(i == 0, 0.0, halo_ref[tile])
        halo_ref[tile] = gate[tm - FFN_HALO:tm, :]
        prev1 = jnp.where(row == 0, halo[FFN_HALO - 1:FFN_HALO, :], pltpu.roll(gate, 1, 0))
        prev2 = jnp.where(row == 0, halo[FFN_HALO - 2:FFN_HALO - 1, :],
                          jnp.where(row == 1, halo[FFN_HALO - 1:FFN_HALO, :], pltpu.roll(gate, 2, 0)))
        cw = cw_refs[k]
        gc = cb_refs[k][...] + prev2 * cw[0:1, :] + prev1 * cw[1:2, :] + gate * cw[2:3, :]
        act = (gc * jax.nn.sigmoid(gc)) * value
        if k > 0:
            act = jnp.where(f * n + k < FFN_TILES, act, 0.0)
        acts.append(act.astype(BF16))

    for c in range(0, o_ref.shape[1], MM_COL_CHUNK):
        cols = slice(c, c + MM_COL_CHUNK)
        contrib = jnp.dot(acts[0], wd_refs[0][:, cols], preferred_element_type=F32)
        for k in range(1, n):
            contrib += jnp.dot(acts[k], wd_refs[k][:, cols], preferred_element_type=F32)
        o_ref[:, cols] += contrib

    @pl.when(f == pl.num_programs(1) - 1)
    def _():
        _residual_layer_norm(r_ref, o_ref, lg_ref, lb_ref, (o_ref,))


def _ffn_ln(h_bf, w_up, conv_w, conv_b, w_down, h1, g, b, tm):
    m, d = h_bf.shape
    tm = min(tm, m)
    n = FFN_TILES_PER_STEP
    once = pl.Buffered(1)
    row = pl.BlockSpec((1, d), lambda i, f: (0, 0))
    per_tile = lambda shape, index: [pl.BlockSpec(shape, functools.partial(index, k=k)) for k in range(n)]
    in_specs = ([pl.BlockSpec((tm, d), lambda i, f: (i, 0), pipeline_mode=once)]
                + per_tile((d, FFN_TF), lambda i, f, k: (0, _ffn_tile(f, k)))
                + per_tile((d, FFN_TF), lambda i, f, k: (0, FFN_TILES + _ffn_tile(f, k)))
                + per_tile((3, FFN_TF), lambda i, f, k: (0, _ffn_tile(f, k)))
                + per_tile((1, FFN_TF), lambda i, f, k: (0, _ffn_tile(f, k)))
                + per_tile((FFN_TF, d), lambda i, f, k: (_ffn_tile(f, k), 0))
                + [pl.BlockSpec((tm, d), lambda i, f: (i, 0), pipeline_mode=once), row, row])
    return pl.pallas_call(
        _ffn_kernel,
        grid=(m // tm, FFN_STEPS),
        in_specs=in_specs,
        out_specs=pl.BlockSpec((tm, d), lambda i, f: (i, 0), pipeline_mode=once),
        out_shape=jax.ShapeDtypeStruct((m, d), F32),
        scratch_shapes=[pltpu.VMEM((FFN_TILES, FFN_HALO, FFN_TF), F32)],
        compiler_params=_cparams(2, VMEM_LIMIT_LARGE),
        name="ffn_ln",
    )(h_bf, *([w_up] * (2 * n)), *([conv_w] * n), *([conv_b] * n), *([w_down] * n), h1, g, b)


def kernel(x, w_in, ssm_log_step, ssm_lambda_re, ssm_lambda_im, ssm_b_re, ssm_b_im, ssm_c_re, ssm_c_im, ssm_d, ssm_w_glu, ssm_b_glu, att_lambda_q1, att_lambda_k1, att_lambda_q2, att_lambda_k2, att_subln_g, rel_bias, w_out, ln1_g, ln1_b, ffn_w_up, ffn_conv_w, ffn_conv_b, ffn_w_down, ln2_g, ln2_b):
    bsz, seq, _ = x.shape
    assert bsz == 1 and DEPTH == 1
    l = 0
    xs = x[0]

    qk_scale = ATT_QK_DIM ** -0.5 * LOG2_E
    col_scale = np.ones((1, w_in.shape[2]), np.float32)
    col_scale[:, SSM_WIDTH:SSM_WIDTH + QK_WIDTH] = qk_scale
    w_in_bf = (w_in[l] * jnp.asarray(col_scale)).astype(BF16)
    x_bf = xs.astype(BF16)

    u = _matmul(x_bf, w_in_bf, F32, 1024, 1024, 1024, 0, SSM_WIDTH)
    qkv = _matmul(x_bf, w_in_bf, BF16, 1024, 1024, 1024, SSM_WIDTH, w_in.shape[2] - SSM_WIDTH)

    lam, b_blocks, c_blocks = _ssm_parameter_layout(
        ssm_log_step[l], ssm_lambda_re[l], ssm_lambda_im[l], ssm_b_re[l], ssm_b_im[l],
        ssm_c_re[l], ssm_c_im[l])
    dskip = ssm_d[l].astype(F32).reshape(N_LANE_TILES, 1, LANES)
    y = _ssm_mixer(u, lam, b_blocks, c_blocks, dskip)
    y_ssm = _glu(y, ssm_w_glu[l].astype(BF16), ssm_b_glu[l].astype(F32).reshape(1, SSM_WIDTH), 512)

    vec = lambda a: a.astype(F32).reshape(1, ATT_QK_DIM)
    y_att = _diff_attention(qkv, rel_bias,
                            vec(att_lambda_q1[l]), vec(att_lambda_k1[l]),
                            vec(att_lambda_q2[l]), vec(att_lambda_k2[l]),
                            att_subln_g[l].astype(F32).reshape(1, ATT_V_DIM))

    h1, h1_bf = _outproj_ln(y_ssm, y_att, w_out[l].astype(BF16), xs,
                            ln1_g[l].reshape(1, D_MODEL), ln1_b[l].reshape(1, D_MODEL), 512, 512)

    out = _ffn_ln(h1_bf, ffn_w_up[l].astype(BF16), ffn_conv_w[l].astype(F32),
                  ffn_conv_b[l].astype(F32).reshape(1, D_FF), ffn_w_down[l].astype(BF16), h1,
                  ln2_g[l].reshape(1, D_MODEL), ln2_b[l].reshape(1, D_MODEL), 512)
    return out[None]
```

```python
import functools
import math

import numpy as np
import jax
import jax.numpy as jnp
from jax import lax
from jax.experimental import pallas as pl
from jax.experimental.pallas import tpu as pltpu

F32 = jnp.float32
BF16 = jnp.bfloat16

D_MODEL = 4096
CHUNK = 64
SSM_WIDTH = 2048
SSM_GROUP = 16
SSM_GROUPS = SSM_WIDTH // SSM_GROUP
SSM_STATE = 64
ATT_QK_DIM = 128
ATT_V_DIM = 256
ATT_WIDTH = 2048
ATT_HEADS = 8
QK_WIDTH = 2048
D_FF = 11008
REL_BUCKETS = 32
REL_MAX_DIST = 128
DEPTH = 1
ALPHA = (2 * DEPTH) ** 0.25
LN_EPS = 1e-5
NEG_INF = -1e30
LAMBDA_INIT = 0.8 - 0.6 * math.exp(-0.3 * 0)

LANES = 128
SSM_L = 16
GROUPS_PER_TILE = LANES // SSM_GROUP
N_LANE_TILES = SSM_WIDTH // LANES
STATE_LANES = GROUPS_PER_TILE * SSM_STATE
VMEM_LIMIT = 56 * 1024 * 1024
VMEM_LIMIT_LARGE = 62 * 1024 * 1024


def _cparams(n_axes, vmem=VMEM_LIMIT):
    return pltpu.CompilerParams(dimension_semantics=("arbitrary",) * n_axes,
                                vmem_limit_bytes=vmem)


def _matmul_kernel(x_ref, w_ref, o_ref, acc_ref):
    k = pl.program_id(2)

    @pl.when(k == 0)
    def _():
        acc_ref[...] = jnp.zeros_like(acc_ref)

    acc_ref[...] += jnp.dot(x_ref[...], w_ref[...], preferred_element_type=F32)

    @pl.when(k == pl.num_programs(2) - 1)
    def _():
        o_ref[...] = acc_ref[...].astype(o_ref.dtype)


def _matmul(x, w, out_dtype, tm, tn, tk, col_start=0, n_cols=None):
    m, kdim = x.shape
    n = w.shape[1] - col_start if n_cols is None else n_cols
    tm, tn, tk = min(tm, m), min(tn, n), min(tk, kdim)
    assert m % tm == 0 and n % tn == 0 and kdim % tk == 0 and col_start % tn == 0
    j0 = col_start // tn
    return pl.pallas_call(
        _matmul_kernel,
        grid=(m // tm, n // tn, kdim // tk),
        in_specs=[pl.BlockSpec((tm, tk), lambda i, j, k: (i, k)),
                  pl.BlockSpec((tk, tn), lambda i, j, k: (k, j + j0))],
        out_specs=pl.BlockSpec((tm, tn), lambda i, j, k: (i, j)),
        out_shape=jax.ShapeDtypeStruct((m, n), out_dtype),
        scratch_shapes=[pltpu.VMEM((tm, tn), F32)],
        compiler_params=_cparams(3),
        name="matmul",
    )(x, w)


def _gelu_tanh(x):
    c = math.sqrt(2.0 / math.pi)
    return 0.5 * x * (1.0 + jnp.tanh(c * (x + 0.044715 * (x * x * x))))


def _ssm_kernel(u_ref, lam_ref, b_ref, c_ref, dskip_ref, y_ref,
                urev_ref, e_ref, hin_ref, p_ref, qt_ref, d_ref):
    n_chunks = u_ref.shape[0] // SSM_L
    contract_last = (((1,), (1,)), ((), ()))

    lr = lam_ref[0:1, :]
    li = lam_ref[1:2, :]
    step = lam_ref[2:3, :]
    mag = jnp.exp(lr * step)
    a_re = mag * jnp.cos(li * step)
    a_im = mag * jnp.sin(li * step)
    den = lr * lr + li * li
    z_re = ((a_re - 1.0) * lr + a_im * li) / den
    z_im = (a_im * lr - (a_re - 1.0) * li) / den

    b_re, b_im = b_ref[0], b_ref[1]
    c_re, c_im = c_ref[0], c_ref[1]
    c0 = jnp.concatenate([c_re, -c_im], axis=1).astype(BF16)
    w_re, w_im = z_re, z_im
    pw_re, pw_im = a_re, a_im
    for s in range(SSM_L):
        rows = slice(s * LANES, (s + 1) * LANES)
        p_ref[rows, 0:STATE_LANES] = (w_re * b_re - w_im * b_im).astype(BF16)
        p_ref[rows, STATE_LANES:2 * STATE_LANES] = (w_re * b_im + w_im * b_re).astype(BF16)
        qt_ref[rows, 0:STATE_LANES] = (c_re * pw_re - c_im * pw_im).astype(BF16)
        qt_ref[rows, STATE_LANES:2 * STATE_LANES] = (-(c_re * pw_im + c_im * pw_re)).astype(BF16)
        w_re, w_im = w_re * a_re - w_im * a_im, w_re * a_im + w_im * a_re
        if s + 1 < SSM_L:
            pw_re, pw_im = pw_re * a_re - pw_im * a_im, pw_re * a_im + pw_im * a_re
    d_ref[...] = lax.dot_general(p_ref[...], c0, contract_last, preferred_element_type=F32).astype(BF16)

    for i in range(SSM_L):
        rows = u_ref[pl.ds(i, n_chunks, stride=SSM_L), :]
        urev_ref[:, (SSM_L - 1 - i) * LANES:(SSM_L - i) * LANES] = rows.astype(BF16)

    e_ref[...] = jnp.dot(urev_ref[...], p_ref[...], preferred_element_type=F32)

    def chunk_step(c, carry):
        h_re, h_im = carry
        hin_ref[pl.ds(c, 1), 0:STATE_LANES] = h_re
        hin_ref[pl.ds(c, 1), STATE_LANES:2 * STATE_LANES] = h_im
        e_re = e_ref[pl.ds(c, 1), 0:STATE_LANES]
        e_im = e_ref[pl.ds(c, 1), STATE_LANES:2 * STATE_LANES]
        return (pw_re * h_re - pw_im * h_im + e_re,
                pw_re * h_im + pw_im * h_re + e_im)

    zero = jnp.zeros((1, STATE_LANES), F32)
    lax.fori_loop(0, n_chunks, chunk_step, (zero, zero))

    carry_in = lax.dot_general(hin_ref[...].astype(BF16), qt_ref[...], contract_last,
                               preferred_element_type=F32)

    dskip = dskip_ref[...]
    for i in range(SSM_L):
        intra = jnp.dot(urev_ref[:, (SSM_L - 1 - i) * LANES:], d_ref[0:(i + 1) * LANES, :],
                        preferred_element_type=F32)
        u_i = u_ref[pl.ds(i, n_chunks, stride=SSM_L), :]
        y = intra + carry_in[:, i * LANES:(i + 1) * LANES] + dskip * u_i
        y_ref[pl.ds(i, n_chunks, stride=SSM_L), :] = _gelu_tanh(y)


def _ssm_mixer(u, lam, b_blocks, c_blocks, dskip):
    seq = u.shape[0]
    n_chunks = seq // SSM_L
    return pl.pallas_call(
        _ssm_kernel,
        grid=(N_LANE_TILES,),
        in_specs=[pl.BlockSpec((seq, LANES), lambda j: (0, j)),
                  pl.BlockSpec((None, 3, STATE_LANES), lambda j: (j, 0, 0)),
                  pl.BlockSpec((None, 2, LANES, STATE_LANES), lambda j: (j, 0, 0, 0)),
                  pl.BlockSpec((None, 2, LANES, STATE_LANES), lambda j: (j, 0, 0, 0)),
                  pl.BlockSpec((None, 1, LANES), lambda j: (j, 0, 0))],
        out_specs=pl.BlockSpec((seq, LANES), lambda j: (0, j)),
        out_shape=jax.ShapeDtypeStruct((seq, SSM_WIDTH), F32),
        scratch_shapes=[pltpu.VMEM((n_chunks, SSM_L * LANES), BF16),
                        pltpu.VMEM((n_chunks, 2 * STATE_LANES), F32),
                        pltpu.VMEM((n_chunks, 2 * STATE_LANES), F32),
                        pltpu.VMEM((SSM_L * LANES, 2 * STATE_LANES), BF16),
                        pltpu.VMEM((SSM_L * LANES, 2 * STATE_LANES), BF16),
                        pltpu.VMEM((SSM_L * LANES, LANES), BF16)],
        compiler_params=_cparams(1),
        name="ssm_mixer",
    )(u, lam, b_blocks, c_blocks, dskip)


def _ssm_parameter_layout(log_step, lam_re, lam_im, b_re, b_im, c_re, c_im):
    nt, gl = N_LANE_TILES, GROUPS_PER_TILE
    step = jnp.broadcast_to(jnp.exp(log_step.astype(F32))[:, None], lam_re.shape)
    lam = jnp.stack([lam_re.astype(F32).reshape(nt, STATE_LANES),
                     lam_im.astype(F32).reshape(nt, STATE_LANES),
                     step.reshape(nt, STATE_LANES)], axis=1)
    eye = jnp.eye(gl, dtype=F32)

    def blocks(x):
        x5 = x.astype(F32).reshape(nt, gl, SSM_GROUP, SSM_STATE)
        return (x5[:, :, :, None, :] * eye[None, :, None, :, None]).reshape(nt, LANES, STATE_LANES)

    b_blocks = jnp.stack([blocks(b_re.transpose(0, 2, 1)), blocks(b_im.transpose(0, 2, 1))], axis=1)
    c_blocks = jnp.stack([blocks(c_re), blocks(c_im)], axis=1)
    return lam, b_blocks, c_blocks


def _glu_kernel(y_ref, w_ref, b_ref, o_ref):
    y = y_ref[...]
    z = jnp.dot(y.astype(BF16), w_ref[...], preferred_element_type=F32) + b_ref[...]
    o_ref[...] = (y * jax.nn.sigmoid(z)).astype(o_ref.dtype)


def _glu(y, w, b, tm):
    m, n = y.shape
    tm = min(tm, m)
    return pl.pallas_call(
        _glu_kernel,
        grid=(m // tm,),
        in_specs=[pl.BlockSpec((tm, n), lambda i: (i, 0)),
                  pl.BlockSpec((n, n), lambda i: (0, 0)),
                  pl.BlockSpec((1, n), lambda i: (0, 0))],
        out_specs=pl.BlockSpec((tm, n), lambda i: (i, 0)),
        out_shape=jax.ShapeDtypeStruct((m, n), BF16),
        compiler_params=_cparams(1),
        name="glu",
    )(y, w, b)


ATT_TQ = 512
LOG2_E = math.log2(math.e)
FAR_BUCKET = REL_BUCKETS // 2 - 1


def _attn_kernel(rel_ref, q_ref, k_ref, v_ref, bucket_ref, lq1_ref, lk1_ref, lq2_ref, lk2_ref, g_ref,
                 o_ref, m_ref, l_ref, acc_ref, bias_ref, s0_ref, s1_ref):
    h = pl.program_id(0)
    i = pl.program_id(1)
    tq = ATT_TQ

    @pl.when(i == 0)
    def _():
        far = rel_ref[FAR_BUCKET, h]

        def rows(r, carry):
            r0 = pl.multiple_of(r * 8, 8)
            bucket = bucket_ref[pl.ds(r0, 8), :]
            tile = jnp.where(bucket < 0, NEG_INF, 0.0)
            for b in range(REL_BUCKETS):
                tile = jnp.where(bucket == b, (rel_ref[b, h] - far) * LOG2_E, tile)
            bias_ref[pl.ds(r0, 8), :] = tile
            return carry

        lax.fori_loop(0, tq // 8, rows, 0)

    m_ref[...] = jnp.full_like(m_ref, NEG_INF)
    l_ref[...] = jnp.zeros_like(l_ref)
    acc_ref[...] = jnp.zeros_like(acc_ref)

    def scores(block, s_ref, bias):
        start = pl.multiple_of(block * tq, tq)
        kb = k_ref[pl.ds(start, tq), :]
        for c in range(2):
            q = q_ref[:, c * ATT_QK_DIM:(c + 1) * ATT_QK_DIM]
            kc = kb[:, c * ATT_QK_DIM:(c + 1) * ATT_QK_DIM]
            s = lax.dot_general(q, kc, (((1,), (1,)), ((), ())), preferred_element_type=F32)
            s_ref[c] = s if bias is None else s + bias

    def absorb(block, s_ref):
        start = pl.multiple_of(block * tq, tq)
        vb = v_ref[pl.ds(start, tq), :]
        for c in range(2):
            tiles = [s_ref[c, :, t * LANES:(t + 1) * LANES] for t in range(tq // LANES)]
            m_tile = functools.reduce(jnp.maximum, tiles)
            m_prev = m_ref[c]
            m_new = jnp.maximum(m_prev, jnp.max(m_tile, axis=1, keepdims=True))
            scale = jnp.exp2(m_prev - m_new)
            ps = [jnp.exp2(t - m_new) for t in tiles]
            l_ref[c] = scale * l_ref[c] + functools.reduce(jnp.add, ps)
            m_ref[c] = m_new
            pv = jnp.dot(jnp.concatenate(ps, axis=1).astype(BF16), vb, preferred_element_type=F32)
            for t in range(ATT_V_DIM // LANES):
                cols = slice(t * LANES, (t + 1) * LANES)
                acc_ref[c, :, cols] = scale * acc_ref[c, :, cols] + pv[:, cols]

    scores(i, s0_ref, bias_ref[:, tq:2 * tq])

    @pl.when(i == 0)
    def _():
        absorb(i, s0_ref)

    @pl.when(i >= 1)
    def _():
        absorb(i, s0_ref)
        scores(i - 1, s1_ref, bias_ref[:, 0:tq])
        n_far = i - 1
        n_pairs = lax.shift_right_logical(n_far, 1)

        def far_pair(p, carry):
            absorb(jnp.where(p == 0, i - 1, 2 * p - 1), s1_ref)
            scores(2 * p, s0_ref, None)
            absorb(2 * p, s0_ref)
            scores(2 * p + 1, s1_ref, None)
            return carry

        lax.fori_loop(0, n_pairs, far_pair, 0)
        in_s1 = jnp.where(n_pairs == 0, i - 1, 2 * n_pairs - 1)

        @pl.when(lax.rem(n_far, 2) == 1)
        def _():
            absorb(in_s1, s1_ref)
            scores(n_far - 1, s0_ref, None)
            absorb(n_far - 1, s0_ref)

        @pl.when(lax.rem(n_far, 2) == 0)
        def _():
            absorb(in_s1, s1_ref)

    lam = (jnp.exp(jnp.sum(lq1_ref[...] * lk1_ref[...], axis=1, keepdims=True))
           - jnp.exp(jnp.sum(lq2_ref[...] * lk2_ref[...], axis=1, keepdims=True)) + LAMBDA_INIT)
    l0 = jnp.sum(l_ref[0], axis=1, keepdims=True)
    l1 = jnp.sum(l_ref[1], axis=1, keepdims=True)
    o = acc_ref[0] / l0 - lam * (acc_ref[1] / l1)
    o = o * lax.rsqrt(jnp.mean(o * o, axis=1, keepdims=True) + LN_EPS) * g_ref[...]
    o_ref[...] = (o * (1.0 - LAMBDA_INIT)).astype(o_ref.dtype)


def _t5_bucket(rel):
    half = REL_BUCKETS // 2
    max_exact = half // 2
    ret = jnp.where(rel > 0, half, 0)
    n = jnp.abs(rel)
    nf = jnp.maximum(n, 1).astype(jnp.float32)
    large = max_exact + (jnp.log(nf / max_exact) / math.log(REL_MAX_DIST / max_exact)
                         * (half - max_exact)).astype(jnp.int32)
    large = jnp.minimum(large, half - 1)
    return ret + jnp.where(n < max_exact, n, large)


def _near_buckets():
    tq = ATT_TQ
    qpos = np.arange(tq)[:, None]
    kpos = np.arange(-tq, tq)[None, :]
    rel = jnp.asarray(kpos - qpos, jnp.int32)
    visible = jnp.asarray((kpos // CHUNK) <= (qpos // CHUNK))
    return jnp.where(visible, _t5_bucket(rel), -1).astype(jnp.int32)


def _diff_attention(qkv, rel_bias, lq1, lk1, lq2, lk2, subln_g):
    seq = qkv.shape[0]
    tq = ATT_TQ
    head_blk = 2 * ATT_QK_DIM
    vec = pl.BlockSpec((1, ATT_QK_DIM), lambda h, i: (0, 0))
    return pl.pallas_call(
        _attn_kernel,
        grid=(ATT_HEADS, seq // tq),
        in_specs=[pl.BlockSpec(memory_space=pltpu.SMEM),
                  pl.BlockSpec((tq, head_blk), lambda h, i: (i, h)),
                  pl.BlockSpec((seq, head_blk), lambda h, i: (0, ATT_HEADS + h)),
                  pl.BlockSpec((seq, ATT_V_DIM), lambda h, i: (0, 2 * ATT_HEADS + h)),
                  pl.BlockSpec((tq, 2 * tq), lambda h, i: (0, 0)),
                  vec, vec, vec, vec,
                  pl.BlockSpec((1, ATT_V_DIM), lambda h, i: (0, 0))],
        out_specs=pl.BlockSpec((tq, ATT_V_DIM), lambda h, i: (i, h)),
        out_shape=jax.ShapeDtypeStruct((seq, ATT_WIDTH), BF16),
        scratch_shapes=[pltpu.VMEM((2, tq, LANES), F32),
                        pltpu.VMEM((2, tq, LANES), F32),
                        pltpu.VMEM((2, tq, ATT_V_DIM), F32),
                        pltpu.VMEM((tq, 2 * tq), F32),
                        pltpu.VMEM((2, tq, tq), F32),
                        pltpu.VMEM((2, tq, tq), F32)],
        compiler_params=_cparams(2),
        name="diff_attention",
    )(rel_bias.astype(F32), qkv, qkv, qkv, _near_buckets(), lq1, lk1, lq2, lk2, subln_g)


MM_COL_CHUNK = 1024
LN_ROW_CHUNK = 128


def _residual_layer_norm(res_ref, acc_ref, g_ref, b_ref, out_refs):
    g = g_ref[...]
    b = b_ref[...]
    for r0 in range(0, acc_ref.shape[0], LN_ROW_CHUNK):
        rows = slice(r0, r0 + LN_ROW_CHUNK)
        r = ALPHA * res_ref[rows, :] + acc_ref[rows, :]
        mu = jnp.mean(r, axis=1, keepdims=True)
        xc = r - mu
        var = jnp.mean(xc * xc, axis=1, keepdims=True)
        h = xc * lax.rsqrt(var + LN_EPS) * g + b
        for o_ref in out_refs:
            o_ref[rows, :] = h.astype(o_ref.dtype)


def _outproj_kernel(ys_ref, ya_ref, wt_ref, wb_ref, x_ref, g_ref, b_ref, h_ref, hb_ref):
    k = pl.program_id(1)

    @pl.when(k == 0)
    def _():
        h_ref[...] = jnp.zeros_like(h_ref)

    for c in range(0, h_ref.shape[1], MM_COL_CHUNK):
        cols = slice(c, c + MM_COL_CHUNK)
        h_ref[:, cols] += (jnp.dot(ys_ref[...], wt_ref[:, cols], preferred_element_type=F32)
                           + jnp.dot(ya_ref[...], wb_ref[:, cols], preferred_element_type=F32))

    @pl.when(k == pl.num_programs(1) - 1)
    def _():
        _residual_layer_norm(x_ref, h_ref, g_ref, b_ref, (h_ref, hb_ref))


def _outproj_ln(ys, ya, w_out, x, g, b, tm, tk):
    m, half = ys.shape
    d = w_out.shape[1]
    tm = min(tm, m)
    nk = half // tk
    row = pl.BlockSpec((1, d), lambda i, k: (0, 0))
    return pl.pallas_call(
        _outproj_kernel,
        grid=(m // tm, nk),
        in_specs=[pl.BlockSpec((tm, tk), lambda i, k: (i, k)),
                  pl.BlockSpec((tm, tk), lambda i, k: (i, k)),
                  pl.BlockSpec((tk, d), lambda i, k: (k, 0)),
                  pl.BlockSpec((tk, d), lambda i, k: (k + nk, 0)),
                  pl.BlockSpec((tm, d), lambda i, k: (i, 0), pipeline_mode=pl.Buffered(1)),
                  row, row],
        out_specs=[pl.BlockSpec((tm, d), lambda i, k: (i, 0)),
                   pl.BlockSpec((tm, d), lambda i, k: (i, 0))],
        out_shape=[jax.ShapeDtypeStruct((m, d), F32), jax.ShapeDtypeStruct((m, d), BF16)],
        compiler_params=_cparams(2, VMEM_LIMIT_LARGE),
        name="outproj_ln",
    )(ys, ya, w_out, w_out, x, g, b)


FFN_TF = 256
FFN_TILES = D_FF // FFN_TF
FFN_TILES_PER_STEP = 2
FFN_STEPS = -(-FFN_TILES // FFN_TILES_PER_STEP)
FFN_TILES_PAD = FFN_STEPS * FFN_TILES_PER_STEP
FFN_HALO = 8


def _cast_up_tiles_kernel(a_ref, g_ref, o_ref):
    t = pl.program_id(0)

    @pl.when(t < FFN_TILES)
    def _():
        o_ref[:, 0:FFN_TF] = a_ref[...].astype(BF16)
        o_ref[:, FFN_TF:2 * FFN_TF] = g_ref[...].astype(BF16)

    @pl.when(t >= FFN_TILES)
    def _():
        o_ref[...] = jnp.zeros_like(o_ref)


def _cast_up_tiles(w_up):
    d = w_up.shape[0]
    last = FFN_TILES - 1
    return pl.pallas_call(
        _cast_up_tiles_kernel,
        grid=(FFN_TILES_PAD,),
        in_specs=[pl.BlockSpec((d, FFN_TF), lambda t: (0, jnp.minimum(t, last))),
                  pl.BlockSpec((d, FFN_TF), lambda t: (0, FFN_TILES + jnp.minimum(t, last)))],
        out_specs=pl.BlockSpec((None, d, 2 * FFN_TF), lambda t: (t, 0, 0)),
        out_shape=jax.ShapeDtypeStruct((FFN_TILES_PAD, d, 2 * FFN_TF), BF16),
        compiler_params=_cparams(1),
        name="cast_w_up",
    )(w_up, w_up)


def _cast_down_kernel(w_ref, o_ref):
    t = pl.program_id(0)

    @pl.when(t < FFN_TILES)
    def _():
        o_ref[...] = w_ref[...].astype(BF16)

    @pl.when(t >= FFN_TILES)
    def _():
        o_ref[...] = jnp.zeros_like(o_ref)


def _cast_down(w_down):
    d = w_down.shape[1]
    return pl.pallas_call(
        _cast_down_kernel,
        grid=(FFN_TILES_PAD,),
        in_specs=[pl.BlockSpec((FFN_TF, d), lambda t: (jnp.minimum(t, FFN_TILES - 1), 0))],
        out_specs=pl.BlockSpec((FFN_TF, d), lambda t: (t, 0)),
        out_shape=jax.ShapeDtypeStruct((FFN_TILES_PAD * FFN_TF, d), BF16),
        compiler_params=_cparams(1),
        name="cast_w_down",
    )(w_down)


def _ffn_kernel(h_ref, wu_ref, cp_ref, wd_ref, r_ref, lg_ref, lb_ref, o_ref, halo_ref):
    n = FFN_TILES_PER_STEP
    i = pl.program_id(0)
    f = pl.program_id(1)
    tm = h_ref.shape[0]

    @pl.when(f == 0)
    def _():
        o_ref[...] = jnp.zeros_like(o_ref)

    h = h_ref[...]
    row = lax.broadcasted_iota(jnp.int32, (tm, FFN_TF), 0)
    acts = []
    for k in range(n):
        tile = f * n + k
        cols = slice(k * FFN_TF, (k + 1) * FFN_TF)
        value = jnp.dot(h, wu_ref[k, :, 0:FFN_TF], preferred_element_type=F32)
        gate = jnp.dot(h, wu_ref[k, :, FFN_TF:2 * FFN_TF], preferred_element_type=F32)
        halo = jnp.where(i == 0, 0.0, halo_ref[tile])
        halo_ref[tile] = gate[tm - FFN_HALO:tm, :]
        prev1 = jnp.where(row == 0, halo[FFN_HALO - 1:FFN_HALO, :], pltpu.roll(gate, 1, 0))
        prev2 = jnp.where(row == 0, halo[FFN_HALO - 2:FFN_HALO - 1, :],
                          jnp.where(row == 1, halo[FFN_HALO - 1:FFN_HALO, :], pltpu.roll(gate, 2, 0)))
        gc = (cp_ref[3:4, cols] + prev2 * cp_ref[0:1, cols] + prev1 * cp_ref[1:2, cols]
              + gate * cp_ref[2:3, cols])
        acts.append(((gc * jax.nn.sigmoid(gc)) * value).astype(BF16))

    for c in range(0, o_ref.shape[1], MM_COL_CHUNK):
        cols = slice(c, c + MM_COL_CHUNK)
        contrib = jnp.dot(acts[0], wd_ref[0:FFN_TF, cols], preferred_element_type=F32)
        for k in range(1, n):
            contrib += jnp.dot(acts[k], wd_ref[k * FFN_TF:(k + 1) * FFN_TF, cols],
                               preferred_element_type=F32)
        o_ref[:, cols] += contrib

    @pl.when(f == pl.num_programs(1) - 1)
    def _():
        _residual_layer_norm(r_ref, o_ref, lg_ref, lb_ref, (o_ref,))


def _ffn_ln(h_bf, w_up_tiles, conv_params, w_down_pad, h1, g, b, tm):
    m, d = h_bf.shape
    tm = min(tm, m)
    n = FFN_TILES_PER_STEP
    once = pl.Buffered(1)
    row = pl.BlockSpec((1, d), lambda i, f: (0, 0))
    return pl.pallas_call(
        _ffn_kernel,
        grid=(m // tm, FFN_STEPS),
        in_specs=[pl.BlockSpec((tm, d), lambda i, f: (i, 0), pipeline_mode=once),
                  pl.BlockSpec((n, d, 2 * FFN_TF), lambda i, f: (f, 0, 0)),
                  pl.BlockSpec((4, n * FFN_TF), lambda i, f: (0, f)),
                  pl.BlockSpec((n * FFN_TF, d), lambda i, f: (f, 0)),
                  pl.BlockSpec((tm, d), lambda i, f: (i, 0), pipeline_mode=once),
                  row, row],
        out_specs=pl.BlockSpec((tm, d), lambda i, f: (i, 0), pipeline_mode=once),
        out_shape=jax.ShapeDtypeStruct((m, d), F32),
        scratch_shapes=[pltpu.VMEM((FFN_TILES_PAD, FFN_HALO, FFN_TF), F32)],
        compiler_params=_cparams(2, VMEM_LIMIT_LARGE),
        name="ffn_ln",
    )(h_bf, w_up_tiles, conv_params, w_down_pad, h1, g, b)


def kernel(x, w_in, ssm_log_step, ssm_lambda_re, ssm_lambda_im, ssm_b_re, ssm_b_im, ssm_c_re, ssm_c_im, ssm_d, ssm_w_glu, ssm_b_glu, att_lambda_q1, att_lambda_k1, att_lambda_q2, att_lambda_k2, att_subln_g, rel_bias, w_out, ln1_g, ln1_b, ffn_w_up, ffn_conv_w, ffn_conv_b, ffn_w_down, ln2_g, ln2_b):
    bsz, seq, _ = x.shape
    assert bsz == 1 and DEPTH == 1
    l = 0
    xs = x[0]

    qk_scale = ATT_QK_DIM ** -0.5 * LOG2_E
    col_scale = np.ones((1, w_in.shape[2]), np.float32)
    col_scale[:, SSM_WIDTH:SSM_WIDTH + QK_WIDTH] = qk_scale
    w_in_bf = (w_in[l] * jnp.asarray(col_scale)).astype(BF16)
    x_bf = xs.astype(BF16)

    u = _matmul(x_bf, w_in_bf, F32, 1024, 1024, 1024, 0, SSM_WIDTH)
    qkv = _matmul(x_bf, w_in_bf, BF16, 1024, 1024, 1024, SSM_WIDTH, w_in.shape[2] - SSM_WIDTH)

    lam, b_blocks, c_blocks = _ssm_parameter_layout(
        ssm_log_step[l], ssm_lambda_re[l], ssm_lambda_im[l], ssm_b_re[l], ssm_b_im[l],
        ssm_c_re[l], ssm_c_im[l])
    dskip = ssm_d[l].astype(F32).reshape(N_LANE_TILES, 1, LANES)
    y = _ssm_mixer(u, lam, b_blocks, c_blocks, dskip)
    y_ssm = _glu(y, ssm_w_glu[l].astype(BF16), ssm_b_glu[l].astype(F32).reshape(1, SSM_WIDTH), 512)

    vec = lambda a: a.astype(F32).reshape(1, ATT_QK_DIM)
    y_att = _diff_attention(qkv, rel_bias,
                            vec(att_lambda_q1[l]), vec(att_lambda_k1[l]),
                            vec(att_lambda_q2[l]), vec(att_lambda_k2[l]),
                            att_subln_g[l].astype(F32).reshape(1, ATT_V_DIM))

    h1, h1_bf = _outproj_ln(y_ssm, y_att, w_out[l].astype(BF16), xs,
                            ln1_g[l].reshape(1, D_MODEL), ln1_b[l].reshape(1, D_MODEL), 512, 512)

    pad = FFN_TILES_PAD * FFN_TF - D_FF
    conv_params = jnp.pad(jnp.concatenate([ffn_conv_w[l].astype(F32), ffn_conv_b[l].astype(F32)[None]], axis=0),
                          ((0, 0), (0, pad)))
    out = _ffn_ln(h1_bf, _cast_up_tiles(ffn_w_up[l]), conv_params, _cast_down(ffn_w_down[l]), h1,
                  ln2_g[l].reshape(1, D_MODEL), ln2_b[l].reshape(1, D_MODEL), 512)
    return out[None]
```

```python
import functools
import math

import numpy as np
import jax
import jax.numpy as jnp
from jax import lax
from jax.experimental import pallas as pl
from jax.experimental.pallas import tpu as pltpu

F32 = jnp.float32
BF16 = jnp.bfloat16

D_MODEL = 4096
CHUNK = 64
SSM_WIDTH = 2048
SSM_GROUP = 16
SSM_GROUPS = SSM_WIDTH // SSM_GROUP
SSM_STATE = 64
ATT_QK_DIM = 128
ATT_V_DIM = 256
ATT_WIDTH = 2048
ATT_HEADS = 8
QK_WIDTH = 2048
D_FF = 11008
REL_BUCKETS = 32
REL_MAX_DIST = 128
DEPTH = 1
ALPHA = (2 * DEPTH) ** 0.25
LN_EPS = 1e-5
NEG_INF = -1e30
LAMBDA_INIT = 0.8 - 0.6 * math.exp(-0.3 * 0)

LANES = 128
SSM_L = 16
GROUPS_PER_TILE = LANES // SSM_GROUP
N_LANE_TILES = SSM_WIDTH // LANES
STATE_LANES = GROUPS_PER_TILE * SSM_STATE
VMEM_LIMIT = 56 * 1024 * 1024
VMEM_LIMIT_LARGE = 62 * 1024 * 1024


def _cparams(n_axes, vmem=VMEM_LIMIT):
    return pltpu.CompilerParams(dimension_semantics=("arbitrary",) * n_axes,
                                vmem_limit_bytes=vmem)


def _matmul_kernel(x_ref, w_ref, o_ref):
    o_ref[...] = jnp.dot(x_ref[...], w_ref[...], preferred_element_type=F32).astype(o_ref.dtype)


def _matmul(x, w, out_dtype, tm, tn, col_start=0, n_cols=None):
    m, kdim = x.shape
    n = w.shape[1] - col_start if n_cols is None else n_cols
    tm, tn = min(tm, m), min(tn, n)
    assert m % tm == 0 and n % tn == 0 and col_start % tn == 0
    j0 = col_start // tn
    return pl.pallas_call(
        _matmul_kernel,
        grid=(m // tm, n // tn),
        in_specs=[pl.BlockSpec((tm, kdim), lambda i, j: (i, 0)),
                  pl.BlockSpec((kdim, tn), lambda i, j: (0, j + j0))],
        out_specs=pl.BlockSpec((tm, tn), lambda i, j: (i, j)),
        out_shape=jax.ShapeDtypeStruct((m, n), out_dtype),
        compiler_params=_cparams(2),
        name="matmul",
    )(x, w)


def _gelu_tanh(x):
    c = math.sqrt(2.0 / math.pi)
    return 0.5 * x * (1.0 + jnp.tanh(c * (x + 0.044715 * (x * x * x))))


def _ssm_kernel(*refs):
    u_refs = refs[:SSM_L]
    lam_ref, b_ref, c_ref, dskip_ref, y_ref, urev_ref, e_ref, hin_ref, p_ref, qt_ref, d_ref = refs[SSM_L:]
    n_chunks = u_refs[0].shape[0]
    contract_last = (((1,), (1,)), ((), ()))

    lr = lam_ref[0:1, :]
    li = lam_ref[1:2, :]
    step = lam_ref[2:3, :]
    mag = jnp.exp(lr * step)
    a_re = mag * jnp.cos(li * step)
    a_im = mag * jnp.sin(li * step)
    den = lr * lr + li * li
    z_re = ((a_re - 1.0) * lr + a_im * li) / den
    z_im = (a_im * lr - (a_re - 1.0) * li) / den

    b_re, b_im = b_ref[0], b_ref[1]
    c_re, c_im = c_ref[0], c_ref[1]
    c0 = jnp.concatenate([c_re, -c_im], axis=1).astype(BF16)
    w_re, w_im = z_re, z_im
    pw_re, pw_im = a_re, a_im
    for s in range(SSM_L):
        rows = slice(s * LANES, (s + 1) * LANES)
        p_ref[rows, 0:STATE_LANES] = (w_re * b_re - w_im * b_im).astype(BF16)
        p_ref[rows, STATE_LANES:2 * STATE_LANES] = (w_re * b_im + w_im * b_re).astype(BF16)
        qt_ref[rows, 0:STATE_LANES] = (c_re * pw_re - c_im * pw_im).astype(BF16)
        qt_ref[rows, STATE_LANES:2 * STATE_LANES] = (-(c_re * pw_im + c_im * pw_re)).astype(BF16)
        w_re, w_im = w_re * a_re - w_im * a_im, w_re * a_im + w_im * a_re
        if s + 1 < SSM_L:
            pw_re, pw_im = pw_re * a_re - pw_im * a_im, pw_re * a_im + pw_im * a_re
    d_ref[...] = lax.dot_general(p_ref[...], c0, contract_last, preferred_element_type=F32).astype(BF16)

    for i in range(SSM_L):
        urev_ref[:, (SSM_L - 1 - i) * LANES:(SSM_L - i) * LANES] = u_refs[i][...].astype(BF16)

    e_ref[...] = jnp.dot(urev_ref[...], p_ref[...], preferred_element_type=F32)

    def chunk_step(c, carry):
        h_re, h_im = carry
        hin_ref[pl.ds(c, 1), 0:STATE_LANES] = h_re
        hin_ref[pl.ds(c, 1), STATE_LANES:2 * STATE_LANES] = h_im
        e_re = e_ref[pl.ds(c, 1), 0:STATE_LANES]
        e_im = e_ref[pl.ds(c, 1), STATE_LANES:2 * STATE_LANES]
        return (pw_re * h_re - pw_im * h_im + e_re,
                pw_re * h_im + pw_im * h_re + e_im)

    zero = jnp.zeros((1, STATE_LANES), F32)
    lax.fori_loop(0, n_chunks, chunk_step, (zero, zero))

    carry_in = lax.dot_general(hin_ref[...].astype(BF16), qt_ref[...], contract_last,
                               preferred_element_type=F32)

    dskip = dskip_ref[...]
    for i in range(SSM_L):
        intra = jnp.dot(urev_ref[:, (SSM_L - 1 - i) * LANES:], d_ref[0:(i + 1) * LANES, :],
                        preferred_element_type=F32)
        y = intra + carry_in[:, i * LANES:(i + 1) * LANES] + dskip * u_refs[i][...]
        y_ref[i] = _gelu_tanh(y)


def _ssm_mixer(u, lam, b_blocks, c_blocks, dskip):
    seq = u.shape[0]
    n_chunks = seq // SSM_L
    u_chunks = u.reshape(n_chunks, SSM_L * SSM_WIDTH)
    step_rows = [pl.BlockSpec((n_chunks, LANES), functools.partial(lambda j, i: (0, i * N_LANE_TILES + j), i=i))
                 for i in range(SSM_L)]
    return pl.pallas_call(
        _ssm_kernel,
        grid=(N_LANE_TILES,),
        in_specs=step_rows + [
                  pl.BlockSpec((None, 3, STATE_LANES), lambda j: (j, 0, 0)),
                  pl.BlockSpec((None, 2, LANES, STATE_LANES), lambda j: (j, 0, 0, 0)),
                  pl.BlockSpec((None, 2, LANES, STATE_LANES), lambda j: (j, 0, 0, 0)),
                  pl.BlockSpec((None, 1, LANES), lambda j: (j, 0, 0))],
        out_specs=pl.BlockSpec((SSM_L, n_chunks, LANES), lambda j: (0, 0, j)),
        out_shape=jax.ShapeDtypeStruct((SSM_L, n_chunks, SSM_WIDTH), F32),
        scratch_shapes=[pltpu.VMEM((n_chunks, SSM_L * LANES), BF16),
                        pltpu.VMEM((n_chunks, 2 * STATE_LANES), F32),
                        pltpu.VMEM((n_chunks, 2 * STATE_LANES), F32),
                        pltpu.VMEM((SSM_L * LANES, 2 * STATE_LANES), BF16),
                        pltpu.VMEM((SSM_L * LANES, 2 * STATE_LANES), BF16),
                        pltpu.VMEM((SSM_L * LANES, LANES), BF16)],
        compiler_params=_cparams(1),
        name="ssm_mixer",
    )(*([u_chunks] * SSM_L), lam, b_blocks, c_blocks, dskip)


def _ssm_parameter_layout(log_step, lam_re, lam_im, b_re, b_im, c_re, c_im):
    nt, gl = N_LANE_TILES, GROUPS_PER_TILE
    step = jnp.broadcast_to(jnp.exp(log_step.astype(F32))[:, None], lam_re.shape)
    lam = jnp.stack([lam_re.astype(F32).reshape(nt, STATE_LANES),
                     lam_im.astype(F32).reshape(nt, STATE_LANES),
                     step.reshape(nt, STATE_LANES)], axis=1)
    eye = jnp.eye(gl, dtype=F32)

    def blocks(x):
        x5 = x.astype(F32).reshape(nt, gl, SSM_GROUP, SSM_STATE)
        return (x5[:, :, :, None, :] * eye[None, :, None, :, None]).reshape(nt, LANES, STATE_LANES)

    b_blocks = jnp.stack([blocks(b_re.transpose(0, 2, 1)), blocks(b_im.transpose(0, 2, 1))], axis=1)
    c_blocks = jnp.stack([blocks(c_re), blocks(c_im)], axis=1)
    return lam, b_blocks, c_blocks


def _glu_kernel(y_ref, w_ref, b_ref, o_ref):
    y = y_ref[...]
    z = jnp.dot(y.astype(BF16), w_ref[...], preferred_element_type=F32) + b_ref[...]
    o_ref[...] = (y * jax.nn.sigmoid(z)).astype(o_ref.dtype)


def _glu(y_steps, w, b):
    n_steps, n_chunks, n = y_steps.shape
    out = pl.pallas_call(
        _glu_kernel,
        grid=(n_steps,),
        in_specs=[pl.BlockSpec((None, n_chunks, n), lambda i: (i, 0, 0)),
                  pl.BlockSpec((n, n), lambda i: (0, 0)),
                  pl.BlockSpec((1, n), lambda i: (0, 0))],
        out_specs=pl.BlockSpec((n_chunks, n), lambda i: (0, i)),
        out_shape=jax.ShapeDtypeStruct((n_chunks, n_steps * n), BF16),
        compiler_params=_cparams(1),
        name="glu",
    )(y_steps, w, b)
    return out.reshape(n_chunks * n_steps, n)


ATT_TQ = 512
LOG2_E = math.log2(math.e)
FAR_BUCKET = REL_BUCKETS // 2 - 1


def _attn_kernel(rel_ref, q_ref, k_ref, v_ref, bucket_ref, lq1_ref, lk1_ref, lq2_ref, lk2_ref, g_ref,
                 o_ref, m_ref, l_ref, acc_ref, bias_ref, s0_ref, s1_ref):
    h = pl.program_id(0)
    i = pl.program_id(1)
    tq = ATT_TQ

    @pl.when(i == 0)
    def _():
        far = rel_ref[FAR_BUCKET, h]

        def rows(r, carry):
            r0 = pl.multiple_of(r * 8, 8)
            bucket = bucket_ref[pl.ds(r0, 8), :]
            tile = jnp.where(bucket < 0, NEG_INF, 0.0)
            for b in range(REL_BUCKETS):
                tile = jnp.where(bucket == b, (rel_ref[b, h] - far) * LOG2_E, tile)
            bias_ref[pl.ds(r0, 8), :] = tile
            return carry

        lax.fori_loop(0, tq // 8, rows, 0)

    m_ref[...] = jnp.full_like(m_ref, NEG_INF)
    l_ref[...] = jnp.zeros_like(l_ref)
    acc_ref[...] = jnp.zeros_like(acc_ref)

    def scores(block, s_ref, bias):
        start = pl.multiple_of(block * tq, tq)
        kb = k_ref[pl.ds(start, tq), :]
        for c in range(2):
            q = q_ref[:, c * ATT_QK_DIM:(c + 1) * ATT_QK_DIM]
            kc = kb[:, c * ATT_QK_DIM:(c + 1) * ATT_QK_DIM]
            s = lax.dot_general(q, kc, (((1,), (1,)), ((), ())), preferred_element_type=F32)
            s_ref[c] = s if bias is None else s + bias

    def absorb(block, s_ref):
        start = pl.multiple_of(block * tq, tq)
        vb = v_ref[pl.ds(start, tq), :]
        for c in range(2):
            tiles = [s_ref[c, :, t * LANES:(t + 1) * LANES] for t in range(tq // LANES)]
            m_tile = functools.reduce(jnp.maximum, tiles)
            m_prev = m_ref[c]
            m_new = jnp.maximum(m_prev, jnp.max(m_tile, axis=1, keepdims=True))
            scale = jnp.exp2(m_prev - m_new)
            ps = [jnp.exp2(t - m_new) for t in tiles]
            l_ref[c] = scale * l_ref[c] + functools.reduce(jnp.add, ps)
            m_ref[c] = m_new
            pv = jnp.dot(jnp.concatenate(ps, axis=1).astype(BF16), vb, preferred_element_type=F32)
            for t in range(ATT_V_DIM // LANES):
                cols = slice(t * LANES, (t + 1) * LANES)
                acc_ref[c, :, cols] = scale * acc_ref[c, :, cols] + pv[:, cols]

    scores(i, s0_ref, bias_ref[:, tq:2 * tq])

    @pl.when(i == 0)
    def _():
        absorb(i, s0_ref)

    @pl.when(i >= 1)
    def _():
        absorb(i, s0_ref)
        scores(i - 1, s1_ref, bias_ref[:, 0:tq])
        n_far = i - 1
        n_pairs = lax.shift_right_logical(n_far, 1)

        def far_pair(p, carry):
            absorb(jnp.where(p == 0, i - 1, 2 * p - 1), s1_ref)
            scores(2 * p, s0_ref, None)
            absorb(2 * p, s0_ref)
            scores(2 * p + 1, s1_ref, None)
            return carry

        lax.fori_loop(0, n_pairs, far_pair, 0)
        in_s1 = jnp.where(n_pairs == 0, i - 1, 2 * n_pairs - 1)

        @pl.when(lax.rem(n_far, 2) == 1)
        def _():
            absorb(in_s1, s1_ref)
            scores(n_far - 1, s0_ref, None)
            absorb(n_far - 1, s0_ref)

        @pl.when(lax.rem(n_far, 2) == 0)
        def _():
            absorb(in_s1, s1_ref)

    lam = (jnp.exp(jnp.sum(lq1_ref[...] * lk1_ref[...], axis=1, keepdims=True))
           - jnp.exp(jnp.sum(lq2_ref[...] * lk2_ref[...], axis=1, keepdims=True)) + LAMBDA_INIT)
    l0 = jnp.sum(l_ref[0], axis=1, keepdims=True)
    l1 = jnp.sum(l_ref[1], axis=1, keepdims=True)
    o = acc_ref[0] / l0 - lam * (acc_ref[1] / l1)
    o = o * lax.rsqrt(jnp.mean(o * o, axis=1, keepdims=True) + LN_EPS) * g_ref[...]
    o_ref[...] = (o * (1.0 - LAMBDA_INIT)).astype(o_ref.dtype)


def _t5_bucket(rel):
    half = REL_BUCKETS // 2
    max_exact = half // 2
    ret = jnp.where(rel > 0, half, 0)
    n = jnp.abs(rel)
    nf = jnp.maximum(n, 1).astype(jnp.float32)
    large = max_exact + (jnp.log(nf / max_exact) / math.log(REL_MAX_DIST / max_exact)
                         * (half - max_exact)).astype(jnp.int32)
    large = jnp.minimum(large, half - 1)
    return ret + jnp.where(n < max_exact, n, large)


def _near_buckets():
    tq = ATT_TQ
    qpos = np.arange(tq)[:, None]
    kpos = np.arange(-tq, tq)[None, :]
    rel = jnp.asarray(kpos - qpos, jnp.int32)
    visible = jnp.asarray((kpos // CHUNK) <= (qpos // CHUNK))
    return jnp.where(visible, _t5_bucket(rel), -1).astype(jnp.int32)


def _diff_attention(qkv, rel_bias, lq1, lk1, lq2, lk2, subln_g):
    seq = qkv.shape[0]
    tq = ATT_TQ
    head_blk = 2 * ATT_QK_DIM
    vec = pl.BlockSpec((1, ATT_QK_DIM), lambda h, i: (0, 0))
    return pl.pallas_call(
        _attn_kernel,
        grid=(ATT_HEADS, seq // tq),
        in_specs=[pl.BlockSpec(memory_space=pltpu.SMEM),
                  pl.BlockSpec((tq, head_blk), lambda h, i: (i, h)),
                  pl.BlockSpec((seq, head_blk), lambda h, i: (0, ATT_HEADS + h)),
                  pl.BlockSpec((seq, ATT_V_DIM), lambda h, i: (0, 2 * ATT_HEADS + h)),
                  pl.BlockSpec((tq, 2 * tq), lambda h, i: (0, 0)),
                  vec, vec, vec, vec,
                  pl.BlockSpec((1, ATT_V_DIM), lambda h, i: (0, 0))],
        out_specs=pl.BlockSpec((tq, ATT_V_DIM), lambda h, i: (i, h)),
        out_shape=jax.ShapeDtypeStruct((seq, ATT_WIDTH), BF16),
        scratch_shapes=[pltpu.VMEM((2, tq, LANES), F32),
                        pltpu.VMEM((2, tq, LANES), F32),
                        pltpu.VMEM((2, tq, ATT_V_DIM), F32),
                        pltpu.VMEM((tq, 2 * tq), F32),
                        pltpu.VMEM((2, tq, tq), F32),
                        pltpu.VMEM((2, tq, tq), F32)],
        compiler_params=_cparams(2),
        name="diff_attention",
    )(rel_bias.astype(F32), qkv, qkv, qkv, _near_buckets(), lq1, lk1, lq2, lk2, subln_g)


MM_COL_CHUNK = 1024
LN_ROW_CHUNK = 128


def _residual_layer_norm(res_ref, acc_ref, g_ref, b_ref, out_refs):
    g = g_ref[...]
    b = b_ref[...]
    for r0 in range(0, acc_ref.shape[0], LN_ROW_CHUNK):
        rows = slice(r0, r0 + LN_ROW_CHUNK)
        r = ALPHA * res_ref[rows, :] + acc_ref[rows, :]
        mu = jnp.mean(r, axis=1, keepdims=True)
        xc = r - mu
        var = jnp.mean(xc * xc, axis=1, keepdims=True)
        h = xc * lax.rsqrt(var + LN_EPS) * g + b
        for o_ref in out_refs:
            o_ref[rows, :] = h.astype(o_ref.dtype)


def _outproj_kernel(ys_ref, ya_ref, wt_ref, wb_ref, x_ref, g_ref, b_ref, h_ref, hb_ref):
    k = pl.program_id(1)

    @pl.when(k == 0)
    def _():
        h_ref[...] = jnp.zeros_like(h_ref)

    for c in range(0, h_ref.shape[1], MM_COL_CHUNK):
        cols = slice(c, c + MM_COL_CHUNK)
        h_ref[:, cols] += (jnp.dot(ys_ref[...], wt_ref[:, cols], preferred_element_type=F32)
                           + jnp.dot(ya_ref[...], wb_ref[:, cols], preferred_element_type=F32))

    @pl.when(k == pl.num_programs(1) - 1)
    def _():
        _residual_layer_norm(x_ref, h_ref, g_ref, b_ref, (h_ref, hb_ref))


def _outproj_ln(ys, ya, w_out, x, g, b, tm, tk):
    m, half = ys.shape
    d = w_out.shape[1]
    tm = min(tm, m)
    nk = half // tk
    row = pl.BlockSpec((1, d), lambda i, k: (0, 0))
    return pl.pallas_call(
        _outproj_kernel,
        grid=(m // tm, nk),
        in_specs=[pl.BlockSpec((tm, tk), lambda i, k: (i, k)),
                  pl.BlockSpec((tm, tk), lambda i, k: (i, k)),
                  pl.BlockSpec((tk, d), lambda i, k: (k, 0)),
                  pl.BlockSpec((tk, d), lambda i, k: (k + nk, 0)),
                  pl.BlockSpec((tm, d), lambda i, k: (i, 0), pipeline_mode=pl.Buffered(1)),
                  row, row],
        out_specs=[pl.BlockSpec((tm, d), lambda i, k: (i, 0)),
                   pl.BlockSpec((tm, d), lambda i, k: (i, 0))],
        out_shape=[jax.ShapeDtypeStruct((m, d), F32), jax.ShapeDtypeStruct((m, d), BF16)],
        compiler_params=_cparams(2, VMEM_LIMIT_LARGE),
        name="outproj_ln",
    )(ys, ya, w_out, w_out, x, g, b)


FFN_TF = 256
FFN_TILES = D_FF // FFN_TF
FFN_TILES_PER_STEP = 2
FFN_STEPS = -(-FFN_TILES // FFN_TILES_PER_STEP)
FFN_TILES_PAD = FFN_STEPS * FFN_TILES_PER_STEP
FFN_HALO = 8


def _cast_up_tiles_kernel(a_ref, g_ref, o_ref):
    t = pl.program_id(0)

    @pl.when(t < FFN_TILES)
    def _():
        o_ref[:, 0:FFN_TF] = a_ref[...].astype(BF16)
        o_ref[:, FFN_TF:2 * FFN_TF] = g_ref[...].astype(BF16)

    @pl.when(t >= FFN_TILES)
    def _():
        o_ref[...] = jnp.zeros_like(o_ref)


def _cast_up_tiles(w_up):
    d = w_up.shape[0]
    last = FFN_TILES - 1
    return pl.pallas_call(
        _cast_up_tiles_kernel,
        grid=(FFN_TILES_PAD,),
        in_specs=[pl.BlockSpec((d, FFN_TF), lambda t: (0, jnp.minimum(t, last))),
                  pl.BlockSpec((d, FFN_TF), lambda t: (0, FFN_TILES + jnp.minimum(t, last)))],
        out_specs=pl.BlockSpec((None, d, 2 * FFN_TF), lambda t: (t, 0, 0)),
        out_shape=jax.ShapeDtypeStruct((FFN_TILES_PAD, d, 2 * FFN_TF), BF16),
        compiler_params=_cparams(1),
        name="cast_w_up",
    )(w_up, w_up)


def _cast_down_kernel(w_ref, o_ref):
    t = pl.program_id(0)

    @pl.when(t < FFN_TILES)
    def _():
        o_ref[...] = w_ref[...].astype(BF16)

    @pl.when(t >= FFN_TILES)
    def _():
        o_ref[...] = jnp.zeros_like(o_ref)


def _cast_down(w_down):
    d = w_down.shape[1]
    return pl.pallas_call(
        _cast_down_kernel,
        grid=(FFN_TILES_PAD,),
        in_specs=[pl.BlockSpec((FFN_TF, d), lambda t: (jnp.minimum(t, FFN_TILES - 1), 0))],
        out_specs=pl.BlockSpec((FFN_TF, d), lambda t: (t, 0)),
        out_shape=jax.ShapeDtypeStruct((FFN_TILES_PAD * FFN_TF, d), BF16),
        compiler_params=_cparams(1),
        name="cast_w_down",
    )(w_down)


def _ffn_kernel(h_ref, wu_ref, cp_ref, wd_ref, r_ref, lg_ref, lb_ref, o_ref, halo_ref):
    n = FFN_TILES_PER_STEP
    i = pl.program_id(0)
    f = pl.program_id(1)
    tm = h_ref.shape[0]

    @pl.when(f == 0)
    def _():
        o_ref[...] = jnp.zeros_like(o_ref)

    h = h_ref[...]
    row = lax.broadcasted_iota(jnp.int32, (tm, FFN_TF), 0)
    acts = []
    for k in range(n):
        tile = f * n + k
        cols = slice(k * FFN_TF, (k + 1) * FFN_TF)
        value = jnp.dot(h, wu_ref[k, :, 0:FFN_TF], preferred_element_type=F32)
        gate = jnp.dot(h, wu_ref[k, :, FFN_TF:2 * FFN_TF], preferred_element_type=F32)
        halo = jnp.where(i == 0, 0.0, halo_ref[tile])
        halo_ref[tile] = gate[tm - FFN_HALO:tm, :]
        prev1 = jnp.where(row == 0, halo[FFN_HALO - 1:FFN_HALO, :], pltpu.roll(gate, 1, 0))
        prev2 = jnp.where(row == 0, halo[FFN_HALO - 2:FFN_HALO - 1, :],
                          jnp.where(row == 1, halo[FFN_HALO - 1:FFN_HALO, :], pltpu.roll(gate, 2, 0)))
        gc = (cp_ref[3:4, cols] + prev2 * cp_ref[0:1, cols] + prev1 * cp_ref[1:2, cols]
              + gate * cp_ref[2:3, cols])
        acts.append(((gc * jax.nn.sigmoid(gc)) * value).astype(BF16))

    for c in range(0, o_ref.shape[1], MM_COL_CHUNK):
        cols = slice(c, c + MM_COL_CHUNK)
        contrib = jnp.dot(acts[0], wd_ref[0:FFN_TF, cols], preferred_element_type=F32)
        for k in range(1, n):
            contrib += jnp.dot(acts[k], wd_ref[k * FFN_TF:(k + 1) * FFN_TF, cols],
                               preferred_element_type=F32)
        o_ref[:, cols] += contrib

    @pl.when(f == pl.num_programs(1) - 1)
    def _():
        _residual_layer_norm(r_ref, o_ref, lg_ref, lb_ref, (o_ref,))


def _ffn_ln(h_bf, w_up_tiles, conv_params, w_down_pad, h1, g, b, tm):
    m, d = h_bf.shape
    tm = min(tm, m)
    n = FFN_TILES_PER_STEP
    once = pl.Buffered(1)
    row = pl.BlockSpec((1, d), lambda i, f: (0, 0))
    return pl.pallas_call(
        _ffn_kernel,
        grid=(m // tm, FFN_STEPS),
        in_specs=[pl.BlockSpec((tm, d), lambda i, f: (i, 0), pipeline_mode=once),
                  pl.BlockSpec((n, d, 2 * FFN_TF), lambda i, f: (f, 0, 0)),
                  pl.BlockSpec((4, n * FFN_TF), lambda i, f: (0, f)),
                  pl.BlockSpec((n * FFN_TF, d), lambda i, f: (f, 0)),
                  pl.BlockSpec((tm, d), lambda i, f: (i, 0), pipeline_mode=once),
                  row, row],
        out_specs=pl.BlockSpec((tm, d), lambda i, f: (i, 0), pipeline_mode=once),
        out_shape=jax.ShapeDtypeStruct((m, d), F32),
        scratch_shapes=[pltpu.VMEM((FFN_TILES_PAD, FFN_HALO, FFN_TF), F32)],
        compiler_params=_cparams(2, VMEM_LIMIT_LARGE),
        name="ffn_ln",
    )(h_bf, w_up_tiles, conv_params, w_down_pad, h1, g, b)


def kernel(x, w_in, ssm_log_step, ssm_lambda_re, ssm_lambda_im, ssm_b_re, ssm_b_im, ssm_c_re, ssm_c_im, ssm_d, ssm_w_glu, ssm_b_glu, att_lambda_q1, att_lambda_k1, att_lambda_q2, att_lambda_k2, att_subln_g, rel_bias, w_out, ln1_g, ln1_b, ffn_w_up, ffn_conv_w, ffn_conv_b, ffn_w_down, ln2_g, ln2_b):
    bsz, seq, _ = x.shape
    assert bsz == 1 and DEPTH == 1
    l = 0
    xs = x[0]

    qk_scale = ATT_QK_DIM ** -0.5 * LOG2_E
    col_scale = np.ones((1, w_in.shape[2]), np.float32)
    col_scale[:, SSM_WIDTH:SSM_WIDTH + QK_WIDTH] = qk_scale
    w_in_bf = (w_in[l] * jnp.asarray(col_scale)).astype(BF16)
    x_bf = xs.astype(BF16)

    u = _matmul(x_bf, w_in_bf, F32, 1024, 1024, 0, SSM_WIDTH)
    qkv = _matmul(x_bf, w_in_bf, BF16, 1024, 1024, SSM_WIDTH, w_in.shape[2] - SSM_WIDTH)

    lam, b_blocks, c_blocks = _ssm_parameter_layout(
        ssm_log_step[l], ssm_lambda_re[l], ssm_lambda_im[l], ssm_b_re[l], ssm_b_im[l],
        ssm_c_re[l], ssm_c_im[l])
    dskip = ssm_d[l].astype(F32).reshape(N_LANE_TILES, 1, LANES)
    y = _ssm_mixer(u, lam, b_blocks, c_blocks, dskip)
    y_ssm = _glu(y, ssm_w_glu[l].astype(BF16), ssm_b_glu[l].astype(F32).reshape(1, SSM_WIDTH))

    vec = lambda a: a.astype(F32).reshape(1, ATT_QK_DIM)
    y_att = _diff_attention(qkv, rel_bias,
                            vec(att_lambda_q1[l]), vec(att_lambda_k1[l]),
                            vec(att_lambda_q2[l]), vec(att_lambda_k2[l]),
                            att_subln_g[l].astype(F32).reshape(1, ATT_V_DIM))

    h1, h1_bf = _outproj_ln(y_ssm, y_att, w_out[l].astype(BF16), xs,
                            ln1_g[l].reshape(1, D_MODEL), ln1_b[l].reshape(1, D_MODEL), 512, 512)

    pad = FFN_TILES_PAD * FFN_TF - D_FF
    conv_params = jnp.pad(jnp.concatenate([ffn_conv_w[l].astype(F32), ffn_conv_b[l].astype(F32)[None]], axis=0),
                          ((0, 0), (0, pad)))
    out = _ffn_ln(h1_bf, _cast_up_tiles(ffn_w_up[l]), conv_params, _cast_down(ffn_w_down[l]), h1,
                  ln2_g[l].reshape(1, D_MODEL), ln2_b[l].reshape(1, D_MODEL), 512)
    return out[None]
```

```python
import functools
import math

import numpy as np
import jax
import jax.numpy as jnp
from jax import lax
from jax.experimental import pallas as pl
from jax.experimental.pallas import tpu as pltpu

F32 = jnp.float32
BF16 = jnp.bfloat16

D_MODEL = 4096
CHUNK = 64
SSM_WIDTH = 2048
SSM_GROUP = 16
SSM_GROUPS = SSM_WIDTH // SSM_GROUP
SSM_STATE = 64
ATT_QK_DIM = 128
ATT_V_DIM = 256
ATT_WIDTH = 2048
ATT_HEADS = 8
QK_WIDTH = 2048
D_FF = 11008
REL_BUCKETS = 32
REL_MAX_DIST = 128
DEPTH = 1
ALPHA = (2 * DEPTH) ** 0.25
LN_EPS = 1e-5
NEG_INF = -1e30
LAMBDA_INIT = 0.8 - 0.6 * math.exp(-0.3 * 0)

LANES = 128
SSM_L = 16
GROUPS_PER_TILE = LANES // SSM_GROUP
N_LANE_TILES = SSM_WIDTH // LANES
STATE_LANES = GROUPS_PER_TILE * SSM_STATE
VMEM_LIMIT = 56 * 1024 * 1024
VMEM_LIMIT_LARGE = 62 * 1024 * 1024


def _cparams(n_axes, vmem=VMEM_LIMIT):
    return pltpu.CompilerParams(dimension_semantics=("arbitrary",) * n_axes,
                                vmem_limit_bytes=vmem)


def _matmul_kernel(x_ref, w_ref, o_ref):
    o_ref[...] = jnp.dot(x_ref[...], w_ref[...], preferred_element_type=F32).astype(o_ref.dtype)


def _matmul(x, w, out_dtype, tm, tn, col_start=0, n_cols=None):
    m, kdim = x.shape
    n = w.shape[1] - col_start if n_cols is None else n_cols
    tm, tn = min(tm, m), min(tn, n)
    assert m % tm == 0 and n % tn == 0 and col_start % tn == 0
    j0 = col_start // tn
    return pl.pallas_call(
        _matmul_kernel,
        grid=(m // tm, n // tn),
        in_specs=[pl.BlockSpec((tm, kdim), lambda i, j: (i, 0)),
                  pl.BlockSpec((kdim, tn), lambda i, j: (0, j + j0))],
        out_specs=pl.BlockSpec((tm, tn), lambda i, j: (i, j)),
        out_shape=jax.ShapeDtypeStruct((m, n), out_dtype),
        compiler_params=_cparams(2),
        name="matmul",
    )(x, w)


def _gelu_tanh(x):
    c = math.sqrt(2.0 / math.pi)
    return 0.5 * x * (1.0 + jnp.tanh(c * (x + 0.044715 * (x * x * x))))


def _ssm_kernel(u_ref, lam_ref, b_ref, c_ref, dskip_ref, y_ref,
                urev_ref, e_ref, hin_ref, p_ref, qt_ref, d_ref):
    n_chunks = u_ref.shape[0] // SSM_L
    contract_last = (((1,), (1,)), ((), ()))

    lr = lam_ref[0:1, :]
    li = lam_ref[1:2, :]
    step = lam_ref[2:3, :]
    mag = jnp.exp(lr * step)
    a_re = mag * jnp.cos(li * step)
    a_im = mag * jnp.sin(li * step)
    den = lr * lr + li * li
    z_re = ((a_re - 1.0) * lr + a_im * li) / den
    z_im = (a_im * lr - (a_re - 1.0) * li) / den

    b_re, b_im = b_ref[0], b_ref[1]
    c_re, c_im = c_ref[0], c_ref[1]
    c0 = jnp.concatenate([c_re, -c_im], axis=1).astype(BF16)
    w_re, w_im = z_re, z_im
    pw_re, pw_im = a_re, a_im
    for s in range(SSM_L):
        rows = slice(s * LANES, (s + 1) * LANES)
        p_ref[rows, 0:STATE_LANES] = (w_re * b_re - w_im * b_im).astype(BF16)
        p_ref[rows, STATE_LANES:2 * STATE_LANES] = (w_re * b_im + w_im * b_re).astype(BF16)
        qt_ref[rows, 0:STATE_LANES] = (c_re * pw_re - c_im * pw_im).astype(BF16)
        qt_ref[rows, STATE_LANES:2 * STATE_LANES] = (-(c_re * pw_im + c_im * pw_re)).astype(BF16)
        w_re, w_im = w_re * a_re - w_im * a_im, w_re * a_im + w_im * a_re
        if s + 1 < SSM_L:
            pw_re, pw_im = pw_re * a_re - pw_im * a_im, pw_re * a_im + pw_im * a_re
    d_ref[...] = lax.dot_general(p_ref[...], c0, contract_last, preferred_element_type=F32).astype(BF16)

    for i in range(SSM_L):
        rows = u_ref[pl.ds(i, n_chunks, stride=SSM_L), :]
        urev_ref[:, (SSM_L - 1 - i) * LANES:(SSM_L - i) * LANES] = rows.astype(BF16)

    e_ref[...] = jnp.dot(urev_ref[...], p_ref[...], preferred_element_type=F32)

    def chunk_step(c, carry):
        h_re, h_im = carry
        hin_ref[pl.ds(c, 1), 0:STATE_LANES] = h_re
        hin_ref[pl.ds(c, 1), STATE_LANES:2 * STATE_LANES] = h_im
        e_re = e_ref[pl.ds(c, 1), 0:STATE_LANES]
        e_im = e_ref[pl.ds(c, 1), STATE_LANES:2 * STATE_LANES]
        return (pw_re * h_re - pw_im * h_im + e_re,
                pw_re * h_im + pw_im * h_re + e_im)

    zero = jnp.zeros((1, STATE_LANES), F32)
    lax.fori_loop(0, n_chunks, chunk_step, (zero, zero))

    carry_in = lax.dot_general(hin_ref[...].astype(BF16), qt_ref[...], contract_last,
                               preferred_element_type=F32)

    dskip = dskip_ref[...]
    for i in range(SSM_L):
        intra = jnp.dot(urev_ref[:, (SSM_L - 1 - i) * LANES:], d_ref[0:(i + 1) * LANES, :],
                        preferred_element_type=F32)
        u_i = u_ref[pl.ds(i, n_chunks, stride=SSM_L), :]
        y = intra + carry_in[:, i * LANES:(i + 1) * LANES] + dskip * u_i
        y_ref[pl.ds(i, n_chunks, stride=SSM_L), :] = _gelu_tanh(y)


def _ssm_mixer(u, lam, b_blocks, c_blocks, dskip):
    seq = u.shape[0]
    n_chunks = seq // SSM_L
    return pl.pallas_call(
        _ssm_kernel,
        grid=(N_LANE_TILES,),
        in_specs=[pl.BlockSpec((seq, LANES), lambda j: (0, j)),
                  pl.BlockSpec((None, 3, STATE_LANES), lambda j: (j, 0, 0)),
                  pl.BlockSpec((None, 2, LANES, STATE_LANES), lambda j: (j, 0, 0, 0)),
                  pl.BlockSpec((None, 2, LANES, STATE_LANES), lambda j: (j, 0, 0, 0)),
                  pl.BlockSpec((None, 1, LANES), lambda j: (j, 0, 0))],
        out_specs=pl.BlockSpec((seq, LANES), lambda j: (0, j)),
        out_shape=jax.ShapeDtypeStruct((seq, SSM_WIDTH), F32),
        scratch_shapes=[pltpu.VMEM((n_chunks, SSM_L * LANES), BF16),
                        pltpu.VMEM((n_chunks, 2 * STATE_LANES), F32),
                        pltpu.VMEM((n_chunks, 2 * STATE_LANES), F32),
                        pltpu.VMEM((SSM_L * LANES, 2 * STATE_LANES), BF16),
                        pltpu.VMEM((SSM_L * LANES, 2 * STATE_LANES), BF16),
                        pltpu.VMEM((SSM_L * LANES, LANES), BF16)],
        compiler_params=_cparams(1),
        name="ssm_mixer",
    )(u, lam, b_blocks, c_blocks, dskip)


def _ssm_parameter_layout(log_step, lam_re, lam_im, b_re, b_im, c_re, c_im):
    nt, gl = N_LANE_TILES, GROUPS_PER_TILE
    step = jnp.broadcast_to(jnp.exp(log_step.astype(F32))[:, None], lam_re.shape)
    lam = jnp.stack([lam_re.astype(F32).reshape(nt, STATE_LANES),
                     lam_im.astype(F32).reshape(nt, STATE_LANES),
                     step.reshape(nt, STATE_LANES)], axis=1)
    eye = jnp.eye(gl, dtype=F32)

    def blocks(x):
        x5 = x.astype(F32).reshape(nt, gl, SSM_GROUP, SSM_STATE)
        return (x5[:, :, :, None, :] * eye[None, :, None, :, None]).reshape(nt, LANES, STATE_LANES)

    b_blocks = jnp.stack([blocks(b_re.transpose(0, 2, 1)), blocks(b_im.transpose(0, 2, 1))], axis=1)
    c_blocks = jnp.stack([blocks(c_re), blocks(c_im)], axis=1)
    return lam, b_blocks, c_blocks


def _glu_kernel(y_ref, w_ref, b_ref, o_ref):
    y = y_ref[...]
    z = jnp.dot(y.astype(BF16), w_ref[...], preferred_element_type=F32) + b_ref[...]
    o_ref[...] = (y * jax.nn.sigmoid(z)).astype(o_ref.dtype)


def _glu(y, w, b, tm):
    m, n = y.shape
    tm = min(tm, m)
    return pl.pallas_call(
        _glu_kernel,
        grid=(m // tm,),
        in_specs=[pl.BlockSpec((tm, n), lambda i: (i, 0)),
                  pl.BlockSpec((n, n), lambda i: (0, 0)),
                  pl.BlockSpec((1, n), lambda i: (0, 0))],
        out_specs=pl.BlockSpec((tm, n), lambda i: (i, 0)),
        out_shape=jax.ShapeDtypeStruct((m, n), BF16),
        compiler_params=_cparams(1),
        name="glu",
    )(y, w, b)


ATT_TQ = 512
LOG2_E = math.log2(math.e)
FAR_BUCKET = REL_BUCKETS // 2 - 1


def _attn_kernel(rel_ref, q_ref, k_ref, v_ref, bucket_ref, lq1_ref, lk1_ref, lq2_ref, lk2_ref, g_ref,
                 o_ref, m_ref, l_ref, acc_ref, bias_ref, s0_ref, s1_ref):
    h = pl.program_id(0)
    i = pl.program_id(1)
    tq = ATT_TQ

    @pl.when(i == 0)
    def _():
        far = rel_ref[FAR_BUCKET, h]

        def rows(r, carry):
            r0 = pl.multiple_of(r * 8, 8)
            bucket = bucket_ref[pl.ds(r0, 8), :]
            tile = jnp.where(bucket < 0, NEG_INF, 0.0)
            for b in range(REL_BUCKETS):
                tile = jnp.where(bucket == b, (rel_ref[b, h] - far) * LOG2_E, tile)
            bias_ref[pl.ds(r0, 8), :] = tile
            return carry

        lax.fori_loop(0, tq // 8, rows, 0)

    m_ref[...] = jnp.full_like(m_ref, NEG_INF)
    l_ref[...] = jnp.zeros_like(l_ref)
    acc_ref[...] = jnp.zeros_like(acc_ref)

    def scores(block, s_ref, bias):
        start = pl.multiple_of(block * tq, tq)
        kb = k_ref[pl.ds(start, tq), :]
        for c in range(2):
            q = q_ref[:, c * ATT_QK_DIM:(c + 1) * ATT_QK_DIM]
            kc = kb[:, c * ATT_QK_DIM:(c + 1) * ATT_QK_DIM]
            s = lax.dot_general(q, kc, (((1,), (1,)), ((), ())), preferred_element_type=F32)
            s_ref[c] = s if bias is None else s + bias

    def absorb(block, s_ref):
        start = pl.multiple_of(block * tq, tq)
        vb = v_ref[pl.ds(start, tq), :]
        for c in range(2):
            tiles = [s_ref[c, :, t * LANES:(t + 1) * LANES] for t in range(tq // LANES)]
            m_tile = functools.reduce(jnp.maximum, tiles)
            m_prev = m_ref[c]
            m_new = jnp.maximum(m_prev, jnp.max(m_tile, axis=1, keepdims=True))
            scale = jnp.exp2(m_prev - m_new)
            ps = [jnp.exp2(t - m_new) for t in tiles]
            l_ref[c] = scale * l_ref[c] + functools.reduce(jnp.add, ps)
            m_ref[c] = m_new
            pv = jnp.dot(jnp.concatenate(ps, axis=1).astype(BF16), vb, preferred_element_type=F32)
            for t in range(ATT_V_DIM // LANES):
                cols = slice(t * LANES, (t + 1) * LANES)
                acc_ref[c, :, cols] = scale * acc_ref[c, :, cols] + pv[:, cols]

    scores(i, s0_ref, bias_ref[:, tq:2 * tq])

    @pl.when(i == 0)
    def _():
        absorb(i, s0_ref)

    @pl.when(i >= 1)
    def _():
        absorb(i, s0_ref)
        scores(i - 1, s1_ref, bias_ref[:, 0:tq])
        n_far = i - 1
        n_pairs = lax.shift_right_logical(n_far, 1)

        def far_pair(p, carry):
            absorb(jnp.where(p == 0, i - 1, 2 * p - 1), s1_ref)
            scores(2 * p, s0_ref, None)
            absorb(2 * p, s0_ref)
            scores(2 * p + 1, s1_ref, None)
            return carry

        lax.fori_loop(0, n_pairs, far_pair, 0)
        in_s1 = jnp.where(n_pairs == 0, i - 1, 2 * n_pairs - 1)

        @pl.when(lax.rem(n_far, 2) == 1)
        def _():
            absorb(in_s1, s1_ref)
            scores(n_far - 1, s0_ref, None)
            absorb(n_far - 1, s0_ref)

        @pl.when(lax.rem(n_far, 2) == 0)
        def _():
            absorb(in_s1, s1_ref)

    lam = (jnp.exp(jnp.sum(lq1_ref[...] * lk1_ref[...], axis=1, keepdims=True))
           - jnp.exp(jnp.sum(lq2_ref[...] * lk2_ref[...], axis=1, keepdims=True)) + LAMBDA_INIT)
    l0 = jnp.sum(l_ref[0], axis=1, keepdims=True)
    l1 = jnp.sum(l_ref[1], axis=1, keepdims=True)
    o = acc_ref[0] / l0 - lam * (acc_ref[1] / l1)
    o = o * lax.rsqrt(jnp.mean(o * o, axis=1, keepdims=True) + LN_EPS) * g_ref[...]
    o_ref[...] = (o * (1.0 - LAMBDA_INIT)).astype(o_ref.dtype)


def _t5_bucket(rel):
    half = REL_BUCKETS // 2
    max_exact = half // 2
    ret = jnp.where(rel > 0, half, 0)
    n = jnp.abs(rel)
    nf = jnp.maximum(n, 1).astype(jnp.float32)
    large = max_exact + (jnp.log(nf / max_exact) / math.log(REL_MAX_DIST / max_exact)
                         * (half - max_exact)).astype(jnp.int32)
    large = jnp.minimum(large, half - 1)
    return ret + jnp.where(n < max_exact, n, large)


def _near_buckets():
    tq = ATT_TQ
    qpos = np.arange(tq)[:, None]
    kpos = np.arange(-tq, tq)[None, :]
    rel = jnp.asarray(kpos - qpos, jnp.int32)
    visible = jnp.asarray((kpos // CHUNK) <= (qpos // CHUNK))
    return jnp.where(visible, _t5_bucket(rel), -1).astype(jnp.int32)


def _diff_attention(qkv, rel_bias, lq1, lk1, lq2, lk2, subln_g):
    seq = qkv.shape[0]
    tq = ATT_TQ
    head_blk = 2 * ATT_QK_DIM
    vec = pl.BlockSpec((1, ATT_QK_DIM), lambda h, i: (0, 0))
    return pl.pallas_call(
        _attn_kernel,
        grid=(ATT_HEADS, seq // tq),
        in_specs=[pl.BlockSpec(memory_space=pltpu.SMEM),
                  pl.BlockSpec((tq, head_blk), lambda h, i: (i, h)),
                  pl.BlockSpec((seq, head_blk), lambda h, i: (0, ATT_HEADS + h)),
                  pl.BlockSpec((seq, ATT_V_DIM), lambda h, i: (0, 2 * ATT_HEADS + h)),
                  pl.BlockSpec((tq, 2 * tq), lambda h, i: (0, 0)),
                  vec, vec, vec, vec,
                  pl.BlockSpec((1, ATT_V_DIM), lambda h, i: (0, 0))],
        out_specs=pl.BlockSpec((tq, ATT_V_DIM), lambda h, i: (i, h)),
        out_shape=jax.ShapeDtypeStruct((seq, ATT_WIDTH), BF16),
        scratch_shapes=[pltpu.VMEM((2, tq, LANES), F32),
                        pltpu.VMEM((2, tq, LANES), F32),
                        pltpu.VMEM((2, tq, ATT_V_DIM), F32),
                        pltpu.VMEM((tq, 2 * tq), F32),
                        pltpu.VMEM((2, tq, tq), F32),
                        pltpu.VMEM((2, tq, tq), F32)],
        compiler_params=_cparams(2),
        name="diff_attention",
    )(rel_bias.astype(F32), qkv, qkv, qkv, _near_buckets(), lq1, lk1, lq2, lk2, subln_g)


MM_COL_CHUNK = 1024
LN_ROW_CHUNK = 128


def _residual_layer_norm(res_ref, acc_ref, g_ref, b_ref, out_refs):
    g = g_ref[...]
    b = b_ref[...]
    for r0 in range(0, acc_ref.shape[0], LN_ROW_CHUNK):
        rows = slice(r0, r0 + LN_ROW_CHUNK)
        r = ALPHA * res_ref[rows, :] + acc_ref[rows, :]
        mu = jnp.mean(r, axis=1, keepdims=True)
        xc = r - mu
        var = jnp.mean(xc * xc, axis=1, keepdims=True)
        h = xc * lax.rsqrt(var + LN_EPS) * g + b
        for o_ref in out_refs:
            o_ref[rows, :] = h.astype(o_ref.dtype)


def _outproj_kernel(ys_ref, ya_ref, wt_ref, wb_ref, x_ref, g_ref, b_ref, h_ref, hb_ref):
    k = pl.program_id(1)

    @pl.when(k == 0)
    def _():
        h_ref[...] = jnp.zeros_like(h_ref)

    for c in range(0, h_ref.shape[1], MM_COL_CHUNK):
        cols = slice(c, c + MM_COL_CHUNK)
        h_ref[:, cols] += (jnp.dot(ys_ref[...], wt_ref[:, cols], preferred_element_type=F32)
                           + jnp.dot(ya_ref[...], wb_ref[:, cols], preferred_element_type=F32))

    @pl.when(k == pl.num_programs(1) - 1)
    def _():
        _residual_layer_norm(x_ref, h_ref, g_ref, b_ref, (h_ref, hb_ref))


def _outproj_ln(ys, ya, w_out, x, g, b, tm, tk):
    m, half = ys.shape
    d = w_out.shape[1]
    tm = min(tm, m)
    nk = half // tk
    row = pl.BlockSpec((1, d), lambda i, k: (0, 0))
    return pl.pallas_call(
        _outproj_kernel,
        grid=(m // tm, nk),
        in_specs=[pl.BlockSpec((tm, tk), lambda i, k: (i, k)),
                  pl.BlockSpec((tm, tk), lambda i, k: (i, k)),
                  pl.BlockSpec((tk, d), lambda i, k: (k, 0)),
                  pl.BlockSpec((tk, d), lambda i, k: (k + nk, 0)),
                  pl.BlockSpec((tm, d), lambda i, k: (i, 0), pipeline_mode=pl.Buffered(1)),
                  row, row],
        out_specs=[pl.BlockSpec((tm, d), lambda i, k: (i, 0)),
                   pl.BlockSpec((tm, d), lambda i, k: (i, 0))],
        out_shape=[jax.ShapeDtypeStruct((m, d), F32), jax.ShapeDtypeStruct((m, d), BF16)],
        compiler_params=_cparams(2, VMEM_LIMIT_LARGE),
        name="outproj_ln",
    )(ys, ya, w_out, w_out, x, g, b)


FFN_TF = 256
FFN_TILES = D_FF // FFN_TF
FFN_TILES_PER_STEP = 2
FFN_STEPS = -(-FFN_TILES // FFN_TILES_PER_STEP)
FFN_TILES_PAD = FFN_STEPS * FFN_TILES_PER_STEP
FFN_HALO = 8


def _cast_up_tiles_kernel(a_ref, g_ref, o_ref):
    t = pl.program_id(0)

    @pl.when(t < FFN_TILES)
    def _():
        o_ref[:, 0:FFN_TF] = a_ref[...].astype(BF16)
        o_ref[:, FFN_TF:2 * FFN_TF] = g_ref[...].astype(BF16)

    @pl.when(t >= FFN_TILES)
    def _():
        o_ref[...] = jnp.zeros_like(o_ref)


def _cast_up_tiles(w_up):
    d = w_up.shape[0]
    last = FFN_TILES - 1
    return pl.pallas_call(
        _cast_up_tiles_kernel,
        grid=(FFN_TILES_PAD,),
        in_specs=[pl.BlockSpec((d, FFN_TF), lambda t: (0, jnp.minimum(t, last))),
                  pl.BlockSpec((d, FFN_TF), lambda t: (0, FFN_TILES + jnp.minimum(t, last)))],
        out_specs=pl.BlockSpec((None, d, 2 * FFN_TF), lambda t: (t, 0, 0)),
        out_shape=jax.ShapeDtypeStruct((FFN_TILES_PAD, d, 2 * FFN_TF), BF16),
        compiler_params=_cparams(1),
        name="cast_w_up",
    )(w_up, w_up)


def _cast_down_kernel(w_ref, o_ref):
    t = pl.program_id(0)

    @pl.when(t < FFN_TILES)
    def _():
        o_ref[...] = w_ref[...].astype(BF16)

    @pl.when(t >= FFN_TILES)
    def _():
        o_ref[...] = jnp.zeros_like(o_ref)


def _cast_down(w_down):
    d = w_down.shape[1]
    return pl.pallas_call(
        _cast_down_kernel,
        grid=(FFN_TILES_PAD,),
        in_specs=[pl.BlockSpec((FFN_TF, d), lambda t: (jnp.minimum(t, FFN_TILES - 1), 0))],
        out_specs=pl.BlockSpec((FFN_TF, d), lambda t: (t, 0)),
        out_shape=jax.ShapeDtypeStruct((FFN_TILES_PAD * FFN_TF, d), BF16),
        compiler_params=_cparams(1),
        name="cast_w_down",
    )(w_down)


def _ffn_kernel(h_ref, wu_ref, cp_ref, wd_ref, r_ref, lg_ref, lb_ref, o_ref, halo_ref):
    n = FFN_TILES_PER_STEP
    i = pl.program_id(0)
    f = pl.program_id(1)
    tm = h_ref.shape[0]

    @pl.when(f == 0)
    def _():
        o_ref[...] = jnp.zeros_like(o_ref)

    h = h_ref[...]
    row = lax.broadcasted_iota(jnp.int32, (tm, FFN_TF), 0)
    acts = []
    for k in range(n):
        tile = f * n + k
        cols = slice(k * FFN_TF, (k + 1) * FFN_TF)
        value = jnp.dot(h, wu_ref[k, :, 0:FFN_TF], preferred_element_type=F32)
        gate = jnp.dot(h, wu_ref[k, :, FFN_TF:2 * FFN_TF], preferred_element_type=F32)
        halo = jnp.where(i == 0, 0.0, halo_ref[tile])
        halo_ref[tile] = gate[tm - FFN_HALO:tm, :]
        prev1 = jnp.where(row == 0, halo[FFN_HALO - 1:FFN_HALO, :], pltpu.roll(gate, 1, 0))
        prev2 = jnp.where(row == 0, halo[FFN_HALO - 2:FFN_HALO - 1, :],
                          jnp.where(row == 1, halo[FFN_HALO - 1:FFN_HALO, :], pltpu.roll(gate, 2, 0)))
        gc = (cp_ref[3:4, cols] + prev2 * cp_ref[0:1, cols] + prev1 * cp_ref[1:2, cols]
              + gate * cp_ref[2:3, cols])
        acts.append(((gc * jax.nn.sigmoid(gc)) * value).astype(BF16))

    for c in range(0, o_ref.shape[1], MM_COL_CHUNK):
        cols = slice(c, c + MM_COL_CHUNK)
        contrib = jnp.dot(acts[0], wd_ref[0:FFN_TF, cols], preferred_element_type=F32)
        for k in range(1, n):
            contrib += jnp.dot(acts[k], wd_ref[k * FFN_TF:(k + 1) * FFN_TF, cols],
                               preferred_element_type=F32)
        o_ref[:, cols] += contrib

    @pl.when(f == pl.num_programs(1) - 1)
    def _():
        _residual_layer_norm(r_ref, o_ref, lg_ref, lb_ref, (o_ref,))


def _ffn_ln(h_bf, w_up_tiles, conv_params, w_down_pad, h1, g, b, tm):
    m, d = h_bf.shape
    tm = min(tm, m)
    n = FFN_TILES_PER_STEP
    once = pl.Buffered(1)
    row = pl.BlockSpec((1, d), lambda i, f: (0, 0))
    return pl.pallas_call(
        _ffn_kernel,
        grid=(m // tm, FFN_STEPS),
        in_specs=[pl.BlockSpec((tm, d), lambda i, f: (i, 0), pipeline_mode=once),
                  pl.BlockSpec((n, d, 2 * FFN_TF), lambda i, f: (f, 0, 0)),
                  pl.BlockSpec((4, n * FFN_TF), lambda i, f: (0, f)),
                  pl.BlockSpec((n * FFN_TF, d), lambda i, f: (f, 0)),
                  pl.BlockSpec((tm, d), lambda i, f: (i, 0), pipeline_mode=once),
                  row, row],
        out_specs=pl.BlockSpec((tm, d), lambda i, f: (i, 0), pipeline_mode=once),
        out_shape=jax.ShapeDtypeStruct((m, d), F32),
        scratch_shapes=[pltpu.VMEM((FFN_TILES_PAD, FFN_HALO, FFN_TF), F32)],
        compiler_params=_cparams(2, VMEM_LIMIT_LARGE),
        name="ffn_ln",
    )(h_bf, w_up_tiles, conv_params, w_down_pad, h1, g, b)


def kernel(x, w_in, ssm_log_step, ssm_lambda_re, ssm_lambda_im, ssm_b_re, ssm_b_im, ssm_c_re, ssm_c_im, ssm_d, ssm_w_glu, ssm_b_glu, att_lambda_q1, att_lambda_k1, att_lambda_q2, att_lambda_k2, att_subln_g, rel_bias, w_out, ln1_g, ln1_b, ffn_w_up, ffn_conv_w, ffn_conv_b, ffn_w_down, ln2_g, ln2_b):
    bsz, seq, _ = x.shape
    assert bsz == 1 and DEPTH == 1
    l = 0
    xs = x[0]

    qk_scale = ATT_QK_DIM ** -0.5 * LOG2_E
    col_scale = np.ones((1, w_in.shape[2]), np.float32)
    col_scale[:, SSM_WIDTH:SSM_WIDTH + QK_WIDTH] = qk_scale
    w_in_bf = (w_in[l] * jnp.asarray(col_scale)).astype(BF16)
    x_bf = xs.astype(BF16)

    u = _matmul(x_bf, w_in_bf, F32, 1024, 1024, 0, SSM_WIDTH)
    qkv = _matmul(x_bf, w_in_bf, BF16, 1024, 1024, SSM_WIDTH, w_in.shape[2] - SSM_WIDTH)

    lam, b_blocks, c_blocks = _ssm_parameter_layout(
        ssm_log_step[l], ssm_lambda_re[l], ssm_lambda_im[l], ssm_b_re[l], ssm_b_im[l],
        ssm_c_re[l], ssm_c_im[l])
    dskip = ssm_d[l].astype(F32).reshape(N_LANE_TILES, 1, LANES)
    y = _ssm_mixer(u, lam, b_blocks, c_blocks, dskip)
    y_ssm = _glu(y, ssm_w_glu[l].astype(BF16), ssm_b_glu[l].astype(F32).reshape(1, SSM_WIDTH), 512)

    vec = lambda a: a.astype(F32).reshape(1, ATT_QK_DIM)
    y_att = _diff_attention(qkv, rel_bias,
                            vec(att_lambda_q1[l]), vec(att_lambda_k1[l]),
                            vec(att_lambda_q2[l]), vec(att_lambda_k2[l]),
                            att_subln_g[l].astype(F32).reshape(1, ATT_V_DIM))

    h1, h1_bf = _outproj_ln(y_ssm, y_att, w_out[l].astype(BF16), xs,
                            ln1_g[l].reshape(1, D_MODEL), ln1_b[l].reshape(1, D_MODEL), 512, 512)

    pad = FFN_TILES_PAD * FFN_TF - D_FF
    conv_params = jnp.pad(jnp.concatenate([ffn_conv_w[l].astype(F32), ffn_conv_b[l].astype(F32)[None]], axis=0),
                          ((0, 0), (0, pad)))
    out = _ffn_ln(h1_bf, _cast_up_tiles(ffn_w_up[l]), conv_params, _cast_down(ffn_w_down[l]), h1,
                  ln2_g[l].reshape(1, D_MODEL), ln2_b[l].reshape(1, D_MODEL), 512)
    return out[None]
```

```python
import functools
import math

import numpy as np
import jax
import jax.numpy as jnp
from jax import lax
from jax.experimental import pallas as pl
from jax.experimental.pallas import tpu as pltpu

F32 = jnp.float32
BF16 = jnp.bfloat16

D_MODEL = 4096
CHUNK = 64
SSM_WIDTH = 2048
SSM_GROUP = 16
SSM_GROUPS = SSM_WIDTH // SSM_GROUP
SSM_STATE = 64
ATT_QK_DIM = 128
ATT_V_DIM = 256
ATT_WIDTH = 2048
ATT_HEADS = 8
QK_WIDTH = 2048
D_FF = 11008
REL_BUCKETS = 32
REL_MAX_DIST = 128
DEPTH = 1
ALPHA = (2 * DEPTH) ** 0.25
LN_EPS = 1e-5
NEG_INF = -1e30
LAMBDA_INIT = 0.8 - 0.6 * math.exp(-0.3 * 0)

LANES = 128
SSM_L = 16
GROUPS_PER_TILE = LANES // SSM_GROUP
N_LANE_TILES = SSM_WIDTH // LANES
STATE_LANES = GROUPS_PER_TILE * SSM_STATE
VMEM_LIMIT = 56 * 1024 * 1024
VMEM_LIMIT_LARGE = 62 * 1024 * 1024


def _cparams(n_axes, vmem=VMEM_LIMIT):
    return pltpu.CompilerParams(dimension_semantics=("arbitrary",) * n_axes,
                                vmem_limit_bytes=vmem)


def _matmul_kernel(x_ref, w_ref, o_ref):
    o_ref[...] = jnp.dot(x_ref[...], w_ref[...], preferred_element_type=F32).astype(o_ref.dtype)


def _matmul(x, w, out_dtype, tm, tn, col_start=0, n_cols=None):
    m, kdim = x.shape
    n = w.shape[1] - col_start if n_cols is None else n_cols
    tm, tn = min(tm, m), min(tn, n)
    assert m % tm == 0 and n % tn == 0 and col_start % tn == 0
    j0 = col_start // tn
    return pl.pallas_call(
        _matmul_kernel,
        grid=(m // tm, n // tn),
        in_specs=[pl.BlockSpec((tm, kdim), lambda i, j: (i, 0)),
                  pl.BlockSpec((kdim, tn), lambda i, j: (0, j + j0))],
        out_specs=pl.BlockSpec((tm, tn), lambda i, j: (i, j)),
        out_shape=jax.ShapeDtypeStruct((m, n), out_dtype),
        compiler_params=_cparams(2),
        name="matmul",
    )(x, w)


def _gelu_tanh(x):
    c = math.sqrt(2.0 / math.pi)
    return 0.5 * x * (1.0 + jnp.tanh(c * (x + 0.044715 * (x * x * x))))


def _ssm_kernel(u_ref, lam_ref, b_ref, c_ref, dskip_ref, y_ref,
                urev_ref, e_ref, hin_ref, p_ref, qt_ref, d_ref):
    n_chunks = u_ref.shape[0] // SSM_L
    contract_last = (((1,), (1,)), ((), ()))

    lr = lam_ref[0:1, :]
    li = lam_ref[1:2, :]
    step = lam_ref[2:3, :]
    mag = jnp.exp(lr * step)
    a_re = mag * jnp.cos(li * step)
    a_im = mag * jnp.sin(li * step)
    den = lr * lr + li * li
    z_re = ((a_re - 1.0) * lr + a_im * li) / den
    z_im = (a_im * lr - (a_re - 1.0) * li) / den

    b_re, b_im = b_ref[0], b_ref[1]
    c_re, c_im = c_ref[0], c_ref[1]
    c0 = jnp.concatenate([c_re, -c_im], axis=1).astype(BF16)
    w_re, w_im = z_re, z_im
    pw_re, pw_im = a_re, a_im
    for s in range(SSM_L):
        rows = slice(s * LANES, (s + 1) * LANES)
        p_ref[rows, 0:STATE_LANES] = (w_re * b_re - w_im * b_im).astype(BF16)
        p_ref[rows, STATE_LANES:2 * STATE_LANES] = (w_re * b_im + w_im * b_re).astype(BF16)
        qt_ref[rows, 0:STATE_LANES] = (c_re * pw_re - c_im * pw_im).astype(BF16)
        qt_ref[rows, STATE_LANES:2 * STATE_LANES] = (-(c_re * pw_im + c_im * pw_re)).astype(BF16)
        w_re, w_im = w_re * a_re - w_im * a_im, w_re * a_im + w_im * a_re
        if s + 1 < SSM_L:
            pw_re, pw_im = pw_re * a_re - pw_im * a_im, pw_re * a_im + pw_im * a_re
    d_ref[...] = lax.dot_general(p_ref[...], c0, contract_last, preferred_element_type=F32).astype(BF16)

    for i in range(SSM_L):
        rows = u_ref[pl.ds(i, n_chunks, stride=SSM_L), :]
        urev_ref[:, (SSM_L - 1 - i) * LANES:(SSM_L - i) * LANES] = rows.astype(BF16)

    e_ref[...] = jnp.dot(urev_ref[...], p_ref[...], preferred_element_type=F32)

    def chunk_step(c, carry):
        h_re, h_im = carry
        hin_ref[pl.ds(c, 1), 0:STATE_LANES] = h_re
        hin_ref[pl.ds(c, 1), STATE_LANES:2 * STATE_LANES] = h_im
        e_re = e_ref[pl.ds(c, 1), 0:STATE_LANES]
        e_im = e_ref[pl.ds(c, 1), STATE_LANES:2 * STATE_LANES]
        return (pw_re * h_re - pw_im * h_im + e_re,
                pw_re * h_im + pw_im * h_re + e_im)

    zero = jnp.zeros((1, STATE_LANES), F32)
    lax.fori_loop(0, n_chunks, chunk_step, (zero, zero))

    carry_in = lax.dot_general(hin_ref[...].astype(BF16), qt_ref[...], contract_last,
                               preferred_element_type=F32)

    dskip = dskip_ref[...]
    for i in range(SSM_L):
        intra = jnp.dot(urev_ref[:, (SSM_L - 1 - i) * LANES:], d_ref[0:(i + 1) * LANES, :],
                        preferred_element_type=F32)
        u_i = u_ref[pl.ds(i, n_chunks, stride=SSM_L), :]
        y = intra + carry_in[:, i * LANES:(i + 1) * LANES] + dskip * u_i
        y_ref[pl.ds(i, n_chunks, stride=SSM_L), :] = _gelu_tanh(y)


def _ssm_mixer(u, lam, b_blocks, c_blocks, dskip):
    seq = u.shape[0]
    n_chunks = seq // SSM_L
    return pl.pallas_call(
        _ssm_kernel,
        grid=(N_LANE_TILES,),
        in_specs=[pl.BlockSpec((seq, LANES), lambda j: (0, j)),
                  pl.BlockSpec((None, 3, STATE_LANES), lambda j: (j, 0, 0)),
                  pl.BlockSpec((None, 2, LANES, STATE_LANES), lambda j: (j, 0, 0, 0)),
                  pl.BlockSpec((None, 2, LANES, STATE_LANES), lambda j: (j, 0, 0, 0)),
                  pl.BlockSpec((None, 1, LANES), lambda j: (j, 0, 0))],
        out_specs=pl.BlockSpec((seq, LANES), lambda j: (0, j)),
        out_shape=jax.ShapeDtypeStruct((seq, SSM_WIDTH), F32),
        scratch_shapes=[pltpu.VMEM((n_chunks, SSM_L * LANES), BF16),
                        pltpu.VMEM((n_chunks, 2 * STATE_LANES), F32),
                        pltpu.VMEM((n_chunks, 2 * STATE_LANES), F32),
                        pltpu.VMEM((SSM_L * LANES, 2 * STATE_LANES), BF16),
                        pltpu.VMEM((SSM_L * LANES, 2 * STATE_LANES), BF16),
                        pltpu.VMEM((SSM_L * LANES, LANES), BF16)],
        compiler_params=_cparams(1),
        name="ssm_mixer",
    )(u, lam, b_blocks, c_blocks, dskip)


def _ssm_parameter_layout(log_step, lam_re, lam_im, b_re, b_im, c_re, c_im):
    nt, gl = N_LANE_TILES, GROUPS_PER_TILE
    step = jnp.broadcast_to(jnp.exp(log_step.astype(F32))[:, None], lam_re.shape)
    lam = jnp.stack([lam_re.astype(F32).reshape(nt, STATE_LANES),
                     lam_im.astype(F32).reshape(nt, STATE_LANES),
                     step.reshape(nt, STATE_LANES)], axis=1)
    eye = jnp.eye(gl, dtype=F32)

    def blocks(x):
        x5 = x.astype(F32).reshape(nt, gl, SSM_GROUP, SSM_STATE)
        return (x5[:, :, :, None, :] * eye[None, :, None, :, None]).reshape(nt, LANES, STATE_LANES)

    b_blocks = jnp.stack([blocks(b_re.transpose(0, 2, 1)), blocks(b_im.transpose(0, 2, 1))], axis=1)
    c_blocks = jnp.stack([blocks(c_re), blocks(c_im)], axis=1)
    return lam, b_blocks, c_blocks


def _glu_kernel(y_ref, w_ref, b_ref, o_ref):
    y = y_ref[...]
    z = jnp.dot(y.astype(BF16), w_ref[...], preferred_element_type=F32) + b_ref[...]
    o_ref[...] = (y * jax.nn.sigmoid(z)).astype(o_ref.dtype)


def _glu(y, w, b, tm):
    m, n = y.shape
    tm = min(tm, m)
    return pl.pallas_call(
        _glu_kernel,
        grid=(m // tm,),
        in_specs=[pl.BlockSpec((tm, n), lambda i: (i, 0)),
                  pl.BlockSpec((n, n), lambda i: (0, 0)),
                  pl.BlockSpec((1, n), lambda i: (0, 0))],
        out_specs=pl.BlockSpec((tm, n), lambda i: (i, 0)),
        out_shape=jax.ShapeDtypeStruct((m, n), BF16),
        compiler_params=_cparams(1),
        name="glu",
    )(y, w, b)


ATT_TQ = 512
LOG2_E = math.log2(math.e)
FAR_BUCKET = REL_BUCKETS // 2 - 1


def _attn_kernel(rel_ref, q_ref, k_ref, v_ref, bucket_ref, lq1_ref, lk1_ref, lq2_ref, lk2_ref, g_ref,
                 o_ref, m_ref, l_ref, acc_ref, bias_ref, sa_ref, sb_ref):
    h = pl.program_id(0)
    i = pl.program_id(1)
    tq = ATT_TQ

    @pl.when(i == 0)
    def _():
        far = rel_ref[FAR_BUCKET, h]

        def rows(r, carry):
            r0 = pl.multiple_of(r * 8, 8)
            bucket = bucket_ref[pl.ds(r0, 8), :]
            tile = jnp.where(bucket < 0, NEG_INF, 0.0)
            for b in range(REL_BUCKETS):
                tile = jnp.where(bucket == b, (rel_ref[b, h] - far) * LOG2_E, tile)
            bias_ref[pl.ds(r0, 8), :] = tile
            return carry

        lax.fori_loop(0, tq // 8, rows, 0)

    m_ref[...] = jnp.full_like(m_ref, NEG_INF)
    l_ref[...] = jnp.zeros_like(l_ref)
    acc_ref[...] = jnp.zeros_like(acc_ref)

    def scores(start, width, s_ref, bias):
        kb = k_ref[pl.ds(start, width), :]
        for c in range(2):
            q = q_ref[:, c * ATT_QK_DIM:(c + 1) * ATT_QK_DIM]
            kc = kb[:, c * ATT_QK_DIM:(c + 1) * ATT_QK_DIM]
            s = lax.dot_general(q, kc, (((1,), (1,)), ((), ())), preferred_element_type=F32)
            s_ref[c, :, 0:width] = s if bias is None else s + bias

    def absorb(start, width, s_ref):
        vb = v_ref[pl.ds(start, width), :]
        for c in range(2):
            tiles = [s_ref[c, :, t * LANES:(t + 1) * LANES] for t in range(width // LANES)]
            m_tile = functools.reduce(jnp.maximum, tiles)
            m_prev = m_ref[c]
            m_new = jnp.maximum(m_prev, jnp.max(m_tile, axis=1, keepdims=True))
            scale = jnp.exp2(m_prev - m_new)
            ps = [jnp.exp2(t - m_new) for t in tiles]
            l_ref[c] = scale * l_ref[c] + functools.reduce(jnp.add, ps)
            m_ref[c] = m_new
            pv = jnp.dot(jnp.concatenate(ps, axis=1).astype(BF16), vb, preferred_element_type=F32)
            for t in range(ATT_V_DIM // LANES):
                cols = slice(t * LANES, (t + 1) * LANES)
                acc_ref[c, :, cols] = scale * acc_ref[c, :, cols] + pv[:, cols]

    wide = 2 * tq

    @pl.when(i == 0)
    def _():
        scores(0, tq, sa_ref, bias_ref[:, tq:wide])
        absorb(0, tq, sa_ref)

    @pl.when(i >= 1)
    def _():
        near = pl.multiple_of((i - 1) * tq, tq)
        scores(near, wide, sa_ref, bias_ref[...])
        n_far = i - 1
        n_wide = lax.shift_right_logical(n_far, 1)
        has_narrow = lax.rem(n_far, 2) == 1
        n_loop = lax.shift_right_logical(n_wide, 1)
        far = lambda p: pl.multiple_of(p * wide, wide)

        def two_wide(r, carry):
            absorb(jnp.where(r == 0, near, far(2 * r - 1)), wide, sa_ref)
            scores(far(2 * r), wide, sb_ref, None)
            absorb(far(2 * r), wide, sb_ref)
            scores(far(2 * r + 1), wide, sa_ref, None)
            return carry

        lax.fori_loop(0, n_loop, two_wide, 0)
        pending = jnp.where(n_loop == 0, near, far(2 * n_loop - 1))
        odd_wide = lax.rem(n_wide, 2) == 1
        last_wide = far(n_wide - 1)
        narrow = pl.multiple_of(n_wide * wide, tq)

        @pl.when(jnp.logical_and(odd_wide, has_narrow))
        def _():
            absorb(pending, wide, sa_ref)
            scores(last_wide, wide, sb_ref, None)
            absorb(last_wide, wide, sb_ref)
            scores(narrow, tq, sa_ref, None)
            absorb(narrow, tq, sa_ref)

        @pl.when(jnp.logical_and(odd_wide, jnp.logical_not(has_narrow)))
        def _():
            absorb(pending, wide, sa_ref)
            scores(last_wide, wide, sb_ref, None)
            absorb(last_wide, wide, sb_ref)

        @pl.when(jnp.logical_and(jnp.logical_not(odd_wide), has_narrow))
        def _():
            absorb(pending, wide, sa_ref)
            scores(narrow, tq, sb_ref, None)
            absorb(narrow, tq, sb_ref)

        @pl.when(jnp.logical_and(jnp.logical_not(odd_wide), jnp.logical_not(has_narrow)))
        def _():
            absorb(pending, wide, sa_ref)

    lam = (jnp.exp(jnp.sum(lq1_ref[...] * lk1_ref[...], axis=1, keepdims=True))
           - jnp.exp(jnp.sum(lq2_ref[...] * lk2_ref[...], axis=1, keepdims=True)) + LAMBDA_INIT)
    l0 = jnp.sum(l_ref[0], axis=1, keepdims=True)
    l1 = jnp.sum(l_ref[1], axis=1, keepdims=True)
    o = acc_ref[0] / l0 - lam * (acc_ref[1] / l1)
    o = o * lax.rsqrt(jnp.mean(o * o, axis=1, keepdims=True) + LN_EPS) * g_ref[...]
    o_ref[...] = (o * (1.0 - LAMBDA_INIT)).astype(o_ref.dtype)


def _t5_bucket(rel):
    half = REL_BUCKETS // 2
    max_exact = half // 2
    ret = jnp.where(rel > 0, half, 0)
    n = jnp.abs(rel)
    nf = jnp.maximum(n, 1).astype(jnp.float32)
    large = max_exact + (jnp.log(nf / max_exact) / math.log(REL_MAX_DIST / max_exact)
                         * (half - max_exact)).astype(jnp.int32)
    large = jnp.minimum(large, half - 1)
    return ret + jnp.where(n < max_exact, n, large)


def _near_buckets():
    tq = ATT_TQ
    qpos = np.arange(tq)[:, None]
    kpos = np.arange(-tq, tq)[None, :]
    rel = jnp.asarray(kpos - qpos, jnp.int32)
    visible = jnp.asarray((kpos // CHUNK) <= (qpos // CHUNK))
    return jnp.where(visible, _t5_bucket(rel), -1).astype(jnp.int32)


def _diff_attention(qkv, rel_bias, lq1, lk1, lq2, lk2, subln_g):
    seq = qkv.shape[0]
    tq = ATT_TQ
    head_blk = 2 * ATT_QK_DIM
    vec = pl.BlockSpec((1, ATT_QK_DIM), lambda h, i: (0, 0))
    return pl.pallas_call(
        _attn_kernel,
        grid=(ATT_HEADS, seq // tq),
        in_specs=[pl.BlockSpec(memory_space=pltpu.SMEM),
                  pl.BlockSpec((tq, head_blk), lambda h, i: (i, h)),
                  pl.BlockSpec((seq, head_blk), lambda h, i: (0, ATT_HEADS + h)),
                  pl.BlockSpec((seq, ATT_V_DIM), lambda h, i: (0, 2 * ATT_HEADS + h)),
                  pl.BlockSpec((tq, 2 * tq), lambda h, i: (0, 0)),
                  vec, vec, vec, vec,
                  pl.BlockSpec((1, ATT_V_DIM), lambda h, i: (0, 0))],
        out_specs=pl.BlockSpec((tq, ATT_V_DIM), lambda h, i: (i, h)),
        out_shape=jax.ShapeDtypeStruct((seq, ATT_WIDTH), BF16),
        scratch_shapes=[pltpu.VMEM((2, tq, LANES), F32),
                        pltpu.VMEM((2, tq, LANES), F32),
                        pltpu.VMEM((2, tq, ATT_V_DIM), F32),
                        pltpu.VMEM((tq, 2 * tq), F32),
                        pltpu.VMEM((2, tq, 2 * tq), F32),
                        pltpu.VMEM((2, tq, 2 * tq), F32)],
        compiler_params=_cparams(2),
        name="diff_attention",
    )(rel_bias.astype(F32), qkv, qkv, qkv, _near_buckets(), lq1, lk1, lq2, lk2, subln_g)


MM_COL_CHUNK = 1024
LN_ROW_CHUNK = 128


def _residual_layer_norm(res_ref, acc_ref, g_ref, b_ref, out_refs):
    g = g_ref[...]
    b = b_ref[...]
    for r0 in range(0, acc_ref.shape[0], LN_ROW_CHUNK):
        rows = slice(r0, r0 + LN_ROW_CHUNK)
        r = ALPHA * res_ref[rows, :] + acc_ref[rows, :]
        mu = jnp.mean(r, axis=1, keepdims=True)
        xc = r - mu
        var = jnp.mean(xc * xc, axis=1, keepdims=True)
        h = xc * lax.rsqrt(var + LN_EPS) * g + b
        for o_ref in out_refs:
            o_ref[rows, :] = h.astype(o_ref.dtype)


def _outproj_kernel(ys_ref, ya_ref, wt_ref, wb_ref, x_ref, g_ref, b_ref, h_ref, hb_ref):
    k = pl.program_id(1)

    @pl.when(k == 0)
    def _():
        h_ref[...] = jnp.zeros_like(h_ref)

    for c in range(0, h_ref.shape[1], MM_COL_CHUNK):
        cols = slice(c, c + MM_COL_CHUNK)
        h_ref[:, cols] += (jnp.dot(ys_ref[...], wt_ref[:, cols], preferred_element_type=F32)
                           + jnp.dot(ya_ref[...], wb_ref[:, cols], preferred_element_type=F32))

    @pl.when(k == pl.num_programs(1) - 1)
    def _():
        _residual_layer_norm(x_ref, h_ref, g_ref, b_ref, (h_ref, hb_ref))


def _outproj_ln(ys, ya, w_out, x, g, b, tm, tk):
    m, half = ys.shape
    d = w_out.shape[1]
    tm = min(tm, m)
    nk = half // tk
    row = pl.BlockSpec((1, d), lambda i, k: (0, 0))
    return pl.pallas_call(
        _outproj_kernel,
        grid=(m // tm, nk),
        in_specs=[pl.BlockSpec((tm, tk), lambda i, k: (i, k)),
                  pl.BlockSpec((tm, tk), lambda i, k: (i, k)),
                  pl.BlockSpec((tk, d), lambda i, k: (k, 0)),
                  pl.BlockSpec((tk, d), lambda i, k: (k + nk, 0)),
                  pl.BlockSpec((tm, d), lambda i, k: (i, 0), pipeline_mode=pl.Buffered(1)),
                  row, row],
        out_specs=[pl.BlockSpec((tm, d), lambda i, k: (i, 0)),
                   pl.BlockSpec((tm, d), lambda i, k: (i, 0))],
        out_shape=[jax.ShapeDtypeStruct((m, d), F32), jax.ShapeDtypeStruct((m, d), BF16)],
        compiler_params=_cparams(2, VMEM_LIMIT_LARGE),
        name="outproj_ln",
    )(ys, ya, w_out, w_out, x, g, b)


FFN_TF = 256
FFN_TILES = D_FF // FFN_TF
FFN_TILES_PER_STEP = 2
FFN_STEPS = -(-FFN_TILES // FFN_TILES_PER_STEP)
FFN_TILES_PAD = FFN_STEPS * FFN_TILES_PER_STEP
FFN_HALO = 8


def _cast_up_tiles_kernel(a_ref, g_ref, o_ref):
    t = pl.program_id(0)

    @pl.when(t < FFN_TILES)
    def _():
        o_ref[:, 0:FFN_TF] = a_ref[...].astype(BF16)
        o_ref[:, FFN_TF:2 * FFN_TF] = g_ref[...].astype(BF16)

    @pl.when(t >= FFN_TILES)
    def _():
        o_ref[...] = jnp.zeros_like(o_ref)


def _cast_up_tiles(w_up):
    d = w_up.shape[0]
    last = FFN_TILES - 1
    return pl.pallas_call(
        _cast_up_tiles_kernel,
        grid=(FFN_TILES_PAD,),
        in_specs=[pl.BlockSpec((d, FFN_TF), lambda t: (0, jnp.minimum(t, last))),
                  pl.BlockSpec((d, FFN_TF), lambda t: (0, FFN_TILES + jnp.minimum(t, last)))],
        out_specs=pl.BlockSpec((None, d, 2 * FFN_TF), lambda t: (t, 0, 0)),
        out_shape=jax.ShapeDtypeStruct((FFN_TILES_PAD, d, 2 * FFN_TF), BF16),
        compiler_params=_cparams(1),
        name="cast_w_up",
    )(w_up, w_up)


def _cast_down_kernel(w_ref, o_ref):
    t = pl.program_id(0)

    @pl.when(t < FFN_TILES)
    def _():
        o_ref[...] = w_ref[...].astype(BF16)

    @pl.when(t >= FFN_TILES)
    def _():
        o_ref[...] = jnp.zeros_like(o_ref)


def _cast_down(w_down):
    d = w_down.shape[1]
    return pl.pallas_call(
        _cast_down_kernel,
        grid=(FFN_TILES_PAD,),
        in_specs=[pl.BlockSpec((FFN_TF, d), lambda t: (jnp.minimum(t, FFN_TILES - 1), 0))],
        out_specs=pl.BlockSpec((FFN_TF, d), lambda t: (t, 0)),
        out_shape=jax.ShapeDtypeStruct((FFN_TILES_PAD * FFN_TF, d), BF16),
        compiler_params=_cparams(1),
        name="cast_w_down",
    )(w_down)


def _ffn_kernel(h_ref, wu_ref, cp_ref, wd_ref, r_ref, lg_ref, lb_ref, o_ref, halo_ref):
    n = FFN_TILES_PER_STEP
    i = pl.program_id(0)
    f = pl.program_id(1)
    tm = h_ref.shape[0]

    @pl.when(f == 0)
    def _():
        o_ref[...] = jnp.zeros_like(o_ref)

    h = h_ref[...]
    row = lax.broadcasted_iota(jnp.int32, (tm, FFN_TF), 0)
    acts = []
    for k in range(n):
        tile = f * n + k
        cols = slice(k * FFN_TF, (k + 1) * FFN_TF)
        value = jnp.dot(h, wu_ref[k, :, 0:FFN_TF], preferred_element_type=F32)
        gate = jnp.dot(h, wu_ref[k, :, FFN_TF:2 * FFN_TF], preferred_element_type=F32)
        halo = jnp.where(i == 0, 0.0, halo_ref[tile])
        halo_ref[tile] = gate[tm - FFN_HALO:tm, :]
        prev1 = jnp.where(row == 0, halo[FFN_HALO - 1:FFN_HALO, :], pltpu.roll(gate, 1, 0))
        prev2 = jnp.where(row == 0, halo[FFN_HALO - 2:FFN_HALO - 1, :],
                          jnp.where(row == 1, halo[FFN_HALO - 1:FFN_HALO, :], pltpu.roll(gate, 2, 0)))
        gc = (cp_ref[3:4, cols] + prev2 * cp_ref[0:1, cols] + prev1 * cp_ref[1:2, cols]
              + gate * cp_ref[2:3, cols])
        acts.append(((gc * jax.nn.sigmoid(gc)) * value).astype(BF16))

    for c in range(0, o_ref.shape[1], MM_COL_CHUNK):
        cols = slice(c, c + MM_COL_CHUNK)
        contrib = jnp.dot(acts[0], wd_ref[0:FFN_TF, cols], preferred_element_type=F32)
        for k in range(1, n):
            contrib += jnp.dot(acts[k], wd_ref[k * FFN_TF:(k + 1) * FFN_TF, cols],
                               preferred_element_type=F32)
        o_ref[:, cols] += contrib

    @pl.when(f == pl.num_programs(1) - 1)
    def _():
        _residual_layer_norm(r_ref, o_ref, lg_ref, lb_ref, (o_ref,))


def _ffn_ln(h_bf, w_up_tiles, conv_params, w_down_pad, h1, g, b, tm):
    m, d = h_bf.shape
    tm = min(tm, m)
    n = FFN_TILES_PER_STEP
    once = pl.Buffered(1)
    row = pl.BlockSpec((1, d), lambda i, f: (0, 0))
    return pl.pallas_call(
        _ffn_kernel,
        grid=(m // tm, FFN_STEPS),
        in_specs=[pl.BlockSpec((tm, d), lambda i, f: (i, 0), pipeline_mode=once),
                  pl.BlockSpec((n, d, 2 * FFN_TF), lambda i, f: (f, 0, 0)),
                  pl.BlockSpec((4, n * FFN_TF), lambda i, f: (0, f)),
                  pl.BlockSpec((n * FFN_TF, d), lambda i, f: (f, 0)),
                  pl.BlockSpec((tm, d), lambda i, f: (i, 0), pipeline_mode=once),
                  row, row],
        out_specs=pl.BlockSpec((tm, d), lambda i, f: (i, 0), pipeline_mode=once),
        out_shape=jax.ShapeDtypeStruct((m, d), F32),
        scratch_shapes=[pltpu.VMEM((FFN_TILES_PAD, FFN_HALO, FFN_TF), F32)],
        compiler_params=_cparams(2, VMEM_LIMIT_LARGE),
        name="ffn_ln",
    )(h_bf, w_up_tiles, conv_params, w_down_pad, h1, g, b)


def kernel(x, w_in, ssm_log_step, ssm_lambda_re, ssm_lambda_im, ssm_b_re, ssm_b_im, ssm_c_re, ssm_c_im, ssm_d, ssm_w_glu, ssm_b_glu, att_lambda_q1, att_lambda_k1, att_lambda_q2, att_lambda_k2, att_subln_g, rel_bias, w_out, ln1_g, ln1_b, ffn_w_up, ffn_conv_w, ffn_conv_b, ffn_w_down, ln2_g, ln2_b):
    bsz, seq, _ = x.shape
    assert bsz == 1 and DEPTH == 1
    l = 0
    xs = x[0]

    qk_scale = ATT_QK_DIM ** -0.5 * LOG2_E
    col_scale = np.ones((1, w_in.shape[2]), np.float32)
    col_scale[:, SSM_WIDTH:SSM_WIDTH + QK_WIDTH] = qk_scale
    w_in_bf = (w_in[l] * jnp.asarray(col_scale)).astype(BF16)
    x_bf = xs.astype(BF16)

    u = _matmul(x_bf, w_in_bf, F32, 1024, 1024, 0, SSM_WIDTH)
    qkv = _matmul(x_bf, w_in_bf, BF16, 1024, 1024, SSM_WIDTH, w_in.shape[2] - SSM_WIDTH)

    lam, b_blocks, c_blocks = _ssm_parameter_layout(
        ssm_log_step[l], ssm_lambda_re[l], ssm_lambda_im[l], ssm_b_re[l], ssm_b_im[l],
        ssm_c_re[l], ssm_c_im[l])
    dskip = ssm_d[l].astype(F32).reshape(N_LANE_TILES, 1, LANES)
    y = _ssm_mixer(u, lam, b_blocks, c_blocks, dskip)
    y_ssm = _glu(y, ssm_w_glu[l].astype(BF16), ssm_b_glu[l].astype(F32).reshape(1, SSM_WIDTH), 512)

    vec = lambda a: a.astype(F32).reshape(1, ATT_QK_DIM)
    y_att = _diff_attention(qkv, rel_bias,
                            vec(att_lambda_q1[l]), vec(att_lambda_k1[l]),
                            vec(att_lambda_q2[l]), vec(att_lambda_k2[l]),
                            att_subln_g[l].astype(F32).reshape(1, ATT_V_DIM))

    h1, h1_bf = _outproj_ln(y_ssm, y_att, w_out[l].astype(BF16), xs,
                            ln1_g[l].reshape(1, D_MODEL), ln1_b[l].reshape(1, D_MODEL), 512, 512)

    pad = FFN_TILES_PAD * FFN_TF - D_FF
    conv_params = jnp.pad(jnp.concatenate([ffn_conv_w[l].astype(F32), ffn_conv_b[l].astype(F32)[None]], axis=0),
                          ((0, 0), (0, pad)))
    out = _ffn_ln(h1_bf, _cast_up_tiles(ffn_w_up[l]), conv_params, _cast_down(ffn_w_down[l]), h1,
                  ln2_g[l].reshape(1, D_MODEL), ln2_b[l].reshape(1, D_MODEL), 512)
    return out[None]
```

```python
import functools
import math

import numpy as np
import jax
import jax.numpy as jnp
from jax import lax
from jax.experimental import pallas as pl
from jax.experimental.pallas import tpu as pltpu

F32 = jnp.float32
BF16 = jnp.bfloat16

D_MODEL = 4096
CHUNK = 64
SSM_WIDTH = 2048
SSM_GROUP = 16
SSM_GROUPS = SSM_WIDTH // SSM_GROUP
SSM_STATE = 64
ATT_QK_DIM = 128
ATT_V_DIM = 256
ATT_WIDTH = 2048
ATT_HEADS = 8
QK_WIDTH = 2048
D_FF = 11008
REL_BUCKETS = 32
REL_MAX_DIST = 128
DEPTH = 1
ALPHA = (2 * DEPTH) ** 0.25
LN_EPS = 1e-5
NEG_INF = -1e30
LAMBDA_INIT = 0.8 - 0.6 * math.exp(-0.3 * 0)

LANES = 128
SSM_L = 16
GROUPS_PER_TILE = LANES // SSM_GROUP
N_LANE_TILES = SSM_WIDTH // LANES
STATE_LANES = GROUPS_PER_TILE * SSM_STATE
VMEM_LIMIT = 56 * 1024 * 1024
VMEM_LIMIT_LARGE = 62 * 1024 * 1024


def _cparams(n_axes, vmem=VMEM_LIMIT):
    return pltpu.CompilerParams(dimension_semantics=("arbitrary",) * n_axes,
                                vmem_limit_bytes=vmem)


def _matmul_kernel(x_ref, w_ref, o_ref):
    o_ref[...] = jnp.dot(x_ref[...], w_ref[...], preferred_element_type=F32).astype(o_ref.dtype)


def _matmul(x, w, out_dtype, tm, tn, col_start=0, n_cols=None):
    m, kdim = x.shape
    n = w.shape[1] - col_start if n_cols is None else n_cols
    tm, tn = min(tm, m), min(tn, n)
    assert m % tm == 0 and n % tn == 0 and col_start % tn == 0
    j0 = col_start // tn
    return pl.pallas_call(
        _matmul_kernel,
        grid=(m // tm, n // tn),
        in_specs=[pl.BlockSpec((tm, kdim), lambda i, j: (i, 0)),
                  pl.BlockSpec((kdim, tn), lambda i, j: (0, j + j0))],
        out_specs=pl.BlockSpec((tm, tn), lambda i, j: (i, j)),
        out_shape=jax.ShapeDtypeStruct((m, n), out_dtype),
        compiler_params=_cparams(2),
        name="matmul",
    )(x, w)


def _gelu_tanh(x):
    c = math.sqrt(2.0 / math.pi)
    return 0.5 * x * (1.0 + jnp.tanh(c * (x + 0.044715 * (x * x * x))))


def _ssm_kernel(u_ref, lam_ref, b_ref, c_ref, dskip_ref, y_ref,
                urev_ref, e_ref, hin_ref, p_ref, qt_ref, d_ref):
    n_chunks = u_ref.shape[0] // SSM_L
    contract_last = (((1,), (1,)), ((), ()))

    lr = lam_ref[0:1, :]
    li = lam_ref[1:2, :]
    step = lam_ref[2:3, :]
    mag = jnp.exp(lr * step)
    a_re = mag * jnp.cos(li * step)
    a_im = mag * jnp.sin(li * step)
    den = lr * lr + li * li
    z_re = ((a_re - 1.0) * lr + a_im * li) / den
    z_im = (a_im * lr - (a_re - 1.0) * li) / den

    b_re, b_im = b_ref[0], b_ref[1]
    c_re, c_im = c_ref[0], c_ref[1]
    c0 = jnp.concatenate([c_re, -c_im], axis=1).astype(BF16)
    w_re, w_im = z_re, z_im
    pw_re, pw_im = a_re, a_im
    for s in range(SSM_L):
        rows = slice(s * LANES, (s + 1) * LANES)
        p_ref[rows, 0:STATE_LANES] = (w_re * b_re - w_im * b_im).astype(BF16)
        p_ref[rows, STATE_LANES:2 * STATE_LANES] = (w_re * b_im + w_im * b_re).astype(BF16)
        qt_ref[rows, 0:STATE_LANES] = (c_re * pw_re - c_im * pw_im).astype(BF16)
        qt_ref[rows, STATE_LANES:2 * STATE_LANES] = (-(c_re * pw_im + c_im * pw_re)).astype(BF16)
        w_re, w_im = w_re * a_re - w_im * a_im, w_re * a_im + w_im * a_re
        if s + 1 < SSM_L:
            pw_re, pw_im = pw_re * a_re - pw_im * a_im, pw_re * a_im + pw_im * a_re
    d = lax.dot_general(p_ref[...], c0, contract_last, preferred_element_type=F32).astype(BF16)
    d_ref[:, LANES:2 * LANES] = d
    d_ref[0:LANES, 0:LANES] = jnp.zeros((LANES, LANES), BF16)
    d_ref[LANES:SSM_L * LANES, 0:LANES] = d[0:(SSM_L - 1) * LANES, :]

    for i in range(SSM_L):
        rows = u_ref[pl.ds(i, n_chunks, stride=SSM_L), :]
        urev_ref[:, (SSM_L - 1 - i) * LANES:(SSM_L - i) * LANES] = rows.astype(BF16)

    e_ref[...] = jnp.dot(urev_ref[...], p_ref[...], preferred_element_type=F32)

    def chunk_step(c, carry):
        h_re, h_im = carry
        hin_ref[pl.ds(c, 1), 0:STATE_LANES] = h_re
        hin_ref[pl.ds(c, 1), STATE_LANES:2 * STATE_LANES] = h_im
        e_re = e_ref[pl.ds(c, 1), 0:STATE_LANES]
        e_im = e_ref[pl.ds(c, 1), STATE_LANES:2 * STATE_LANES]
        return (pw_re * h_re - pw_im * h_im + e_re,
                pw_re * h_im + pw_im * h_re + e_im)

    zero = jnp.zeros((1, STATE_LANES), F32)
    lax.fori_loop(0, n_chunks, chunk_step, (zero, zero))

    carry_in = lax.dot_general(hin_ref[...].astype(BF16), qt_ref[...], contract_last,
                               preferred_element_type=F32)

    dskip = dskip_ref[...]
    for i0 in range(0, SSM_L, 2):
        intra = jnp.dot(urev_ref[:, (SSM_L - 2 - i0) * LANES:], d_ref[0:(i0 + 2) * LANES, :],
                        preferred_element_type=F32)
        for i in (i0, i0 + 1):
            u_i = u_ref[pl.ds(i, n_chunks, stride=SSM_L), :]
            y = (intra[:, (i - i0) * LANES:(i - i0 + 1) * LANES] + carry_in[:, i * LANES:(i + 1) * LANES]
                 + dskip * u_i)
            y_ref[pl.ds(i, n_chunks, stride=SSM_L), :] = _gelu_tanh(y)


def _ssm_mixer(u, lam, b_blocks, c_blocks, dskip):
    seq = u.shape[0]
    n_chunks = seq // SSM_L
    return pl.pallas_call(
        _ssm_kernel,
        grid=(N_LANE_TILES,),
        in_specs=[pl.BlockSpec((seq, LANES), lambda j: (0, j)),
                  pl.BlockSpec((None, 3, STATE_LANES), lambda j: (j, 0, 0)),
                  pl.BlockSpec((None, 2, LANES, STATE_LANES), lambda j: (j, 0, 0, 0)),
                  pl.BlockSpec((None, 2, LANES, STATE_LANES), lambda j: (j, 0, 0, 0)),
                  pl.BlockSpec((None, 1, LANES), lambda j: (j, 0, 0))],
        out_specs=pl.BlockSpec((seq, LANES), lambda j: (0, j)),
        out_shape=jax.ShapeDtypeStruct((seq, SSM_WIDTH), F32),
        scratch_shapes=[pltpu.VMEM((n_chunks, SSM_L * LANES), BF16),
                        pltpu.VMEM((n_chunks, 2 * STATE_LANES), F32),
                        pltpu.VMEM((n_chunks, 2 * STATE_LANES), F32),
                        pltpu.VMEM((SSM_L * LANES, 2 * STATE_LANES), BF16),
                        pltpu.VMEM((SSM_L * LANES, 2 * STATE_LANES), BF16),
                        pltpu.VMEM((SSM_L * LANES, 2 * LANES), BF16)],
        compiler_params=_cparams(1),
        name="ssm_mixer",
    )(u, lam, b_blocks, c_blocks, dskip)


def _ssm_parameter_layout(log_step, lam_re, lam_im, b_re, b_im, c_re, c_im):
    nt, gl = N_LANE_TILES, GROUPS_PER_TILE
    step = jnp.broadcast_to(jnp.exp(log_step.astype(F32))[:, None], lam_re.shape)
    lam = jnp.stack([lam_re.astype(F32).reshape(nt, STATE_LANES),
                     lam_im.astype(F32).reshape(nt, STATE_LANES),
                     step.reshape(nt, STATE_LANES)], axis=1)
    eye = jnp.eye(gl, dtype=F32)

    def blocks(x):
        x5 = x.astype(F32).reshape(nt, gl, SSM_GROUP, SSM_STATE)
        return (x5[:, :, :, None, :] * eye[None, :, None, :, None]).reshape(nt, LANES, STATE_LANES)

    b_blocks = jnp.stack([blocks(b_re.transpose(0, 2, 1)), blocks(b_im.transpose(0, 2, 1))], axis=1)
    c_blocks = jnp.stack([blocks(c_re), blocks(c_im)], axis=1)
    return lam, b_blocks, c_blocks


def _glu_kernel(y_ref, w_ref, b_ref, o_ref):
    y = y_ref[...]
    z = jnp.dot(y.astype(BF16), w_ref[...], preferred_element_type=F32) + b_ref[...]
    o_ref[...] = (y * jax.nn.sigmoid(z)).astype(o_ref.dtype)


def _glu(y, w, b, tm):
    m, n = y.shape
    tm = min(tm, m)
    return pl.pallas_call(
        _glu_kernel,
        grid=(m // tm,),
        in_specs=[pl.BlockSpec((tm, n), lambda i: (i, 0)),
                  pl.BlockSpec((n, n), lambda i: (0, 0)),
                  pl.BlockSpec((1, n), lambda i: (0, 0))],
        out_specs=pl.BlockSpec((tm, n), lambda i: (i, 0)),
        out_shape=jax.ShapeDtypeStruct((m, n), BF16),
        compiler_params=_cparams(1),
        name="glu",
    )(y, w, b)


ATT_TQ = 512
LOG2_E = math.log2(math.e)
FAR_BUCKET = REL_BUCKETS // 2 - 1


def _attn_kernel(rel_ref, q_ref, k_ref, v_ref, bucket_ref, lq1_ref, lk1_ref, lq2_ref, lk2_ref, g_ref,
                 o_ref, m_ref, l_ref, acc_ref, bias_ref, sa_ref, sb_ref):
    h = pl.program_id(0)
    i = pl.program_id(1)
    tq = ATT_TQ

    @pl.when(i == 0)
    def _():
        far = rel_ref[FAR_BUCKET, h]

        def rows(r, carry):
            r0 = pl.multiple_of(r * 8, 8)
            bucket = bucket_ref[pl.ds(r0, 8), :]
            tile = jnp.where(bucket < 0, NEG_INF, 0.0)
            for b in range(REL_BUCKETS):
                tile = jnp.where(bucket == b, (rel_ref[b, h] - far) * LOG2_E, tile)
            bias_ref[pl.ds(r0, 8), :] = tile
            return carry

        lax.fori_loop(0, tq // 8, rows, 0)

    m_ref[...] = jnp.full_like(m_ref, NEG_INF)
    l_ref[...] = jnp.zeros_like(l_ref)
    acc_ref[...] = jnp.zeros_like(acc_ref)

    def scores(start, width, s_ref, bias):
        kb = k_ref[pl.ds(start, width), :]
        for c in range(2):
            q = q_ref[:, c * ATT_QK_DIM:(c + 1) * ATT_QK_DIM]
            kc = kb[:, c * ATT_QK_DIM:(c + 1) * ATT_QK_DIM]
            s = lax.dot_general(q, kc, (((1,), (1,)), ((), ())), preferred_element_type=F32)
            s_ref[c, :, 0:width] = s if bias is None else s + bias

    def absorb(start, width, s_ref):
        vb = v_ref[pl.ds(start, width), :]
        for c in range(2):
            tiles = [s_ref[c, :, t * LANES:(t + 1) * LANES] for t in range(width // LANES)]
            m_tile = functools.reduce(jnp.maximum, tiles)
            m_prev = m_ref[c]
            m_new = jnp.maximum(m_prev, jnp.max(m_tile, axis=1, keepdims=True))
            scale = jnp.exp2(m_prev - m_new)
            ps = [jnp.exp2(t - m_new) for t in tiles]
            l_ref[c] = scale * l_ref[c] + functools.reduce(jnp.add, ps)
            m_ref[c] = m_new
            pv = jnp.dot(jnp.concatenate(ps, axis=1).astype(BF16), vb, preferred_element_type=F32)
            for t in range(ATT_V_DIM // LANES):
                cols = slice(t * LANES, (t + 1) * LANES)
                acc_ref[c, :, cols] = scale * acc_ref[c, :, cols] + pv[:, cols]

    wide = 2 * tq

    @pl.when(i == 0)
    def _():
        scores(0, tq, sa_ref, bias_ref[:, tq:wide])
        absorb(0, tq, sa_ref)

    @pl.when(i >= 1)
    def _():
        near = pl.multiple_of((i - 1) * tq, tq)
        scores(near, wide, sa_ref, bias_ref[...])
        n_far = i - 1
        n_wide = lax.shift_right_logical(n_far, 1)
        has_narrow = lax.rem(n_far, 2) == 1
        n_loop = lax.shift_right_logical(n_wide, 1)
        far = lambda p: pl.multiple_of(p * wide, wide)

        def two_wide(r, carry):
            absorb(jnp.where(r == 0, near, far(2 * r - 1)), wide, sa_ref)
            scores(far(2 * r), wide, sb_ref, None)
            absorb(far(2 * r), wide, sb_ref)
            scores(far(2 * r + 1), wide, sa_ref, None)
            return carry

        lax.fori_loop(0, n_loop, two_wide, 0)
        pending = jnp.where(n_loop == 0, near, far(2 * n_loop - 1))
        odd_wide = lax.rem(n_wide, 2) == 1
        last_wide = far(n_wide - 1)
        narrow = pl.multiple_of(n_wide * wide, tq)

        @pl.when(jnp.logical_and(odd_wide, has_narrow))
        def _():
            absorb(pending, wide, sa_ref)
            scores(last_wide, wide, sb_ref, None)
            absorb(last_wide, wide, sb_ref)
            scores(narrow, tq, sa_ref, None)
            absorb(narrow, tq, sa_ref)

        @pl.when(jnp.logical_and(odd_wide, jnp.logical_not(has_narrow)))
        def _():
            absorb(pending, wide, sa_ref)
            scores(last_wide, wide, sb_ref, None)
            absorb(last_wide, wide, sb_ref)

        @pl.when(jnp.logical_and(jnp.logical_not(odd_wide), has_narrow))
        def _():
            absorb(pending, wide, sa_ref)
            scores(narrow, tq, sb_ref, None)
            absorb(narrow, tq, sb_ref)

        @pl.when(jnp.logical_and(jnp.logical_not(odd_wide), jnp.logical_not(has_narrow)))
        def _():
            absorb(pending, wide, sa_ref)

    lam = (jnp.exp(jnp.sum(lq1_ref[...] * lk1_ref[...], axis=1, keepdims=True))
           - jnp.exp(jnp.sum(lq2_ref[...] * lk2_ref[...], axis=1, keepdims=True)) + LAMBDA_INIT)
    l0 = jnp.sum(l_ref[0], axis=1, keepdims=True)
    l1 = jnp.sum(l_ref[1], axis=1, keepdims=True)
    o = acc_ref[0] / l0 - lam * (acc_ref[1] / l1)
    o = o * lax.rsqrt(jnp.mean(o * o, axis=1, keepdims=True) + LN_EPS) * g_ref[...]
    o_ref[...] = (o * (1.0 - LAMBDA_INIT)).astype(o_ref.dtype)


def _t5_bucket(rel):
    half = REL_BUCKETS // 2
    max_exact = half // 2
    ret = jnp.where(rel > 0, half, 0)
    n = jnp.abs(rel)
    nf = jnp.maximum(n, 1).astype(jnp.float32)
    large = max_exact + (jnp.log(nf / max_exact) / math.log(REL_MAX_DIST / max_exact)
                         * (half - max_exact)).astype(jnp.int32)
    large = jnp.minimum(large, half - 1)
    return ret + jnp.where(n < max_exact, n, large)


def _near_buckets():
    tq = ATT_TQ
    qpos = np.arange(tq)[:, None]
    kpos = np.arange(-tq, tq)[None, :]
    rel = jnp.asarray(kpos - qpos, jnp.int32)
    visible = jnp.asarray((kpos // CHUNK) <= (qpos // CHUNK))
    return jnp.where(visible, _t5_bucket(rel), -1).astype(jnp.int32)


def _diff_attention(qkv, rel_bias, lq1, lk1, lq2, lk2, subln_g):
    seq = qkv.shape[0]
    tq = ATT_TQ
    head_blk = 2 * ATT_QK_DIM
    vec = pl.BlockSpec((1, ATT_QK_DIM), lambda h, i: (0, 0))
    return pl.pallas_call(
        _attn_kernel,
        grid=(ATT_HEADS, seq // tq),
        in_specs=[pl.BlockSpec(memory_space=pltpu.SMEM),
                  pl.BlockSpec((tq, head_blk), lambda h, i: (i, h)),
                  pl.BlockSpec((seq, head_blk), lambda h, i: (0, ATT_HEADS + h)),
                  pl.BlockSpec((seq, ATT_V_DIM), lambda h, i: (0, 2 * ATT_HEADS + h)),
                  pl.BlockSpec((tq, 2 * tq), lambda h, i: (0, 0)),
                  vec, vec, vec, vec,
                  pl.BlockSpec((1, ATT_V_DIM), lambda h, i: (0, 0))],
        out_specs=pl.BlockSpec((tq, ATT_V_DIM), lambda h, i: (i, h)),
        out_shape=jax.ShapeDtypeStruct((seq, ATT_WIDTH), BF16),
        scratch_shapes=[pltpu.VMEM((2, tq, LANES), F32),
                        pltpu.VMEM((2, tq, LANES), F32),
                        pltpu.VMEM((2, tq, ATT_V_DIM), F32),
                        pltpu.VMEM((tq, 2 * tq), F32),
                        pltpu.VMEM((2, tq, 2 * tq), F32),
                        pltpu.VMEM((2, tq, 2 * tq), F32)],
        compiler_params=_cparams(2),
        name="diff_attention",
    )(rel_bias.astype(F32), qkv, qkv, qkv, _near_buckets(), lq1, lk1, lq2, lk2, subln_g)


MM_COL_CHUNK = 1024
LN_ROW_CHUNK = 128


def _residual_layer_norm(res_ref, acc_ref, g_ref, b_ref, out_refs):
    g = g_ref[...]
    b = b_ref[...]
    for r0 in range(0, acc_ref.shape[0], LN_ROW_CHUNK):
        rows = slice(r0, r0 + LN_ROW_CHUNK)
        r = ALPHA * res_ref[rows, :] + acc_ref[rows, :]
        mu = jnp.mean(r, axis=1, keepdims=True)
        xc = r - mu
        var = jnp.mean(xc * xc, axis=1, keepdims=True)
        h = xc * lax.rsqrt(var + LN_EPS) * g + b
        for o_ref in out_refs:
            o_ref[rows, :] = h.astype(o_ref.dtype)


def _outproj_kernel(ys_ref, ya_ref, wt_ref, wb_ref, x_ref, g_ref, b_ref, h_ref, hb_ref):
    k = pl.program_id(1)

    @pl.when(k == 0)
    def _():
        h_ref[...] = jnp.zeros_like(h_ref)

    for c in range(0, h_ref.shape[1], MM_COL_CHUNK):
        cols = slice(c, c + MM_COL_CHUNK)
        h_ref[:, cols] += (jnp.dot(ys_ref[...], wt_ref[:, cols], preferred_element_type=F32)
                           + jnp.dot(ya_ref[...], wb_ref[:, cols], preferred_element_type=F32))

    @pl.when(k == pl.num_programs(1) - 1)
    def _():
        _residual_layer_norm(x_ref, h_ref, g_ref, b_ref, (h_ref, hb_ref))


def _outproj_ln(ys, ya, w_out, x, g, b, tm, tk):
    m, half = ys.shape
    d = w_out.shape[1]
    tm = min(tm, m)
    nk = half // tk
    row = pl.BlockSpec((1, d), lambda i, k: (0, 0))
    return pl.pallas_call(
        _outproj_kernel,
        grid=(m // tm, nk),
        in_specs=[pl.BlockSpec((tm, tk), lambda i, k: (i, k)),
                  pl.BlockSpec((tm, tk), lambda i, k: (i, k)),
                  pl.BlockSpec((tk, d), lambda i, k: (k, 0)),
                  pl.BlockSpec((tk, d), lambda i, k: (k + nk, 0)),
                  pl.BlockSpec((tm, d), lambda i, k: (i, 0), pipeline_mode=pl.Buffered(1)),
                  row, row],
        out_specs=[pl.BlockSpec((tm, d), lambda i, k: (i, 0)),
                   pl.BlockSpec((tm, d), lambda i, k: (i, 0))],
        out_shape=[jax.ShapeDtypeStruct((m, d), F32), jax.ShapeDtypeStruct((m, d), BF16)],
        compiler_params=_cparams(2, VMEM_LIMIT_LARGE),
        name="outproj_ln",
    )(ys, ya, w_out, w_out, x, g, b)


FFN_TF = 256
FFN_TILES = D_FF // FFN_TF
FFN_TILES_PER_STEP = 2
FFN_STEPS = -(-FFN_TILES // FFN_TILES_PER_STEP)
FFN_TILES_PAD = FFN_STEPS * FFN_TILES_PER_STEP
FFN_HALO = 8


def _cast_up_tiles_kernel(a_ref, g_ref, o_ref):
    t = pl.program_id(0)

    @pl.when(t < FFN_TILES)
    def _():
        o_ref[:, 0:FFN_TF] = a_ref[...].astype(BF16)
        o_ref[:, FFN_TF:2 * FFN_TF] = g_ref[...].astype(BF16)

    @pl.when(t >= FFN_TILES)
    def _():
        o_ref[...] = jnp.zeros_like(o_ref)


def _cast_up_tiles(w_up):
    d = w_up.shape[0]
    last = FFN_TILES - 1
    return pl.pallas_call(
        _cast_up_tiles_kernel,
        grid=(FFN_TILES_PAD,),
        in_specs=[pl.BlockSpec((d, FFN_TF), lambda t: (0, jnp.minimum(t, last))),
                  pl.BlockSpec((d, FFN_TF), lambda t: (0, FFN_TILES + jnp.minimum(t, last)))],
        out_specs=pl.BlockSpec((None, d, 2 * FFN_TF), lambda t: (t, 0, 0)),
        out_shape=jax.ShapeDtypeStruct((FFN_TILES_PAD, d, 2 * FFN_TF), BF16),
        compiler_params=_cparams(1),
        name="cast_w_up",
    )(w_up, w_up)


def _cast_down_kernel(w_ref, o_ref):
    t = pl.program_id(0)

    @pl.when(t < FFN_TILES)
    def _():
        o_ref[...] = w_ref[...].astype(BF16)

    @pl.when(t >= FFN_TILES)
    def _():
        o_ref[...] = jnp.zeros_like(o_ref)


def _cast_down(w_down):
    d = w_down.shape[1]
    return pl.pallas_call(
        _cast_down_kernel,
        grid=(FFN_TILES_PAD,),
        in_specs=[pl.BlockSpec((FFN_TF, d), lambda t: (jnp.minimum(t, FFN_TILES - 1), 0))],
        out_specs=pl.BlockSpec((FFN_TF, d), lambda t: (t, 0)),
        out_shape=jax.ShapeDtypeStruct((FFN_TILES_PAD * FFN_TF, d), BF16),
        compiler_params=_cparams(1),
        name="cast_w_down",
    )(w_down)


def _ffn_kernel(h_ref, wu_ref, cp_ref, wd_ref, r_hbm, lg_ref, lb_ref, o_ref, halo_ref, r_ref, r_sem):
    n = FFN_TILES_PER_STEP
    i = pl.program_id(0)
    f = pl.program_id(1)
    tm = h_ref.shape[0]

    def residual_copy():
        rows = pl.ds(pl.multiple_of(i * tm, tm), tm)
        return pltpu.make_async_copy(r_hbm.at[rows, :], r_ref, r_sem)

    @pl.when(f == 0)
    def _():
        residual_copy().start()
        o_ref[...] = jnp.zeros_like(o_ref)

    h = h_ref[...]
    row = lax.broadcasted_iota(jnp.int32, (tm, FFN_TF), 0)
    acts = []
    for k in range(n):
        tile = f * n + k
        cols = slice(k * FFN_TF, (k + 1) * FFN_TF)
        value = jnp.dot(h, wu_ref[k, :, 0:FFN_TF], preferred_element_type=F32)
        gate = jnp.dot(h, wu_ref[k, :, FFN_TF:2 * FFN_TF], preferred_element_type=F32)
        halo = jnp.where(i == 0, 0.0, halo_ref[tile])
        halo_ref[tile] = gate[tm - FFN_HALO:tm, :]
        prev1 = jnp.where(row == 0, halo[FFN_HALO - 1:FFN_HALO, :], pltpu.roll(gate, 1, 0))
        prev2 = jnp.where(row == 0, halo[FFN_HALO - 2:FFN_HALO - 1, :],
                          jnp.where(row == 1, halo[FFN_HALO - 1:FFN_HALO, :], pltpu.roll(gate, 2, 0)))
        gc = (cp_ref[3:4, cols] + prev2 * cp_ref[0:1, cols] + prev1 * cp_ref[1:2, cols]
              + gate * cp_ref[2:3, cols])
        acts.append(((gc * jax.nn.sigmoid(gc)) * value).astype(BF16))

    for c in range(0, o_ref.shape[1], MM_COL_CHUNK):
        cols = slice(c, c + MM_COL_CHUNK)
        contrib = jnp.dot(acts[0], wd_ref[0:FFN_TF, cols], preferred_element_type=F32)
        for k in range(1, n):
            contrib += jnp.dot(acts[k], wd_ref[k * FFN_TF:(k + 1) * FFN_TF, cols],
                               preferred_element_type=F32)
        o_ref[:, cols] += contrib

    @pl.when(f == pl.num_programs(1) - 1)
    def _():
        residual_copy().wait()
        _residual_layer_norm(r_ref, o_ref, lg_ref, lb_ref, (o_ref,))


def _ffn_ln(h_bf, w_up_tiles, conv_params, w_down_pad, h1, g, b, tm):
    m, d = h_bf.shape
    tm = min(tm, m)
    n = FFN_TILES_PER_STEP
    once = pl.Buffered(1)
    row = pl.BlockSpec((1, d), lambda i, f: (0, 0))
    return pl.pallas_call(
        _ffn_kernel,
        grid=(m // tm, FFN_STEPS),
        in_specs=[pl.BlockSpec((tm, d), lambda i, f: (i, 0), pipeline_mode=once),
                  pl.BlockSpec((n, d, 2 * FFN_TF), lambda i, f: (f, 0, 0)),
                  pl.BlockSpec((4, n * FFN_TF), lambda i, f: (0, f)),
                  pl.BlockSpec((n * FFN_TF, d), lambda i, f: (f, 0)),
                  pl.BlockSpec(memory_space=pl.ANY),
                  row, row],
        out_specs=pl.BlockSpec((tm, d), lambda i, f: (i, 0), pipeline_mode=once),
        out_shape=jax.ShapeDtypeStruct((m, d), F32),
        scratch_shapes=[pltpu.VMEM((FFN_TILES_PAD, FFN_HALO, FFN_TF), F32),
                        pltpu.VMEM((tm, d), F32),
                        pltpu.SemaphoreType.DMA(())],
        compiler_params=_cparams(2, VMEM_LIMIT_LARGE),
        name="ffn_ln",
    )(h_bf, w_up_tiles, conv_params, w_down_pad, h1, g, b)


def kernel(x, w_in, ssm_log_step, ssm_lambda_re, ssm_lambda_im, ssm_b_re, ssm_b_im, ssm_c_re, ssm_c_im, ssm_d, ssm_w_glu, ssm_b_glu, att_lambda_q1, att_lambda_k1, att_lambda_q2, att_lambda_k2, att_subln_g, rel_bias, w_out, ln1_g, ln1_b, ffn_w_up, ffn_conv_w, ffn_conv_b, ffn_w_down, ln2_g, ln2_b):
    bsz, seq, _ = x.shape
    assert bsz == 1 and DEPTH == 1
    l = 0
    xs = x[0]

    qk_scale = ATT_QK_DIM ** -0.5 * LOG2_E
    col_scale = np.ones((1, w_in.shape[2]), np.float32)
    col_scale[:, SSM_WIDTH:SSM_WIDTH + QK_WIDTH] = qk_scale
    w_in_bf = (w_in[l] * jnp.asarray(col_scale)).astype(BF16)
    x_bf = xs.astype(BF16)

    u = _matmul(x_bf, w_in_bf, F32, 1024, 1024, 0, SSM_WIDTH)
    qkv = _matmul(x_bf, w_in_bf, BF16, 1024, 1024, SSM_WIDTH, w_in.shape[2] - SSM_WIDTH)

    lam, b_blocks, c_blocks = _ssm_parameter_layout(
        ssm_log_step[l], ssm_lambda_re[l], ssm_lambda_im[l], ssm_b_re[l], ssm_b_im[l],
        ssm_c_re[l], ssm_c_im[l])
    dskip = ssm_d[l].astype(F32).reshape(N_LANE_TILES, 1, LANES)
    y = _ssm_mixer(u, lam, b_blocks, c_blocks, dskip)
    y_ssm = _glu(y, ssm_w_glu[l].astype(BF16), ssm_b_glu[l].astype(F32).reshape(1, SSM_WIDTH), 512)

    vec = lambda a: a.astype(F32).reshape(1, ATT_QK_DIM)
    y_att = _diff_attention(qkv, rel_bias,
                            vec(att_lambda_q1[l]), vec(att_lambda_k1[l]),
                            vec(att_lambda_q2[l]), vec(att_lambda_k2[l]),
                            att_subln_g[l].astype(F32).reshape(1, ATT_V_DIM))

    h1, h1_bf = _outproj_ln(y_ssm, y_att, w_out[l].astype(BF16), xs,
                            ln1_g[l].reshape(1, D_MODEL), ln1_b[l].reshape(1, D_MODEL), 512, 512)

    pad = FFN_TILES_PAD * FFN_TF - D_FF
    conv_params = jnp.pad(jnp.concatenate([ffn_conv_w[l].astype(F32), ffn_conv_b[l].astype(F32)[None]], axis=0),
                          ((0, 0), (0, pad)))
    out = _ffn_ln(h1_bf, _cast_up_tiles(ffn_w_up[l]), conv_params, _cast_down(ffn_w_down[l]), h1,
                  ln2_g[l].reshape(1, D_MODEL), ln2_b[l].reshape(1, D_MODEL), 512)
    return out[None]
```

```python
import functools
import math

import numpy as np
import jax
import jax.numpy as jnp
from jax import lax
from jax.experimental import pallas as pl
from jax.experimental.pallas import tpu as pltpu

F32 = jnp.float32
BF16 = jnp.bfloat16

D_MODEL = 4096
CHUNK = 64
SSM_WIDTH = 2048
SSM_GROUP = 16
SSM_GROUPS = SSM_WIDTH // SSM_GROUP
SSM_STATE = 64
ATT_QK_DIM = 128
ATT_V_DIM = 256
ATT_WIDTH = 2048
ATT_HEADS = 8
QK_WIDTH = 2048
D_FF = 11008
REL_BUCKETS = 32
REL_MAX_DIST = 128
DEPTH = 1
ALPHA = (2 * DEPTH) ** 0.25
LN_EPS = 1e-5
NEG_INF = -1e30
LAMBDA_INIT = 0.8 - 0.6 * math.exp(-0.3 * 0)

LANES = 128
SSM_L = 16
GROUPS_PER_TILE = LANES // SSM_GROUP
N_LANE_TILES = SSM_WIDTH // LANES
STATE_LANES = GROUPS_PER_TILE * SSM_STATE
VMEM_LIMIT = 56 * 1024 * 1024
VMEM_LIMIT_LARGE = 62 * 1024 * 1024


def _cparams(n_axes, vmem=VMEM_LIMIT):
    return pltpu.CompilerParams(dimension_semantics=("arbitrary",) * n_axes,
                                vmem_limit_bytes=vmem)


def _matmul_kernel(x_ref, w_ref, o_ref):
    o_ref[...] = jnp.dot(x_ref[...], w_ref[...], preferred_element_type=F32).astype(o_ref.dtype)


def _matmul(x, w, out_dtype, tm, tn, col_start=0, n_cols=None):
    m, kdim = x.shape
    n = w.shape[1] - col_start if n_cols is None else n_cols
    tm, tn = min(tm, m), min(tn, n)
    assert m % tm == 0 and n % tn == 0 and col_start % tn == 0
    j0 = col_start // tn
    return pl.pallas_call(
        _matmul_kernel,
        grid=(m // tm, n // tn),
        in_specs=[pl.BlockSpec((tm, kdim), lambda i, j: (i, 0)),
                  pl.BlockSpec((kdim, tn), lambda i, j: (0, j + j0))],
        out_specs=pl.BlockSpec((tm, tn), lambda i, j: (i, j)),
        out_shape=jax.ShapeDtypeStruct((m, n), out_dtype),
        compiler_params=_cparams(2),
        name="matmul",
    )(x, w)


def _gelu_tanh(x):
    c = math.sqrt(2.0 / math.pi)
    return 0.5 * x * (1.0 + jnp.tanh(c * (x + 0.044715 * (x * x * x))))


def _ssm_kernel(u_ref, lam_ref, b_ref, c_ref, dskip_ref, y_ref,
                urev_ref, e_ref, hin_ref, p_ref, qt_ref, d_ref):
    n_chunks = u_ref.shape[0] // SSM_L
    contract_last = (((1,), (1,)), ((), ()))

    lr = lam_ref[0:1, :]
    li = lam_ref[1:2, :]
    step = lam_ref[2:3, :]
    mag = jnp.exp(lr * step)
    a_re = mag * jnp.cos(li * step)
    a_im = mag * jnp.sin(li * step)
    den = lr * lr + li * li
    z_re = ((a_re - 1.0) * lr + a_im * li) / den
    z_im = (a_im * lr - (a_re - 1.0) * li) / den

    b_re, b_im = b_ref[0], b_ref[1]
    c_re, c_im = c_ref[0], c_ref[1]
    c0 = jnp.concatenate([c_re, -c_im], axis=1).astype(BF16)
    w_re, w_im = z_re, z_im
    pw_re, pw_im = a_re, a_im
    for s in range(SSM_L):
        rows = slice(s * LANES, (s + 1) * LANES)
        p_ref[rows, 0:STATE_LANES] = (w_re * b_re - w_im * b_im).astype(BF16)
        p_ref[rows, STATE_LANES:2 * STATE_LANES] = (w_re * b_im + w_im * b_re).astype(BF16)
        qt_ref[rows, 0:STATE_LANES] = (c_re * pw_re - c_im * pw_im).astype(BF16)
        qt_ref[rows, STATE_LANES:2 * STATE_LANES] = (-(c_re * pw_im + c_im * pw_re)).astype(BF16)
        w_re, w_im = w_re * a_re - w_im * a_im, w_re * a_im + w_im * a_re
        if s + 1 < SSM_L:
            pw_re, pw_im = pw_re * a_re - pw_im * a_im, pw_re * a_im + pw_im * a_re
    d = lax.dot_general(p_ref[...], c0, contract_last, preferred_element_type=F32).astype(BF16)
    d_ref[:, LANES:2 * LANES] = d
    d_ref[0:LANES, 0:LANES] = jnp.zeros((LANES, LANES), BF16)
    d_ref[LANES:SSM_L * LANES, 0:LANES] = d[0:(SSM_L - 1) * LANES, :]

    for i in range(SSM_L):
        rows = u_ref[pl.ds(i, n_chunks, stride=SSM_L), :]
        urev_ref[:, (SSM_L - 1 - i) * LANES:(SSM_L - i) * LANES] = rows.astype(BF16)

    e_ref[...] = jnp.dot(urev_ref[...], p_ref[...], preferred_element_type=F32)

    def chunk_step(c, carry):
        h_re, h_im = carry
        hin_ref[pl.ds(c, 1), 0:STATE_LANES] = h_re
        hin_ref[pl.ds(c, 1), STATE_LANES:2 * STATE_LANES] = h_im
        e_re = e_ref[pl.ds(c, 1), 0:STATE_LANES]
        e_im = e_ref[pl.ds(c, 1), STATE_LANES:2 * STATE_LANES]
        return (pw_re * h_re - pw_im * h_im + e_re,
                pw_re * h_im + pw_im * h_re + e_im)

    zero = jnp.zeros((1, STATE_LANES), F32)
    lax.fori_loop(0, n_chunks, chunk_step, (zero, zero))

    carry_in = lax.dot_general(hin_ref[...].astype(BF16), qt_ref[...], contract_last,
                               preferred_element_type=F32)

    dskip = dskip_ref[...]
    for i0 in range(0, SSM_L, 2):
        intra = jnp.dot(urev_ref[:, (SSM_L - 2 - i0) * LANES:], d_ref[0:(i0 + 2) * LANES, :],
                        preferred_element_type=F32)
        for i in (i0, i0 + 1):
            u_i = u_ref[pl.ds(i, n_chunks, stride=SSM_L), :]
            y = (intra[:, (i - i0) * LANES:(i - i0 + 1) * LANES] + carry_in[:, i * LANES:(i + 1) * LANES]
                 + dskip * u_i)
            y_ref[pl.ds(i, n_chunks, stride=SSM_L), :] = _gelu_tanh(y)


def _ssm_mixer(u, lam, b_blocks, c_blocks, dskip):
    seq = u.shape[0]
    n_chunks = seq // SSM_L
    return pl.pallas_call(
        _ssm_kernel,
        grid=(N_LANE_TILES,),
        in_specs=[pl.BlockSpec((seq, LANES), lambda j: (0, j)),
                  pl.BlockSpec((None, 3, STATE_LANES), lambda j: (j, 0, 0)),
                  pl.BlockSpec((None, 2, LANES, STATE_LANES), lambda j: (j, 0, 0, 0)),
                  pl.BlockSpec((None, 2, LANES, STATE_LANES), lambda j: (j, 0, 0, 0)),
                  pl.BlockSpec((None, 1, LANES), lambda j: (j, 0, 0))],
        out_specs=pl.BlockSpec((seq, LANES), lambda j: (0, j)),
        out_shape=jax.ShapeDtypeStruct((seq, SSM_WIDTH), F32),
        scratch_shapes=[pltpu.VMEM((n_chunks, SSM_L * LANES), BF16),
                        pltpu.VMEM((n_chunks, 2 * STATE_LANES), F32),
                        pltpu.VMEM((n_chunks, 2 * STATE_LANES), F32),
                        pltpu.VMEM((SSM_L * LANES, 2 * STATE_LANES), BF16),
                        pltpu.VMEM((SSM_L * LANES, 2 * STATE_LANES), BF16),
                        pltpu.VMEM((SSM_L * LANES, 2 * LANES), BF16)],
        compiler_params=_cparams(1),
        name="ssm_mixer",
    )(u, lam, b_blocks, c_blocks, dskip)


def _ssm_parameter_layout(log_step, lam_re, lam_im, b_re, b_im, c_re, c_im):
    nt, gl = N_LANE_TILES, GROUPS_PER_TILE
    step = jnp.broadcast_to(jnp.exp(log_step.astype(F32))[:, None], lam_re.shape)
    lam = jnp.stack([lam_re.astype(F32).reshape(nt, STATE_LANES),
                     lam_im.astype(F32).reshape(nt, STATE_LANES),
                     step.reshape(nt, STATE_LANES)], axis=1)
    eye = jnp.eye(gl, dtype=F32)

    def blocks(x):
        x5 = x.astype(F32).reshape(nt, gl, SSM_GROUP, SSM_STATE)
        return (x5[:, :, :, None, :] * eye[None, :, None, :, None]).reshape(nt, LANES, STATE_LANES)

    b_blocks = jnp.stack([blocks(b_re.transpose(0, 2, 1)), blocks(b_im.transpose(0, 2, 1))], axis=1)
    c_blocks = jnp.stack([blocks(c_re), blocks(c_im)], axis=1)
    return lam, b_blocks, c_blocks


def _glu_kernel(y_ref, w_ref, b_ref, o_ref):
    y = y_ref[...]
    z = jnp.dot(y.astype(BF16), w_ref[...], preferred_element_type=F32) + b_ref[...]
    o_ref[...] = (y * jax.nn.sigmoid(z)).astype(o_ref.dtype)


def _glu(y, w, b, tm):
    m, n = y.shape
    tm = min(tm, m)
    return pl.pallas_call(
        _glu_kernel,
        grid=(m // tm,),
        in_specs=[pl.BlockSpec((tm, n), lambda i: (i, 0)),
                  pl.BlockSpec((n, n), lambda i: (0, 0)),
                  pl.BlockSpec((1, n), lambda i: (0, 0))],
        out_specs=pl.BlockSpec((tm, n), lambda i: (i, 0)),
        out_shape=jax.ShapeDtypeStruct((m, n), BF16),
        compiler_params=_cparams(1),
        name="glu",
    )(y, w, b)


ATT_TQ = 512
LOG2_E = math.log2(math.e)
FAR_BUCKET = REL_BUCKETS // 2 - 1


def _attn_kernel(rel_ref, q_ref, k_ref, v_ref, bucket_ref, lq1_ref, lk1_ref, lq2_ref, lk2_ref, g_ref,
                 o_ref, m_ref, l_ref, acc_ref, bias_ref, sa_ref, sb_ref):
    h = pl.program_id(0)
    i = pl.program_id(1)
    tq = ATT_TQ

    @pl.when(i == 0)
    def _():
        far = rel_ref[FAR_BUCKET, h]

        def rows(r, carry):
            r0 = pl.multiple_of(r * 8, 8)
            bucket = bucket_ref[pl.ds(r0, 8), :]
            tile = jnp.where(bucket < 0, NEG_INF, 0.0)
            for b in range(REL_BUCKETS):
                tile = jnp.where(bucket == b, (rel_ref[b, h] - far) * LOG2_E, tile)
            bias_ref[pl.ds(r0, 8), :] = tile
            return carry

        lax.fori_loop(0, tq // 8, rows, 0)

    m_ref[...] = jnp.full_like(m_ref, NEG_INF)
    l_ref[...] = jnp.zeros_like(l_ref)
    acc_ref[...] = jnp.zeros_like(acc_ref)

    def scores(start, width, s_ref, bias):
        kb = k_ref[pl.ds(start, width), :]
        for c in range(2):
            q = q_ref[:, c * ATT_QK_DIM:(c + 1) * ATT_QK_DIM]
            kc = kb[:, c * ATT_QK_DIM:(c + 1) * ATT_QK_DIM]
            s = lax.dot_general(q, kc, (((1,), (1,)), ((), ())), preferred_element_type=F32)
            s_ref[c, :, 0:width] = s if bias is None else s + bias

    def absorb(start, width, s_ref):
        vb = v_ref[pl.ds(start, width), :]
        for c in range(2):
            tiles = [s_ref[c, :, t * LANES:(t + 1) * LANES] for t in range(width // LANES)]
            m_tile = functools.reduce(jnp.maximum, tiles)
            m_prev = m_ref[c]
            m_new = jnp.maximum(m_prev, jnp.max(m_tile, axis=1, keepdims=True))
            scale = jnp.exp2(m_prev - m_new)
            ps = [jnp.exp2(t - m_new) for t in tiles]
            l_ref[c] = scale * l_ref[c] + functools.reduce(jnp.add, ps)
            m_ref[c] = m_new
            pv = jnp.dot(jnp.concatenate(ps, axis=1).astype(BF16), vb, preferred_element_type=F32)
            for t in range(ATT_V_DIM // LANES):
                cols = slice(t * LANES, (t + 1) * LANES)
                acc_ref[c, :, cols] = scale * acc_ref[c, :, cols] + pv[:, cols]

    wide = 2 * tq

    @pl.when(i == 0)
    def _():
        scores(0, tq, sa_ref, bias_ref[:, tq:wide])
        absorb(0, tq, sa_ref)

    @pl.when(i >= 1)
    def _():
        near = pl.multiple_of((i - 1) * tq, tq)
        scores(near, wide, sa_ref, bias_ref[...])
        n_far = i - 1
        n_wide = lax.shift_right_logical(n_far, 1)
        has_narrow = lax.rem(n_far, 2) == 1
        n_loop = lax.shift_right_logical(n_wide, 1)
        far = lambda p: pl.multiple_of(p * wide, wide)

        def two_wide(r, carry):
            absorb(jnp.where(r == 0, near, far(2 * r - 1)), wide, sa_ref)
            scores(far(2 * r), wide, sb_ref, None)
            absorb(far(2 * r), wide, sb_ref)
            scores(far(2 * r + 1), wide, sa_ref, None)
            return carry

        lax.fori_loop(0, n_loop, two_wide, 0)
        pending = jnp.where(n_loop == 0, near, far(2 * n_loop - 1))
        odd_wide = lax.rem(n_wide, 2) == 1
        last_wide = far(n_wide - 1)
        narrow = pl.multiple_of(n_wide * wide, tq)

        @pl.when(jnp.logical_and(odd_wide, has_narrow))
        def _():
            absorb(pending, wide, sa_ref)
            scores(last_wide, wide, sb_ref, None)
            absorb(last_wide, wide, sb_ref)
            scores(narrow, tq, sa_ref, None)
            absorb(narrow, tq, sa_ref)

        @pl.when(jnp.logical_and(odd_wide, jnp.logical_not(has_narrow)))
        def _():
            absorb(pending, wide, sa_ref)
            scores(last_wide, wide, sb_ref, None)
            absorb(last_wide, wide, sb_ref)

        @pl.when(jnp.logical_and(jnp.logical_not(odd_wide), has_narrow))
        def _():
            absorb(pending, wide, sa_ref)
            scores(narrow, tq, sb_ref, None)
            absorb(narrow, tq, sb_ref)

        @pl.when(jnp.logical_and(jnp.logical_not(odd_wide), jnp.logical_not(has_narrow)))
        def _():
            absorb(pending, wide, sa_ref)

    lam = (jnp.exp(jnp.sum(lq1_ref[...] * lk1_ref[...], axis=1, keepdims=True))
           - jnp.exp(jnp.sum(lq2_ref[...] * lk2_ref[...], axis=1, keepdims=True)) + LAMBDA_INIT)
    l0 = jnp.sum(l_ref[0], axis=1, keepdims=True)
    l1 = jnp.sum(l_ref[1], axis=1, keepdims=True)
    o = acc_ref[0] / l0 - lam * (acc_ref[1] / l1)
    o = o * lax.rsqrt(jnp.mean(o * o, axis=1, keepdims=True) + LN_EPS) * g_ref[...]
    o_ref[...] = (o * (1.0 - LAMBDA_INIT)).astype(o_ref.dtype)


def _t5_bucket(rel):
    half = REL_BUCKETS // 2
    max_exact = half // 2
    ret = jnp.where(rel > 0, half, 0)
    n = jnp.abs(rel)
    nf = jnp.maximum(n, 1).astype(jnp.float32)
    large = max_exact + (jnp.log(nf / max_exact) / math.log(REL_MAX_DIST / max_exact)
                         * (half - max_exact)).astype(jnp.int32)
    large = jnp.minimum(large, half - 1)
    return ret + jnp.where(n < max_exact, n, large)


def _near_buckets():
    tq = ATT_TQ
    qpos = np.arange(tq)[:, None]
    kpos = np.arange(-tq, tq)[None, :]
    rel = jnp.asarray(kpos - qpos, jnp.int32)
    visible = jnp.asarray((kpos // CHUNK) <= (qpos // CHUNK))
    return jnp.where(visible, _t5_bucket(rel), -1).astype(jnp.int32)


def _diff_attention(qkv, rel_bias, lq1, lk1, lq2, lk2, subln_g):
    seq = qkv.shape[0]
    tq = ATT_TQ
    head_blk = 2 * ATT_QK_DIM
    vec = pl.BlockSpec((1, ATT_QK_DIM), lambda h, i: (0, 0))
    return pl.pallas_call(
        _attn_kernel,
        grid=(ATT_HEADS, seq // tq),
        in_specs=[pl.BlockSpec(memory_space=pltpu.SMEM),
                  pl.BlockSpec((tq, head_blk), lambda h, i: (i, h)),
                  pl.BlockSpec((seq, head_blk), lambda h, i: (0, ATT_HEADS + h)),
                  pl.BlockSpec((seq, ATT_V_DIM), lambda h, i: (0, 2 * ATT_HEADS + h)),
                  pl.BlockSpec((tq, 2 * tq), lambda h, i: (0, 0)),
                  vec, vec, vec, vec,
                  pl.BlockSpec((1, ATT_V_DIM), lambda h, i: (0, 0))],
        out_specs=pl.BlockSpec((tq, ATT_V_DIM), lambda h, i: (i, h)),
        out_shape=jax.ShapeDtypeStruct((seq, ATT_WIDTH), BF16),
        scratch_shapes=[pltpu.VMEM((2, tq, LANES), F32),
                        pltpu.VMEM((2, tq, LANES), F32),
                        pltpu.VMEM((2, tq, ATT_V_DIM), F32),
                        pltpu.VMEM((tq, 2 * tq), F32),
                        pltpu.VMEM((2, tq, 2 * tq), F32),
                        pltpu.VMEM((2, tq, 2 * tq), F32)],
        compiler_params=_cparams(2),
        name="diff_attention",
    )(rel_bias.astype(F32), qkv, qkv, qkv, _near_buckets(), lq1, lk1, lq2, lk2, subln_g)


MM_COL_CHUNK = 1024
LN_ROW_CHUNK = 128


def _residual_layer_norm(res_ref, acc_ref, g_ref, b_ref, out_refs):
    g = g_ref[...]
    b = b_ref[...]
    for r0 in range(0, acc_ref.shape[0], LN_ROW_CHUNK):
        rows = slice(r0, r0 + LN_ROW_CHUNK)
        r = ALPHA * res_ref[rows, :] + acc_ref[rows, :]
        mu = jnp.mean(r, axis=1, keepdims=True)
        xc = r - mu
        var = jnp.mean(xc * xc, axis=1, keepdims=True)
        h = xc * lax.rsqrt(var + LN_EPS) * g + b
        for o_ref in out_refs:
            o_ref[rows, :] = h.astype(o_ref.dtype)


def _outproj_kernel(ys_ref, ya_ref, wt_ref, wb_ref, x_hbm, g_ref, b_ref, h_ref, hb_ref, x_ref, x_sem):
    i = pl.program_id(0)
    k = pl.program_id(1)
    tm = h_ref.shape[0]

    def residual_copy():
        rows = pl.ds(pl.multiple_of(i * tm, tm), tm)
        return pltpu.make_async_copy(x_hbm.at[rows, :], x_ref, x_sem)

    @pl.when(k == 0)
    def _():
        residual_copy().start()
        h_ref[...] = jnp.zeros_like(h_ref)

    for c in range(0, h_ref.shape[1], MM_COL_CHUNK):
        cols = slice(c, c + MM_COL_CHUNK)
        h_ref[:, cols] += (jnp.dot(ys_ref[...], wt_ref[:, cols], preferred_element_type=F32)
                           + jnp.dot(ya_ref[...], wb_ref[:, cols], preferred_element_type=F32))

    @pl.when(k == pl.num_programs(1) - 1)
    def _():
        residual_copy().wait()
        _residual_layer_norm(x_ref, h_ref, g_ref, b_ref, (h_ref, hb_ref))


def _outproj_ln(ys, ya, w_out, x, g, b, tm, tk):
    m, half = ys.shape
    d = w_out.shape[1]
    tm = min(tm, m)
    nk = half // tk
    row = pl.BlockSpec((1, d), lambda i, k: (0, 0))
    return pl.pallas_call(
        _outproj_kernel,
        grid=(m // tm, nk),
        in_specs=[pl.BlockSpec((tm, tk), lambda i, k: (i, k)),
                  pl.BlockSpec((tm, tk), lambda i, k: (i, k)),
                  pl.BlockSpec((tk, d), lambda i, k: (k, 0)),
                  pl.BlockSpec((tk, d), lambda i, k: (k + nk, 0)),
                  pl.BlockSpec(memory_space=pl.ANY),
                  row, row],
        out_specs=[pl.BlockSpec((tm, d), lambda i, k: (i, 0)),
                   pl.BlockSpec((tm, d), lambda i, k: (i, 0))],
        out_shape=[jax.ShapeDtypeStruct((m, d), F32), jax.ShapeDtypeStruct((m, d), BF16)],
        scratch_shapes=[pltpu.VMEM((tm, d), F32), pltpu.SemaphoreType.DMA(())],
        compiler_params=_cparams(2, VMEM_LIMIT_LARGE),
        name="outproj_ln",
    )(ys, ya, w_out, w_out, x, g, b)


FFN_TF = 256
FFN_TILES = D_FF // FFN_TF
FFN_TILES_PER_STEP = 2
FFN_STEPS = -(-FFN_TILES // FFN_TILES_PER_STEP)
FFN_TILES_PAD = FFN_STEPS * FFN_TILES_PER_STEP
FFN_HALO = 8


def _cast_up_tiles_kernel(a_ref, g_ref, o_ref):
    t = pl.program_id(0)

    @pl.when(t < FFN_TILES)
    def _():
        o_ref[:, 0:FFN_TF] = a_ref[...].astype(BF16)
        o_ref[:, FFN_TF:2 * FFN_TF] = g_ref[...].astype(BF16)

    @pl.when(t >= FFN_TILES)
    def _():
        o_ref[...] = jnp.zeros_like(o_ref)


def _cast_up_tiles(w_up):
    d = w_up.shape[0]
    last = FFN_TILES - 1
    return pl.pallas_call(
        _cast_up_tiles_kernel,
        grid=(FFN_TILES_PAD,),
        in_specs=[pl.BlockSpec((d, FFN_TF), lambda t: (0, jnp.minimum(t, last))),
                  pl.BlockSpec((d, FFN_TF), lambda t: (0, FFN_TILES + jnp.minimum(t, last)))],
        out_specs=pl.BlockSpec((None, d, 2 * FFN_TF), lambda t: (t, 0, 0)),
        out_shape=jax.ShapeDtypeStruct((FFN_TILES_PAD, d, 2 * FFN_TF), BF16),
        compiler_params=_cparams(1),
        name="cast_w_up",
    )(w_up, w_up)


def _cast_down_kernel(w_ref, o_ref):
    t = pl.program_id(0)

    @pl.when(t < FFN_TILES)
    def _():
        o_ref[...] = w_ref[...].astype(BF16)

    @pl.when(t >= FFN_TILES)
    def _():
        o_ref[...] = jnp.zeros_like(o_ref)


def _cast_down(w_down):
    d = w_down.shape[1]
    return pl.pallas_call(
        _cast_down_kernel,
        grid=(FFN_TILES_PAD,),
        in_specs=[pl.BlockSpec((FFN_TF, d), lambda t: (jnp.minimum(t, FFN_TILES - 1), 0))],
        out_specs=pl.BlockSpec((FFN_TF, d), lambda t: (t, 0)),
        out_shape=jax.ShapeDtypeStruct((FFN_TILES_PAD * FFN_TF, d), BF16),
        compiler_params=_cparams(1),
        name="cast_w_down",
    )(w_down)


def _ffn_kernel(h_ref, wu_ref, cp_ref, wd_ref, r_hbm, lg_ref, lb_ref, o_ref, halo_ref, r_ref, r_sem):
    n = FFN_TILES_PER_STEP
    i = pl.program_id(0)
    f = pl.program_id(1)
    tm = h_ref.shape[0]

    def residual_copy():
        rows = pl.ds(pl.multiple_of(i * tm, tm), tm)
        return pltpu.make_async_copy(r_hbm.at[rows, :], r_ref, r_sem)

    @pl.when(f == 0)
    def _():
        residual_copy().start()
        o_ref[...] = jnp.zeros_like(o_ref)

    h = h_ref[...]
    row = lax.broadcasted_iota(jnp.int32, (tm, FFN_TF), 0)
    acts = []
    for k in range(n):
        tile = f * n + k
        cols = slice(k * FFN_TF, (k + 1) * FFN_TF)
        value = jnp.dot(h, wu_ref[k, :, 0:FFN_TF], preferred_element_type=F32)
        gate = jnp.dot(h, wu_ref[k, :, FFN_TF:2 * FFN_TF], preferred_element_type=F32)
        halo = jnp.where(i == 0, 0.0, halo_ref[tile])
        halo_ref[tile] = gate[tm - FFN_HALO:tm, :]
        prev1 = jnp.where(row == 0, halo[FFN_HALO - 1:FFN_HALO, :], pltpu.roll(gate, 1, 0))
        prev2 = jnp.where(row == 0, halo[FFN_HALO - 2:FFN_HALO - 1, :],
                          jnp.where(row == 1, halo[FFN_HALO - 1:FFN_HALO, :], pltpu.roll(gate, 2, 0)))
        gc = (cp_ref[3:4, cols] + prev2 * cp_ref[0:1, cols] + prev1 * cp_ref[1:2, cols]
              + gate * cp_ref[2:3, cols])
        acts.append(((gc * jax.nn.sigmoid(gc)) * value).astype(BF16))

    for c in range(0, o_ref.shape[1], MM_COL_CHUNK):
        cols = slice(c, c + MM_COL_CHUNK)
        contrib = jnp.dot(acts[0], wd_ref[0:FFN_TF, cols], preferred_element_type=F32)
        for k in range(1, n):
            contrib += jnp.dot(acts[k], wd_ref[k * FFN_TF:(k + 1) * FFN_TF, cols],
                               preferred_element_type=F32)
        o_ref[:, cols] += contrib

    @pl.when(f == pl.num_programs(1) - 1)
    def _():
        residual_copy().wait()
        _residual_layer_norm(r_ref, o_ref, lg_ref, lb_ref, (o_ref,))


def _ffn_ln(h_bf, w_up_tiles, conv_params, w_down_pad, h1, g, b, tm):
    m, d = h_bf.shape
    tm = min(tm, m)
    n = FFN_TILES_PER_STEP
    once = pl.Buffered(1)
    row = pl.BlockSpec((1, d), lambda i, f: (0, 0))
    return pl.pallas_call(
        _ffn_kernel,
        grid=(m // tm, FFN_STEPS),
        in_specs=[pl.BlockSpec((tm, d), lambda i, f: (i, 0), pipeline_mode=once),
                  pl.BlockSpec((n, d, 2 * FFN_TF), lambda i, f: (f, 0, 0)),
                  pl.BlockSpec((4, n * FFN_TF), lambda i, f: (0, f)),
                  pl.BlockSpec((n * FFN_TF, d), lambda i, f: (f, 0)),
                  pl.BlockSpec(memory_space=pl.ANY),
                  row, row],
        out_specs=pl.BlockSpec((tm, d), lambda i, f: (i, 0), pipeline_mode=once),
        out_shape=jax.ShapeDtypeStruct((m, d), F32),
        scratch_shapes=[pltpu.VMEM((FFN_TILES_PAD, FFN_HALO, FFN_TF), F32),
                        pltpu.VMEM((tm, d), F32),
                        pltpu.SemaphoreType.DMA(())],
        compiler_params=_cparams(2, VMEM_LIMIT_LARGE),
        name="ffn_ln",
    )(h_bf, w_up_tiles, conv_params, w_down_pad, h1, g, b)


def kernel(x, w_in, ssm_log_step, ssm_lambda_re, ssm_lambda_im, ssm_b_re, ssm_b_im, ssm_c_re, ssm_c_im, ssm_d, ssm_w_glu, ssm_b_glu, att_lambda_q1, att_lambda_k1, att_lambda_q2, att_lambda_k2, att_subln_g, rel_bias, w_out, ln1_g, ln1_b, ffn_w_up, ffn_conv_w, ffn_conv_b, ffn_w_down, ln2_g, ln2_b):
    bsz, seq, _ = x.shape
    assert bsz == 1 and DEPTH == 1
    l = 0
    xs = x[0]

    qk_scale = ATT_QK_DIM ** -0.5 * LOG2_E
    col_scale = np.ones((1, w_in.shape[2]), np.float32)
    col_scale[:, SSM_WIDTH:SSM_WIDTH + QK_WIDTH] = qk_scale
    w_in_bf = (w_in[l] * jnp.asarray(col_scale)).astype(BF16)
    x_bf = xs.astype(BF16)

    u = _matmul(x_bf, w_in_bf, F32, 1024, 1024, 0, SSM_WIDTH)
    qkv = _matmul(x_bf, w_in_bf, BF16, 1024, 1024, SSM_WIDTH, w_in.shape[2] - SSM_WIDTH)

    lam, b_blocks, c_blocks = _ssm_parameter_layout(
        ssm_log_step[l], ssm_lambda_re[l], ssm_lambda_im[l], ssm_b_re[l], ssm_b_im[l],
        ssm_c_re[l], ssm_c_im[l])
    dskip = ssm_d[l].astype(F32).reshape(N_LANE_TILES, 1, LANES)
    y = _ssm_mixer(u, lam, b_blocks, c_blocks, dskip)
    y_ssm = _glu(y, ssm_w_glu[l].astype(BF16), ssm_b_glu[l].astype(F32).reshape(1, SSM_WIDTH), 512)

    vec = lambda a: a.astype(F32).reshape(1, ATT_QK_DIM)
    y_att = _diff_attention(qkv, rel_bias,
                            vec(att_lambda_q1[l]), vec(att_lambda_k1[l]),
                            vec(att_lambda_q2[l]), vec(att_lambda_k2[l]),
                            att_subln_g[l].astype(F32).reshape(1, ATT_V_DIM))

    h1, h1_bf = _outproj_ln(y_ssm, y_att, w_out[l].astype(BF16), xs,
                            ln1_g[l].reshape(1, D_MODEL), ln1_b[l].reshape(1, D_MODEL), 512, 512)

    pad = FFN_TILES_PAD * FFN_TF - D_FF
    conv_params = jnp.pad(jnp.concatenate([ffn_conv_w[l].astype(F32), ffn_conv_b[l].astype(F32)[None]], axis=0),
                          ((0, 0), (0, pad)))
    out = _ffn_ln(h1_bf, _cast_up_tiles(ffn_w_up[l]), conv_params, _cast_down(ffn_w_down[l]), h1,
                  ln2_g[l].reshape(1, D_MODEL), ln2_b[l].reshape(1, D_MODEL), 512)
    return out[None]
```

```python
import functools
import math

import numpy as np
import jax
import jax.numpy as jnp
from jax import lax
from jax.experimental import pallas as pl
from jax.experimental.pallas import tpu as pltpu

F32 = jnp.float32
BF16 = jnp.bfloat16

D_MODEL = 4096
CHUNK = 64
SSM_WIDTH = 2048
SSM_GROUP = 16
SSM_GROUPS = SSM_WIDTH // SSM_GROUP
SSM_STATE = 64
ATT_QK_DIM = 128
ATT_V_DIM = 256
ATT_WIDTH = 2048
ATT_HEADS = 8
QK_WIDTH = 2048
D_FF = 11008
REL_BUCKETS = 32
REL_MAX_DIST = 128
DEPTH = 1
ALPHA = (2 * DEPTH) ** 0.25
LN_EPS = 1e-5
NEG_INF = -1e30
LAMBDA_INIT = 0.8 - 0.6 * math.exp(-0.3 * 0)

LANES = 128
SSM_L = 16
GROUPS_PER_TILE = LANES // SSM_GROUP
N_LANE_TILES = SSM_WIDTH // LANES
STATE_LANES = GROUPS_PER_TILE * SSM_STATE
VMEM_LIMIT = 56 * 1024 * 1024
VMEM_LIMIT_LARGE = 62 * 1024 * 1024


def _cparams(n_axes, vmem=VMEM_LIMIT):
    return pltpu.CompilerParams(dimension_semantics=("arbitrary",) * n_axes,
                                vmem_limit_bytes=vmem)


def _matmul_kernel(x_ref, w_ref, o_ref):
    o_ref[...] = jnp.dot(x_ref[...], w_ref[...], preferred_element_type=F32).astype(o_ref.dtype)


def _matmul(x, w, out_dtype, tm, tn, col_start=0, n_cols=None):
    m, kdim = x.shape
    n = w.shape[1] - col_start if n_cols is None else n_cols
    tm, tn = min(tm, m), min(tn, n)
    assert m % tm == 0 and n % tn == 0 and col_start % tn == 0
    j0 = col_start // tn
    return pl.pallas_call(
        _matmul_kernel,
        grid=(m // tm, n // tn),
        in_specs=[pl.BlockSpec((tm, kdim), lambda i, j: (i, 0)),
                  pl.BlockSpec((kdim, tn), lambda i, j: (0, j + j0))],
        out_specs=pl.BlockSpec((tm, tn), lambda i, j: (i, j)),
        out_shape=jax.ShapeDtypeStruct((m, n), out_dtype),
        compiler_params=_cparams(2),
        name="matmul",
    )(x, w)


def _gelu_tanh(x):
    c = math.sqrt(2.0 / math.pi)
    return 0.5 * x * (1.0 + jnp.tanh(c * (x + 0.044715 * (x * x * x))))


def _ssm_kernel(u_ref, lam_ref, b_ref, c_ref, dskip_ref, y_ref,
                urev_ref, e_ref, hin_ref, p_ref, qt_ref, d_ref):
    n_chunks = u_ref.shape[0] // SSM_L
    contract_last = (((1,), (1,)), ((), ()))

    lr = lam_ref[0:1, :]
    li = lam_ref[1:2, :]
    step = lam_ref[2:3, :]
    mag = jnp.exp(lr * step)
    a_re = mag * jnp.cos(li * step)
    a_im = mag * jnp.sin(li * step)
    den = lr * lr + li * li
    z_re = ((a_re - 1.0) * lr + a_im * li) / den
    z_im = (a_im * lr - (a_re - 1.0) * li) / den

    b_re, b_im = b_ref[0], b_ref[1]
    c_re, c_im = c_ref[0], c_ref[1]
    c0 = jnp.concatenate([c_re, -c_im], axis=1).astype(BF16)
    w_re, w_im = z_re, z_im
    pw_re, pw_im = a_re, a_im
    for s in range(SSM_L):
        rows = slice(s * LANES, (s + 1) * LANES)
        p_ref[rows, 0:STATE_LANES] = (w_re * b_re - w_im * b_im).astype(BF16)
        p_ref[rows, STATE_LANES:2 * STATE_LANES] = (w_re * b_im + w_im * b_re).astype(BF16)
        qt_ref[rows, 0:STATE_LANES] = (c_re * pw_re - c_im * pw_im).astype(BF16)
        qt_ref[rows, STATE_LANES:2 * STATE_LANES] = (-(c_re * pw_im + c_im * pw_re)).astype(BF16)
        w_re, w_im = w_re * a_re - w_im * a_im, w_re * a_im + w_im * a_re
        if s + 1 < SSM_L:
            pw_re, pw_im = pw_re * a_re - pw_im * a_im, pw_re * a_im + pw_im * a_re
    d = lax.dot_general(p_ref[...], c0, contract_last, preferred_element_type=F32).astype(BF16)
    d_ref[:, LANES:2 * LANES] = d
    d_ref[0:LANES, 0:LANES] = jnp.zeros((LANES, LANES), BF16)
    d_ref[LANES:SSM_L * LANES, 0:LANES] = d[0:(SSM_L - 1) * LANES, :]

    for i in range(SSM_L):
        rows = u_ref[pl.ds(i, n_chunks, stride=SSM_L), :]
        urev_ref[:, (SSM_L - 1 - i) * LANES:(SSM_L - i) * LANES] = rows.astype(BF16)

    e_ref[...] = jnp.dot(urev_ref[...], p_ref[...], preferred_element_type=F32)

    def chunk_step(c, carry):
        h_re, h_im = carry
        hin_ref[pl.ds(c, 1), 0:STATE_LANES] = h_re
        hin_ref[pl.ds(c, 1), STATE_LANES:2 * STATE_LANES] = h_im
        e_re = e_ref[pl.ds(c, 1), 0:STATE_LANES]
        e_im = e_ref[pl.ds(c, 1), STATE_LANES:2 * STATE_LANES]
        return (pw_re * h_re - pw_im * h_im + e_re,
                pw_re * h_im + pw_im * h_re + e_im)

    zero = jnp.zeros((1, STATE_LANES), F32)
    lax.fori_loop(0, n_chunks, chunk_step, (zero, zero))

    carry_in = lax.dot_general(hin_ref[...].astype(BF16), qt_ref[...], contract_last,
                               preferred_element_type=F32)

    dskip = dskip_ref[...]
    for i0 in range(0, SSM_L, 2):
        intra = jnp.dot(urev_ref[:, (SSM_L - 2 - i0) * LANES:], d_ref[0:(i0 + 2) * LANES, :],
                        preferred_element_type=F32)
        for i in (i0, i0 + 1):
            u_i = u_ref[pl.ds(i, n_chunks, stride=SSM_L), :]
            y = (intra[:, (i - i0) * LANES:(i - i0 + 1) * LANES] + carry_in[:, i * LANES:(i + 1) * LANES]
                 + dskip * u_i)
            y_ref[pl.ds(i, n_chunks, stride=SSM_L), :] = _gelu_tanh(y)


def _ssm_mixer(u, lam, b_blocks, c_blocks, dskip):
    seq = u.shape[0]
    n_chunks = seq // SSM_L
    return pl.pallas_call(
        _ssm_kernel,
        grid=(N_LANE_TILES,),
        in_specs=[pl.BlockSpec((seq, LANES), lambda j: (0, j)),
                  pl.BlockSpec((None, 3, STATE_LANES), lambda j: (j, 0, 0)),
                  pl.BlockSpec((None, 2, LANES, STATE_LANES), lambda j: (j, 0, 0, 0)),
                  pl.BlockSpec((None, 2, LANES, STATE_LANES), lambda j: (j, 0, 0, 0)),
                  pl.BlockSpec((None, 1, LANES), lambda j: (j, 0, 0))],
        out_specs=pl.BlockSpec((seq, LANES), lambda j: (0, j)),
        out_shape=jax.ShapeDtypeStruct((seq, SSM_WIDTH), F32),
        scratch_shapes=[pltpu.VMEM((n_chunks, SSM_L * LANES), BF16),
                        pltpu.VMEM((n_chunks, 2 * STATE_LANES), F32),
                        pltpu.VMEM((n_chunks, 2 * STATE_LANES), F32),
                        pltpu.VMEM((SSM_L * LANES, 2 * STATE_LANES), BF16),
                        pltpu.VMEM((SSM_L * LANES, 2 * STATE_LANES), BF16),
                        pltpu.VMEM((SSM_L * LANES, 2 * LANES), BF16)],
        compiler_params=_cparams(1),
        name="ssm_mixer",
    )(u, lam, b_blocks, c_blocks, dskip)


def _ssm_parameter_layout(log_step, lam_re, lam_im, b_re, b_im, c_re, c_im):
    nt, gl = N_LANE_TILES, GROUPS_PER_TILE
    step = jnp.broadcast_to(jnp.exp(log_step.astype(F32))[:, None], lam_re.shape)
    lam = jnp.stack([lam_re.astype(F32).reshape(nt, STATE_LANES),
                     lam_im.astype(F32).reshape(nt, STATE_LANES),
                     step.reshape(nt, STATE_LANES)], axis=1)
    same_group = jnp.asarray(np.arange(LANES)[:, None] // SSM_GROUP == np.arange(STATE_LANES)[None, :] // SSM_STATE)

    def blocks(x):
        rows = x.astype(F32).reshape(nt, LANES, SSM_STATE)
        return jnp.where(same_group, jnp.tile(rows, (1, 1, gl)), 0.0)

    b_blocks = jnp.stack([blocks(b_re.transpose(0, 2, 1)), blocks(b_im.transpose(0, 2, 1))], axis=1)
    c_blocks = jnp.stack([blocks(c_re), blocks(c_im)], axis=1)
    return lam, b_blocks, c_blocks


def _glu_kernel(y_ref, w_ref, b_ref, o_ref):
    y = y_ref[...]
    z = jnp.dot(y.astype(BF16), w_ref[...], preferred_element_type=F32) + b_ref[...]
    o_ref[...] = (y * jax.nn.sigmoid(z)).astype(o_ref.dtype)


def _glu(y, w, b, tm):
    m, n = y.shape
    tm = min(tm, m)
    return pl.pallas_call(
        _glu_kernel,
        grid=(m // tm,),
        in_specs=[pl.BlockSpec((tm, n), lambda i: (i, 0)),
                  pl.BlockSpec((n, n), lambda i: (0, 0)),
                  pl.BlockSpec((1, n), lambda i: (0, 0))],
        out_specs=pl.BlockSpec((tm, n), lambda i: (i, 0)),
        out_shape=jax.ShapeDtypeStruct((m, n), BF16),
        compiler_params=_cparams(1),
        name="glu",
    )(y, w, b)


ATT_TQ = 512
LOG2_E = math.log2(math.e)
FAR_BUCKET = REL_BUCKETS // 2 - 1


def _attn_kernel(rel_ref, q_ref, k_ref, v_ref, bucket_ref, lq1_ref, lk1_ref, lq2_ref, lk2_ref, g_ref,
                 o_ref, m_ref, l_ref, acc_ref, bias_ref, sa_ref, sb_ref):
    h = pl.program_id(0)
    i = pl.program_id(1)
    tq = ATT_TQ

    @pl.when(i == 0)
    def _():
        far = rel_ref[FAR_BUCKET, h]

        def rows(r, carry):
            r0 = pl.multiple_of(r * 8, 8)
            bucket = bucket_ref[pl.ds(r0, 8), :]
            tile = jnp.where(bucket < 0, NEG_INF, 0.0)
            for b in range(REL_BUCKETS):
                tile = jnp.where(bucket == b, (rel_ref[b, h] - far) * LOG2_E, tile)
            bias_ref[pl.ds(r0, 8), :] = tile
            return carry

        lax.fori_loop(0, tq // 8, rows, 0)

    m_ref[...] = jnp.full_like(m_ref, NEG_INF)
    l_ref[...] = jnp.zeros_like(l_ref)
    acc_ref[...] = jnp.zeros_like(acc_ref)

    def scores(start, width, s_ref, bias):
        kb = k_ref[pl.ds(start, width), :]
        for c in range(2):
            q = q_ref[:, c * ATT_QK_DIM:(c + 1) * ATT_QK_DIM]
            kc = kb[:, c * ATT_QK_DIM:(c + 1) * ATT_QK_DIM]
            s = lax.dot_general(q, kc, (((1,), (1,)), ((), ())), preferred_element_type=F32)
            s_ref[c, :, 0:width] = s if bias is None else s + bias

    def absorb(start, width, s_ref):
        vb = v_ref[pl.ds(start, width), :]
        for c in range(2):
            tiles = [s_ref[c, :, t * LANES:(t + 1) * LANES] for t in range(width // LANES)]
            m_tile = functools.reduce(jnp.maximum, tiles)
            m_prev = m_ref[c]
            m_new = jnp.maximum(m_prev, jnp.max(m_tile, axis=1, keepdims=True))
            scale = jnp.exp2(m_prev - m_new)
            ps = [jnp.exp2(t - m_new) for t in tiles]
            l_ref[c] = scale * l_ref[c] + functools.reduce(jnp.add, ps)
            m_ref[c] = m_new
            pv = jnp.dot(jnp.concatenate(ps, axis=1).astype(BF16), vb, preferred_element_type=F32)
            for t in range(ATT_V_DIM // LANES):
                cols = slice(t * LANES, (t + 1) * LANES)
                acc_ref[c, :, cols] = scale * acc_ref[c, :, cols] + pv[:, cols]

    wide = 2 * tq

    @pl.when(i == 0)
    def _():
        scores(0, tq, sa_ref, bias_ref[:, tq:wide])
        absorb(0, tq, sa_ref)

    @pl.when(i >= 1)
    def _():
        near = pl.multiple_of((i - 1) * tq, tq)
        scores(near, wide, sa_ref, bias_ref[...])
        n_far = i - 1
        n_wide = lax.shift_right_logical(n_far, 1)
        has_narrow = lax.rem(n_far, 2) == 1
        n_loop = lax.shift_right_logical(n_wide, 1)
        far = lambda p: pl.multiple_of(p * wide, wide)

        def two_wide(r, carry):
            absorb(jnp.where(r == 0, near, far(2 * r - 1)), wide, sa_ref)
            scores(far(2 * r), wide, sb_ref, None)
            absorb(far(2 * r), wide, sb_ref)
            scores(far(2 * r + 1), wide, sa_ref, None)
            return carry

        lax.fori_loop(0, n_loop, two_wide, 0)
        pending = jnp.where(n_loop == 0, near, far(2 * n_loop - 1))
        odd_wide = lax.rem(n_wide, 2) == 1
        last_wide = far(n_wide - 1)
        narrow = pl.multiple_of(n_wide * wide, tq)

        @pl.when(jnp.logical_and(odd_wide, has_narrow))
        def _():
            absorb(pending, wide, sa_ref)
            scores(last_wide, wide, sb_ref, None)
            absorb(last_wide, wide, sb_ref)
            scores(narrow, tq, sa_ref, None)
            absorb(narrow, tq, sa_ref)

        @pl.when(jnp.logical_and(odd_wide, jnp.logical_not(has_narrow)))
        def _():
            absorb(pending, wide, sa_ref)
            scores(last_wide, wide, sb_ref, None)
            absorb(last_wide, wide, sb_ref)

        @pl.when(jnp.logical_and(jnp.logical_not(odd_wide), has_narrow))
        def _():
            absorb(pending, wide, sa_ref)
            scores(narrow, tq, sb_ref, None)
            absorb(narrow, tq, sb_ref)

        @pl.when(jnp.logical_and(jnp.logical_not(odd_wide), jnp.logical_not(has_narrow)))
        def _():
            absorb(pending, wide, sa_ref)

    lam = (jnp.exp(jnp.sum(lq1_ref[...] * lk1_ref[...], axis=1, keepdims=True))
           - jnp.exp(jnp.sum(lq2_ref[...] * lk2_ref[...], axis=1, keepdims=True)) + LAMBDA_INIT)
    l0 = jnp.sum(l_ref[0], axis=1, keepdims=True)
    l1 = jnp.sum(l_ref[1], axis=1, keepdims=True)
    o = acc_ref[0] / l0 - lam * (acc_ref[1] / l1)
    o = o * lax.rsqrt(jnp.mean(o * o, axis=1, keepdims=True) + LN_EPS) * g_ref[...]
    o_ref[...] = (o * (1.0 - LAMBDA_INIT)).astype(o_ref.dtype)


def _t5_bucket(rel):
    half = REL_BUCKETS // 2
    max_exact = half // 2
    ret = jnp.where(rel > 0, half, 0)
    n = jnp.abs(rel)
    nf = jnp.maximum(n, 1).astype(jnp.float32)
    large = max_exact + (jnp.log(nf / max_exact) / math.log(REL_MAX_DIST / max_exact)
                         * (half - max_exact)).astype(jnp.int32)
    large = jnp.minimum(large, half - 1)
    return ret + jnp.where(n < max_exact, n, large)


def _near_buckets():
    tq = ATT_TQ
    qpos = np.arange(tq)[:, None]
    kpos = np.arange(-tq, tq)[None, :]
    rel = jnp.asarray(kpos - qpos, jnp.int32)
    visible = jnp.asarray((kpos // CHUNK) <= (qpos // CHUNK))
    return jnp.where(visible, _t5_bucket(rel), -1).astype(jnp.int32)


def _diff_attention(qkv, rel_bias, lq1, lk1, lq2, lk2, subln_g):
    seq = qkv.shape[0]
    tq = ATT_TQ
    head_blk = 2 * ATT_QK_DIM
    vec = pl.BlockSpec((1, ATT_QK_DIM), lambda h, i: (0, 0))
    return pl.pallas_call(
        _attn_kernel,
        grid=(ATT_HEADS, seq // tq),
        in_specs=[pl.BlockSpec(memory_space=pltpu.SMEM),
                  pl.BlockSpec((tq, head_blk), lambda h, i: (i, h)),
                  pl.BlockSpec((seq, head_blk), lambda h, i: (0, ATT_HEADS + h)),
                  pl.BlockSpec((seq, ATT_V_DIM), lambda h, i: (0, 2 * ATT_HEADS + h)),
                  pl.BlockSpec((tq, 2 * tq), lambda h, i: (0, 0)),
                  vec, vec, vec, vec,
                  pl.BlockSpec((1, ATT_V_DIM), lambda h, i: (0, 0))],
        out_specs=pl.BlockSpec((tq, ATT_V_DIM), lambda h, i: (i, h)),
        out_shape=jax.ShapeDtypeStruct((seq, ATT_WIDTH), BF16),
        scratch_shapes=[pltpu.VMEM((2, tq, LANES), F32),
                        pltpu.VMEM((2, tq, LANES), F32),
                        pltpu.VMEM((2, tq, ATT_V_DIM), F32),
                        pltpu.VMEM((tq, 2 * tq), F32),
                        pltpu.VMEM((2, tq, 2 * tq), F32),
                        pltpu.VMEM((2, tq, 2 * tq), F32)],
        compiler_params=_cparams(2),
        name="diff_attention",
    )(rel_bias.astype(F32), qkv, qkv, qkv, _near_buckets(), lq1, lk1, lq2, lk2, subln_g)


MM_COL_CHUNK = 1024
LN_ROW_CHUNK = 128


def _residual_layer_norm(res_ref, acc_ref, g_ref, b_ref, out_refs):
    g = g_ref[...]
    b = b_ref[...]
    for r0 in range(0, acc_ref.shape[0], LN_ROW_CHUNK):
        rows = slice(r0, r0 + LN_ROW_CHUNK)
        r = ALPHA * res_ref[rows, :] + acc_ref[rows, :]
        mu = jnp.mean(r, axis=1, keepdims=True)
        xc = r - mu
        var = jnp.mean(xc * xc, axis=1, keepdims=True)
        h = xc * lax.rsqrt(var + LN_EPS) * g + b
        for o_ref in out_refs:
            o_ref[rows, :] = h.astype(o_ref.dtype)


def _outproj_kernel(ys_ref, ya_ref, wt_ref, wb_ref, x_hbm, g_ref, b_ref, h_ref, hb_ref, x_ref, x_sem):
    i = pl.program_id(0)
    k = pl.program_id(1)
    tm = h_ref.shape[0]

    def residual_copy():
        rows = pl.ds(pl.multiple_of(i * tm, tm), tm)
        return pltpu.make_async_copy(x_hbm.at[rows, :], x_ref, x_sem)

    @pl.when(k == 0)
    def _():
        residual_copy().start()
        h_ref[...] = jnp.zeros_like(h_ref)

    for c in range(0, h_ref.shape[1], MM_COL_CHUNK):
        cols = slice(c, c + MM_COL_CHUNK)
        h_ref[:, cols] += (jnp.dot(ys_ref[...], wt_ref[:, cols], preferred_element_type=F32)
                           + jnp.dot(ya_ref[...], wb_ref[:, cols], preferred_element_type=F32))

    @pl.when(k == pl.num_programs(1) - 1)
    def _():
        residual_copy().wait()
        _residual_layer_norm(x_ref, h_ref, g_ref, b_ref, (h_ref, hb_ref))


def _outproj_ln(ys, ya, w_out, x, g, b, tm, tk):
    m, half = ys.shape
    d = w_out.shape[1]
    tm = min(tm, m)
    nk = half // tk
    row = pl.BlockSpec((1, d), lambda i, k: (0, 0))
    return pl.pallas_call(
        _outproj_kernel,
        grid=(m // tm, nk),
        in_specs=[pl.BlockSpec((tm, tk), lambda i, k: (i, k)),
                  pl.BlockSpec((tm, tk), lambda i, k: (i, k)),
                  pl.BlockSpec((tk, d), lambda i, k: (k, 0)),
                  pl.BlockSpec((tk, d), lambda i, k: (k + nk, 0)),
                  pl.BlockSpec(memory_space=pl.ANY),
                  row, row],
        out_specs=[pl.BlockSpec((tm, d), lambda i, k: (i, 0)),
                   pl.BlockSpec((tm, d), lambda i, k: (i, 0))],
        out_shape=[jax.ShapeDtypeStruct((m, d), F32), jax.ShapeDtypeStruct((m, d), BF16)],
        scratch_shapes=[pltpu.VMEM((tm, d), F32), pltpu.SemaphoreType.DMA(())],
        compiler_params=_cparams(2, VMEM_LIMIT_LARGE),
        name="outproj_ln",
    )(ys, ya, w_out, w_out, x, g, b)


FFN_TF = 256
FFN_TILES = D_FF // FFN_TF
FFN_TILES_PER_STEP = 2
FFN_STEPS = -(-FFN_TILES // FFN_TILES_PER_STEP)
FFN_TILES_PAD = FFN_STEPS * FFN_TILES_PER_STEP
FFN_HALO = 8


def _cast_up_tiles_kernel(a_ref, g_ref, o_ref):
    t = pl.program_id(0)

    @pl.when(t < FFN_TILES)
    def _():
        o_ref[:, 0:FFN_TF] = a_ref[...].astype(BF16)
        o_ref[:, FFN_TF:2 * FFN_TF] = g_ref[...].astype(BF16)

    @pl.when(t >= FFN_TILES)
    def _():
        o_ref[...] = jnp.zeros_like(o_ref)


def _cast_up_tiles(w_up):
    d = w_up.shape[0]
    last = FFN_TILES - 1
    return pl.pallas_call(
        _cast_up_tiles_kernel,
        grid=(FFN_TILES_PAD,),
        in_specs=[pl.BlockSpec((d, FFN_TF), lambda t: (0, jnp.minimum(t, last))),
                  pl.BlockSpec((d, FFN_TF), lambda t: (0, FFN_TILES + jnp.minimum(t, last)))],
        out_specs=pl.BlockSpec((None, d, 2 * FFN_TF), lambda t: (t, 0, 0)),
        out_shape=jax.ShapeDtypeStruct((FFN_TILES_PAD, d, 2 * FFN_TF), BF16),
        compiler_params=_cparams(1),
        name="cast_w_up",
    )(w_up, w_up)


def _cast_down_kernel(w_ref, o_ref):
    t = pl.program_id(0)

    @pl.when(t < FFN_TILES)
    def _():
        o_ref[...] = w_ref[...].astype(BF16)

    @pl.when(t >= FFN_TILES)
    def _():
        o_ref[...] = jnp.zeros_like(o_ref)


def _cast_down(w_down):
    d = w_down.shape[1]
    return pl.pallas_call(
        _cast_down_kernel,
        grid=(FFN_TILES_PAD,),
        in_specs=[pl.BlockSpec((FFN_TF, d), lambda t: (jnp.minimum(t, FFN_TILES - 1), 0))],
        out_specs=pl.BlockSpec((FFN_TF, d), lambda t: (t, 0)),
        out_shape=jax.ShapeDtypeStruct((FFN_TILES_PAD * FFN_TF, d), BF16),
        compiler_params=_cparams(1),
        name="cast_w_down",
    )(w_down)


def _ffn_kernel(h_ref, wu_ref, cp_ref, wd_ref, r_hbm, lg_ref, lb_ref, o_ref, halo_ref, r_ref, r_sem):
    n = FFN_TILES_PER_STEP
    i = pl.program_id(0)
    f = pl.program_id(1)
    tm = h_ref.shape[0]

    def residual_copy():
        rows = pl.ds(pl.multiple_of(i * tm, tm), tm)
        return pltpu.make_async_copy(r_hbm.at[rows, :], r_ref, r_sem)

    @pl.when(f == 0)
    def _():
        residual_copy().start()
        o_ref[...] = jnp.zeros_like(o_ref)

    h = h_ref[...]
    row = lax.broadcasted_iota(jnp.int32, (tm, FFN_TF), 0)
    acts = []
    for k in range(n):
        tile = f * n + k
        cols = slice(k * FFN_TF, (k + 1) * FFN_TF)
        value = jnp.dot(h, wu_ref[k, :, 0:FFN_TF], preferred_element_type=F32)
        gate = jnp.dot(h, wu_ref[k, :, FFN_TF:2 * FFN_TF], preferred_element_type=F32)
        halo = jnp.where(i == 0, 0.0, halo_ref[tile])
        halo_ref[tile] = gate[tm - FFN_HALO:tm, :]
        prev1 = jnp.where(row == 0, halo[FFN_HALO - 1:FFN_HALO, :], pltpu.roll(gate, 1, 0))
        prev2 = jnp.where(row == 0, halo[FFN_HALO - 2:FFN_HALO - 1, :],
                          jnp.where(row == 1, halo[FFN_HALO - 1:FFN_HALO, :], pltpu.roll(gate, 2, 0)))
        gc = (cp_ref[3:4, cols] + prev2 * cp_ref[0:1, cols] + prev1 * cp_ref[1:2, cols]
              + gate * cp_ref[2:3, cols])
        acts.append(((gc * jax.nn.sigmoid(gc)) * value).astype(BF16))

    for c in range(0, o_ref.shape[1], MM_COL_CHUNK):
        cols = slice(c, c + MM_COL_CHUNK)
        contrib = jnp.dot(acts[0], wd_ref[0:FFN_TF, cols], preferred_element_type=F32)
        for k in range(1, n):
            contrib += jnp.dot(acts[k], wd_ref[k * FFN_TF:(k + 1) * FFN_TF, cols],
                               preferred_element_type=F32)
        o_ref[:, cols] += contrib

    @pl.when(f == pl.num_programs(1) - 1)
    def _():
        residual_copy().wait()
        _residual_layer_norm(r_ref, o_ref, lg_ref, lb_ref, (o_ref,))


def _ffn_ln(h_bf, w_up_tiles, conv_params, w_down_pad, h1, g, b, tm):
    m, d = h_bf.shape
    tm = min(tm, m)
    n = FFN_TILES_PER_STEP
    once = pl.Buffered(1)
    row = pl.BlockSpec((1, d), lambda i, f: (0, 0))
    return pl.pallas_call(
        _ffn_kernel,
        grid=(m // tm, FFN_STEPS),
        in_specs=[pl.BlockSpec((tm, d), lambda i, f: (i, 0), pipeline_mode=once),
                  pl.BlockSpec((n, d, 2 * FFN_TF), lambda i, f: (f, 0, 0)),
                  pl.BlockSpec((4, n * FFN_TF), lambda i, f: (0, f)),
                  pl.BlockSpec((n * FFN_TF, d), lambda i, f: (f, 0)),
                  pl.BlockSpec(memory_space=pl.ANY),
                  row, row],
        out_specs=pl.BlockSpec((tm, d), lambda i, f: (i, 0), pipeline_mode=once),
        out_shape=jax.ShapeDtypeStruct((m, d), F32),
        scratch_shapes=[pltpu.VMEM((FFN_TILES_PAD, FFN_HALO, FFN_TF), F32),
                        pltpu.VMEM((tm, d), F32),
                        pltpu.SemaphoreType.DMA(())],
        compiler_params=_cparams(2, VMEM_LIMIT_LARGE),
        name="ffn_ln",
    )(h_bf, w_up_tiles, conv_params, w_down_pad, h1, g, b)


def kernel(x, w_in, ssm_log_step, ssm_lambda_re, ssm_lambda_im, ssm_b_re, ssm_b_im, ssm_c_re, ssm_c_im, ssm_d, ssm_w_glu, ssm_b_glu, att_lambda_q1, att_lambda_k1, att_lambda_q2, att_lambda_k2, att_subln_g, rel_bias, w_out, ln1_g, ln1_b, ffn_w_up, ffn_conv_w, ffn_conv_b, ffn_w_down, ln2_g, ln2_b):
    bsz, seq, _ = x.shape
    assert bsz == 1 and DEPTH == 1
    l = 0
    xs = x[0]

    qk_scale = ATT_QK_DIM ** -0.5 * LOG2_E
    col_scale = np.ones((1, w_in.shape[2]), np.float32)
    col_scale[:, SSM_WIDTH:SSM_WIDTH + QK_WIDTH] = qk_scale
    w_in_bf = (w_in[l] * jnp.asarray(col_scale)).astype(BF16)
    x_bf = xs.astype(BF16)

    u = _matmul(x_bf, w_in_bf, F32, 1024, 1024, 0, SSM_WIDTH)
    qkv = _matmul(x_bf, w_in_bf, BF16, 1024, 1024, SSM_WIDTH, w_in.shape[2] - SSM_WIDTH)

    lam, b_blocks, c_blocks = _ssm_parameter_layout(
        ssm_log_step[l], ssm_lambda_re[l], ssm_lambda_im[l], ssm_b_re[l], ssm_b_im[l],
        ssm_c_re[l], ssm_c_im[l])
    dskip = ssm_d[l].astype(F32).reshape(N_LANE_TILES, 1, LANES)
    y = _ssm_mixer(u, lam, b_blocks, c_blocks, dskip)
    y_ssm = _glu(y, ssm_w_glu[l].astype(BF16), ssm_b_glu[l].astype(F32).reshape(1, SSM_WIDTH), 512)

    vec = lambda a: a.astype(F32).reshape(1, ATT_QK_DIM)
    y_att = _diff_attention(qkv, rel_bias,
                            vec(att_lambda_q1[l]), vec(att_lambda_k1[l]),
                            vec(att_lambda_q2[l]), vec(att_lambda_k2[l]),
                            att_subln_g[l].astype(F32).reshape(1, ATT_V_DIM))

    h1, h1_bf = _outproj_ln(y_ssm, y_att, w_out[l].astype(BF16), xs,
                            ln1_g[l].reshape(1, D_MODEL), ln1_b[l].reshape(1, D_MODEL), 512, 512)

    pad = FFN_TILES_PAD * FFN_TF - D_FF
    conv_params = jnp.pad(jnp.concatenate([ffn_conv_w[l].astype(F32), ffn_conv_b[l].astype(F32)[None]], axis=0),
                          ((0, 0), (0, pad)))
    out = _ffn_ln(h1_bf, _cast_up_tiles(ffn_w_up[l]), conv_params, _cast_down(ffn_w_down[l]), h1,
                  ln2_g[l].reshape(1, D_MODEL), ln2_b[l].reshape(1, D_MODEL), 512)
    return out[None]
```
